```python
import jax, jax.numpy as jnp
from jax import lax
import numpy as np

D_MODEL = 1024
BATCH = 8
SEQ = 4096
DEPTH = 2

D_MIX = D_MODEL
D_FF = 2816
RMS_EPS = 1e-6
D_SSD = D_MIX // 2
SSD_HEAD_DIM = 64
SSD_HEADS = D_SSD // SSD_HEAD_DIM
SSD_GROUPS = 2
SSD_STATE = 128
SSD_CONV = 4
SSD_CHUNK = 128
SSD_CONV_CH = D_SSD + 2 * SSD_GROUPS * SSD_STATE
D_ATT = D_MIX // 4
ATT_HEAD_DIM = 64
ATT_HEADS = D_ATT // ATT_HEAD_DIM
Q_RANK = 256
KV_RANK = 128
IDX_HEADS = 4
IDX_DIM = 64
TOP_K = 256
Q_BLOCK = 128
ROPE_THETA = 500000.0
ROPE_FRACTION_DEN = 4
D_LRU = D_MIX - D_SSD - D_ATT
LRU_BLOCKS = 4
LRU_BLOCK_W = D_LRU // LRU_BLOCKS
LRU_CONV = 4
LRU_C = 8.0
SSD_IN = D_SSD + SSD_CONV_CH + SSD_HEADS
ATT_IN = Q_RANK + KV_RANK + IDX_DIM + IDX_HEADS
LRU_IN = 2 * D_LRU
IN_COLS = SSD_IN + ATT_IN + LRU_IN

kernel_name = 'hymba_style_ssd_dsa_rglru_macaron'


def _in_proj_offsets():
    sizes = [D_SSD, SSD_CONV_CH, SSD_HEADS, Q_RANK, KV_RANK, IDX_DIM, IDX_HEADS, D_LRU, D_LRU]
    return np.cumsum(sizes)[:-1].tolist()


def rms_norm(x, g):
    xf = x.astype(jnp.float32)
    y = xf * lax.rsqrt(jnp.mean(xf * xf, axis=-1, keepdims=True) + RMS_EPS)
    return (y * g.astype(jnp.float32)).astype(x.dtype)


def swiglu(h, w13, w2):
    g, u = jnp.split(h @ w13, 2, axis=-1)
    return (jax.nn.silu(g) * u) @ w2


def causal_dwconv(x, w, b):
    width, ch = w.shape
    y = lax.conv_general_dilated(x, w[:, None, :].astype(x.dtype), window_strides=(1,),
                                 padding=[(width - 1, 0)], dimension_numbers=('NWC', 'WIO', 'NWC'),
                                 feature_group_count=ch)
    return y + b


def rope_tables(length, rot_dim, dtype):
    half = rot_dim // 2
    inv = ROPE_THETA ** (-jnp.arange(half, dtype=jnp.float32) * 2.0 / rot_dim)
    ang = jnp.arange(length, dtype=jnp.float32)[:, None] * inv[None, :]
    return jnp.cos(ang).astype(dtype), jnp.sin(ang).astype(dtype)


def partial_rope(x, cos, sin):
    half = cos.shape[-1]
    x1, x2, rest = x[..., :half], x[..., half:2 * half], x[..., 2 * half:]
    return jnp.concatenate([x1 * cos - x2 * sin, x1 * sin + x2 * cos, rest], axis=-1)


def ssd_mixer(z, xbc, dt_raw, conv_w, conv_b, dt_bias, a_log, d_skip, norm_g):
    bsz, length, _ = z.shape
    nc = length // SSD_CHUNK
    rpg = SSD_HEADS // SSD_GROUPS
    xbc = jax.nn.silu(causal_dwconv(xbc, conv_w, conv_b))
    xs, bm, cm = jnp.split(xbc, [D_SSD, D_SSD + SSD_GROUPS * SSD_STATE], axis=-1)
    xs = xs.reshape(bsz, nc, SSD_CHUNK, SSD_GROUPS, rpg, SSD_HEAD_DIM)
    bm = bm.reshape(bsz, nc, SSD_CHUNK, SSD_GROUPS, SSD_STATE)
    cm = cm.reshape(bsz, nc, SSD_CHUNK, SSD_GROUPS, SSD_STATE)
    dt = jax.nn.softplus((dt_raw + dt_bias).astype(jnp.float32))
    a = -jnp.exp(a_log.astype(jnp.float32))
    adt = (dt * a).reshape(bsz, nc, SSD_CHUNK, SSD_GROUPS, rpg)
    dt = dt.reshape(bsz, nc, SSD_CHUNK, SSD_GROUPS, rpg)
    acum = jnp.moveaxis(jnp.cumsum(adt, axis=2), 2, -1)
    causal = jnp.tril(jnp.ones((SSD_CHUNK, SSD_CHUNK), dtype=bool))
    seg = acum[..., :, None] - acum[..., None, :]
    decay = jnp.exp(jnp.where(causal, seg, -jnp.inf))
    cb = jnp.einsum('bcign,bcjgn->bcgij', cm, bm)
    y_diag = jnp.einsum('bcgij,bcgrij,bcjgr,bcjgrp->bcigrp', cb, decay, dt, xs)
    decay_last = jnp.exp(acum[..., -1:] - acum)
    states = jnp.einsum('bcjgn,bcgrj,bcjgr,bcjgrp->bcgrpn', bm, decay_last, dt, xs)
    chunk_decay = jnp.exp(acum[..., -1])

    def step(h, inp):
        dec, st = inp
        return dec[..., None, None] * h + st, h

    h0 = jnp.zeros((bsz, SSD_GROUPS, rpg, SSD_HEAD_DIM, SSD_STATE), states.dtype)
    _, prev = lax.scan(step, h0, (jnp.moveaxis(chunk_decay, 1, 0), jnp.moveaxis(states, 1, 0)))
    prev = jnp.moveaxis(prev, 0, 1)
    y_off = jnp.einsum('bcign,bcgrpn,bcgri->bcigrp', cm, prev, jnp.exp(acum))
    y = y_diag + y_off + d_skip.reshape(SSD_GROUPS, rpg)[:, :, None] * xs
    y = y.reshape(bsz, length, D_SSD).astype(z.dtype)
    yg = (y * jax.nn.silu(z)).reshape(bsz, length, SSD_GROUPS, D_SSD // SSD_GROUPS)
    return rms_norm(yg, norm_g.reshape(SSD_GROUPS, -1)).reshape(bsz, length, D_SSD)


def dsa_mixer(cq, ckv, kidx, widx, cq_g, ckv_g, w_uq, w_ukv, q_g, k_g, w_qidx, kidx_g):
    bsz, length, _ = cq.shape
    cq = rms_norm(cq, cq_g)
    ckv = rms_norm(ckv, ckv_g)
    q = (cq @ w_uq).reshape(bsz, length, ATT_HEADS, ATT_HEAD_DIM)
    kv = (ckv @ w_ukv).reshape(bsz, length, ATT_HEADS, 2 * ATT_HEAD_DIM)
    k, v = jnp.split(kv, 2, axis=-1)
    cos, sin = rope_tables(length, ATT_HEAD_DIM // ROPE_FRACTION_DEN, q.dtype)
    q = partial_rope(rms_norm(q, q_g), cos[:, None], sin[:, None])
    k = partial_rope(rms_norm(k, k_g), cos[:, None], sin[:, None])
    ci, si = rope_tables(length, IDX_DIM // ROPE_FRACTION_DEN, q.dtype)
    qi = partial_rope((cq @ w_qidx).reshape(bsz, length, IDX_HEADS, IDX_DIM), ci[:, None], si[:, None])
    ki = partial_rope(rms_norm(kidx, kidx_g), ci, si)
    widx = widx * IDX_HEADS ** -0.5
    n_sel = min(TOP_K, length // 4)
    nb = length // Q_BLOCK
    key_pos = jnp.arange(length)
    gather = jax.vmap(lambda tab, ids: tab[ids])

    def to_blocks(t):
        return jnp.moveaxis(t.reshape(bsz, nb, Q_BLOCK, *t.shape[2:]), 1, 0)

    def block(args):
        qb, qib, wb, start = args
        qpos = start + jnp.arange(Q_BLOCK)
        rel = jax.nn.relu(jnp.einsum('bqhd,bsd->bqhs', qib, ki).astype(jnp.float32) * IDX_DIM ** -0.5)
        score = jnp.einsum('bqhs,bqh->bqs', rel, wb.astype(jnp.float32))
        score = jnp.where(key_pos[None, :] <= qpos[:, None], score, -jnp.inf)
        _, idx = lax.top_k(score, n_sel)
        ks = gather(k, idx)
        vs = gather(v, idx)
        logits = jnp.einsum('bqhd,bqkhd->bqhk', qb, ks).astype(jnp.float32) * ATT_HEAD_DIM ** -0.5
        valid = (idx <= qpos[None, :, None])[:, :, None, :]
        p = jax.nn.softmax(jnp.where(valid, logits, -jnp.inf), axis=-1)
        return jnp.einsum('bqhk,bqkhd->bqhd', p.astype(vs.dtype), vs)

    starts = jnp.arange(nb) * Q_BLOCK
    out = lax.map(block, (to_blocks(q), to_blocks(qi), to_blocks(widx), starts))
    return jnp.moveaxis(out, 0, 1).reshape(bsz, length, D_ATT)


def rglru_mixer(xb, gb, conv_w, conv_b, wa, ba, wi, bi, lam):
    bsz, length, _ = xb.shape
    xr = causal_dwconv(xb, conv_w, conv_b)
    xblk = xr.reshape(bsz, length, LRU_BLOCKS, LRU_BLOCK_W)
    r = jax.nn.sigmoid(jnp.einsum('blnc,ncd->blnd', xblk, wa).reshape(bsz, length, D_LRU) + ba)
    i = jax.nn.sigmoid(jnp.einsum('blnc,ncd->blnd', xblk, wi).reshape(bsz, length, D_LRU) + bi)
    log_a = -LRU_C * r.astype(jnp.float32) * jax.nn.softplus(-lam.astype(jnp.float32))
    a = jnp.exp(log_a)
    b = jnp.sqrt(-jnp.expm1(2.0 * log_a)) * (i * xr).astype(jnp.float32)

    def combine(left, right):
        a_l, b_l = left
        a_r, b_r = right
        return a_l * a_r, a_r * b_l + b_r

    _, hs = lax.associative_scan(combine, (a, b), axis=1)
    return hs.astype(xb.dtype) * jax.nn.gelu(gb)


def setup_inputs(seed: int = 0) -> dict:
    key = jax.random.key(seed)
    ks = jax.random.split(key, 32)
    f32 = jnp.float32

    def w(k, shape, fan_in):
        return jax.random.normal(k, shape, f32) * fan_in ** -0.5

    def gain(k, shape):
        return 1.0 + 0.02 * jax.random.normal(k, shape, f32)

    def small(k, shape):
        return 0.02 * jax.random.normal(k, shape, f32)

    L = DEPTH
    dt0 = jnp.exp(jax.random.uniform(ks[8], (L, SSD_HEADS), f32, np.log(1e-3), np.log(1e-1)))
    a_pow = jax.random.uniform(ks[26], (L, D_LRU), f32, 0.9, 0.999)
    sig = a_pow ** (1.0 / LRU_C)
    return {
        'x': jax.random.normal(ks[0], (BATCH, SEQ, D_MODEL), f32),
        'norm_ffn1': gain(ks[1], (L, D_MODEL)),
        'ffn1_w13': w(ks[2], (L, D_MODEL, 2 * D_FF), D_MODEL),
        'ffn1_w2': w(ks[3], (L, D_FF, D_MODEL), D_FF),
        'norm_mix': gain(ks[4], (L, D_MODEL)),
        'w_in': w(ks[5], (L, D_MODEL, IN_COLS), D_MODEL),
        'ssd_conv_w': w(ks[6], (L, SSD_CONV, SSD_CONV_CH), SSD_CONV),
        'ssd_conv_b': small(ks[7], (L, SSD_CONV_CH)),
        'ssd_dt_bias': dt0 + jnp.log(-jnp.expm1(-dt0)),
        'ssd_a_log': jnp.log(jax.random.uniform(ks[9], (L, SSD_HEADS), f32, 1.0, 16.0)),
        'ssd_d': gain(ks[10], (L, SSD_HEADS)),
        'ssd_norm': gain(ks[11], (L, D_SSD)),
        'cq_norm': gain(ks[12], (L, Q_RANK)),
        'ckv_norm': gain(ks[13], (L, KV_RANK)),
        'w_uq': w(ks[14], (L, Q_RANK, ATT_HEADS * ATT_HEAD_DIM), Q_RANK),
        'w_ukv': w(ks[15], (L, KV_RANK, ATT_HEADS * 2 * ATT_HEAD_DIM), KV_RANK),
        'q_norm': gain(ks[16], (L, ATT_HEAD_DIM)),
        'k_norm': gain(ks[17], (L, ATT_HEAD_DIM)),
        'w_qidx': w(ks[18], (L, Q_RANK, IDX_HEADS * IDX_DIM), Q_RANK),
        'kidx_norm': gain(ks[19], (L, IDX_DIM)),
        'lru_conv_w': w(ks[20], (L, LRU_CONV, D_LRU), LRU_CONV),
        'lru_conv_b': small(ks[21], (L, D_LRU)),
        'lru_wa': w(ks[22], (L, LRU_BLOCKS, LRU_BLOCK_W, LRU_BLOCK_W), LRU_BLOCK_W),
        'lru_ba': small(ks[23], (L, D_LRU)),
        'lru_wi': w(ks[24], (L, LRU_BLOCKS, LRU_BLOCK_W, LRU_BLOCK_W), LRU_BLOCK_W),
        'lru_bi': small(ks[25], (L, D_LRU)),
        'lru_lambda': jnp.log(sig / (1.0 - sig)),
        'w_out': w(ks[27], (L, D_MIX, D_MODEL), D_MIX),
        'norm_ffn2': gain(ks[28], (L, D_MODEL)),
        'ffn2_w13': w(ks[29], (L, D_MODEL, 2 * D_FF), D_MODEL),
        'ffn2_w2': w(ks[30], (L, D_FF, D_MODEL), D_FF),
    }


def reference(x, norm_ffn1, ffn1_w13, ffn1_w2, norm_mix, w_in, ssd_conv_w, ssd_conv_b, ssd_dt_bias,
              ssd_a_log, ssd_d, ssd_norm, cq_norm, ckv_norm, w_uq, w_ukv, q_norm, k_norm, w_qidx,
              kidx_norm, lru_conv_w, lru_conv_b, lru_wa, lru_ba, lru_wi, lru_bi, lru_lambda, w_out,
              norm_ffn2, ffn2_w13, ffn2_w2):
    offsets = _in_proj_offsets()
    for l in range(DEPTH):
        x = x + 0.5 * swiglu(rms_norm(x, norm_ffn1[l]), ffn1_w13[l], ffn1_w2[l])
        h = rms_norm(x, norm_mix[l])
        proj = h @ w_in[l]
        z, xbc, dt_raw, cq, ckv, kidx, widx, xl, gl = jnp.split(proj, offsets, axis=-1)
        y_ssd = ssd_mixer(z, xbc, dt_raw, ssd_conv_w[l], ssd_conv_b[l], ssd_dt_bias[l], ssd_a_log[l],
                          ssd_d[l], ssd_norm[l])
        y_att = dsa_mixer(cq, ckv, kidx, widx, cq_norm[l], ckv_norm[l], w_uq[l], w_ukv[l], q_norm[l],
                          k_norm[l], w_qidx[l], kidx_norm[l])
        y_lru = rglru_mixer(xl, gl, lru_conv_w[l], lru_conv_b[l], lru_wa[l], lru_ba[l], lru_wi[l],
                            lru_bi[l], lru_lambda[l])
        mixed = jnp.concatenate([y_ssd, y_att, y_lru], axis=-1)
        x = x + mixed @ w_out[l]
        x = x + 0.5 * swiglu(rms_norm(x, norm_ffn2[l]), ffn2_w13[l], ffn2_w2[l])
    return x
```

```python
import functools

import jax
import jax.numpy as jnp
from jax import lax
from jax.experimental import pallas as pl
from jax.experimental.pallas import tpu as pltpu

F32 = jnp.float32
BF16 = jnp.bfloat16
HIGHEST = lax.Precision.HIGHEST

RMS_EPS = 1e-6
SSD_HEAD_DIM = 64
SSD_GROUPS = 2
SSD_STATE = 128
SSD_CHUNK = 128
ATT_HEADS = 4
ATT_HEAD_DIM = 64
IDX_HEADS = 4
IDX_DIM = 64
TOP_K = 256
Q_BLOCK = 128
ROPE_THETA = 500000.0
ROPE_ROT = 16
LRU_C = 8.0
CONV_W = 4
LANE = 128
SUBLANE = 8
KEY_CHUNK = 512
VMEM_LIMIT = 56 * 1024 * 1024


def _cparams(*sem):
    return pltpu.CompilerParams(dimension_semantics=sem, vmem_limit_bytes=VMEM_LIMIT)


def _const_spec(shape):
    nd = len(shape)
    return pl.BlockSpec(shape, lambda *_: (0,) * nd, pipeline_mode=pl.Buffered(1))


def _rms(x, g):
    ms = jnp.mean(x * x, axis=-1, keepdims=True)
    return x * lax.rsqrt(ms + RMS_EPS) * g


def _softplus(x):
    return jnp.maximum(x, 0.0) + jnp.log1p(jnp.exp(-jnp.abs(x)))


def _silu(x):
    return x * jax.nn.sigmoid(x)


def _ffn_body(x_ref, g_ref, w13_ref, w2_ref, o_ref, *, d_ff, tf):
    x = x_ref[...]
    h = _rms(x, g_ref[...]).astype(BF16)
    acc = jnp.zeros(x.shape, F32)
    for j in range(d_ff // tf):
        g = jnp.dot(h, w13_ref[:, j * tf:(j + 1) * tf], preferred_element_type=F32)
        u = jnp.dot(h, w13_ref[:, d_ff + j * tf:d_ff + (j + 1) * tf], preferred_element_type=F32)
        a = (_silu(g) * u).astype(BF16)
        acc = acc + jnp.dot(a, w2_ref[j * tf:(j + 1) * tf, :], preferred_element_type=F32)
    o_ref[...] = x + 0.5 * acc


def _ffn(x, g, w13, w2, *, tm=512, tf=256):
    t, d = x.shape
    d_ff = w2.shape[0]
    return pl.pallas_call(
        functools.partial(_ffn_body, d_ff=d_ff, tf=tf),
        grid=(t // tm,),
        in_specs=[pl.BlockSpec((tm, d), lambda i: (i, 0)), _const_spec((1, d)),
                  _const_spec((d, 2 * d_ff)), _const_spec((d_ff, d))],
        out_specs=pl.BlockSpec((tm, d), lambda i: (i, 0)),
        out_shape=jax.ShapeDtypeStruct((t, d), F32),
        compiler_params=_cparams("parallel"),
        name="ffn",
    )(x, g.reshape(1, d), w13.astype(BF16), w2.astype(BF16))


def _inproj_body(x_ref, g_ref, w_ref, *o_refs, widths):
    h = _rms(x_ref[...], g_ref[...]).astype(BF16)
    off = 0
    for o_ref, wd in zip(o_refs, widths):
        o_ref[...] = jnp.dot(h, w_ref[:, off:off + wd], preferred_element_type=F32)
        off += wd


def _inproj(x, g, w_pad, widths, *, tm=512):
    t, d = x.shape
    return pl.pallas_call(
        functools.partial(_inproj_body, widths=widths),
        grid=(t // tm,),
        in_specs=[pl.BlockSpec((tm, d), lambda i: (i, 0)), _const_spec((1, d)), _const_spec(w_pad.shape)],
        out_specs=[pl.BlockSpec((tm, wd), lambda i: (i, 0)) for wd in widths],
        out_shape=[jax.ShapeDtypeStruct((t, wd), F32) for wd in widths],
        compiler_params=_cparams("parallel"),
        name="inproj",
    )(x, g.reshape(1, d), w_pad)


def _outproj_body(x_ref, ys_ref, ya_ref, yl_ref, w_ref, o_ref):
    d_s, d_a = ys_ref.shape[1], ya_ref.shape[1]
    acc = jnp.dot(ys_ref[...].astype(BF16), w_ref[0:d_s, :], preferred_element_type=F32)
    acc = acc + jnp.dot(ya_ref[...].astype(BF16), w_ref[d_s:d_s + d_a, :], preferred_element_type=F32)
    acc = acc + jnp.dot(yl_ref[...].astype(BF16), w_ref[d_s + d_a:, :], preferred_element_type=F32)
    o_ref[...] = x_ref[...] + acc


def _outproj(x, ys, ya, yl, w, *, tm=512):
    t, d = x.shape
    row = lambda c: pl.BlockSpec((tm, c), lambda i: (i, 0))
    return pl.pallas_call(
        _outproj_body,
        grid=(t // tm,),
        in_specs=[row(d), row(ys.shape[1]), row(ya.shape[1]), row(yl.shape[1]), _const_spec(w.shape)],
        out_specs=row(d),
        out_shape=jax.ShapeDtypeStruct((t, d), F32),
        compiler_params=_cparams("parallel"),
        name="outproj",
    )(x, ys, ya, yl, w.astype(BF16))


def _causal_conv(x_ref, cw_ref, cb_ref, xe_scr, step, rows):
    @pl.when(step == 0)
    def _():
        xe_scr[0:SUBLANE, :] = jnp.zeros((SUBLANE, xe_scr.shape[1]), F32)

    @pl.when(step > 0)
    def _():
        xe_scr[0:SUBLANE, :] = xe_scr[rows:rows + SUBLANE, :]

    xe_scr[SUBLANE:SUBLANE + rows, :] = x_ref[...]
    y = cb_ref[...] + cw_ref[CONV_W - 1:CONV_W, :] * x_ref[...]
    for w in range(CONV_W - 1):
        off = SUBLANE - (CONV_W - 1) + w
        y = y + cw_ref[w:w + 1, :] * xe_scr[off:off + rows, :]
    return y


def _ssd_body(z_ref, xbc_ref, dtp_ref, cw_ref, cb_ref, dtb_ref, alog_ref, dsk_ref, ng_ref,
              y_ref, xe_scr, st_scr, *, q, d_ssd):
    step = pl.program_id(1)
    n = SSD_STATE
    gw = d_ssd // SSD_GROUPS
    rpg = gw // SSD_HEAD_DIM

    @pl.when(step == 0)
    def _():
        st_scr[...] = jnp.zeros(st_scr.shape, F32)

    xbc = _silu(_causal_conv(xbc_ref, cw_ref, cb_ref, xe_scr, step, q))
    xs = xbc[:, :d_ssd]
    bm = xbc[:, d_ssd:d_ssd + SSD_GROUPS * n]
    cm = xbc[:, d_ssd + SSD_GROUPS * n:]

    dt = _softplus(dtp_ref[...] + dtb_ref[...])
    adt = dt * (-jnp.exp(alog_ref[...]))
    ri = lax.broadcasted_iota(jnp.int32, (q, q), 0)
    ci = lax.broadcasted_iota(jnp.int32, (q, q), 1)
    causal = ri >= ci
    acum = jnp.dot(causal.astype(F32), adt, precision=HIGHEST, preferred_element_type=F32)
    er = lax.broadcasted_iota(jnp.int32, (LANE, d_ssd), 0)
    ec = lax.broadcasted_iota(jnp.int32, (LANE, d_ssd), 1)
    expand = (ec // SSD_HEAD_DIM == er).astype(F32)
    a_x = jnp.dot(acum, expand, precision=HIGHEST, preferred_element_type=F32)
    dt_x = jnp.dot(dt, expand, precision=HIGHEST, preferred_element_type=F32)
    acum_t = acum.T
    a_last = a_x[q - 1:q, :]
    exp_a = jnp.exp(a_x)
    xdt = xs * dt_x
    xsw = xdt * jnp.exp(a_last - a_x)
    cdec = jnp.exp(a_last)

    ys = []
    for g in range(SSD_GROUPS):
        bm_g = bm[:, g * n:(g + 1) * n]
        cm_g = cm[:, g * n:(g + 1) * n].astype(BF16)
        cb = lax.dot_general(cm_g, bm_g.astype(BF16), (((1,), (1,)), ((), ())), preferred_element_type=F32)
        yd = []
        for r in range(rpg):
            h = g * rpg + r
            seg = acum[:, h:h + 1] - acum_t[h:h + 1, :]
            decay = jnp.exp(jnp.where(causal, seg, -jnp.inf))
            m = (cb * decay).astype(BF16)
            xh = xdt[:, h * SSD_HEAD_DIM:(h + 1) * SSD_HEAD_DIM].astype(BF16)
            yd.append(jnp.dot(m, xh, preferred_element_type=F32))
        yd = jnp.concatenate(yd, axis=1)
        gs = slice(g * gw, (g + 1) * gw)
        prev = st_scr[g]
        yoff = jnp.dot(cm_g, prev.astype(BF16), preferred_element_type=F32) * exp_a[:, gs]
        st = jnp.dot(bm_g.T.astype(BF16), xsw[:, gs].astype(BF16), preferred_element_type=F32)
        st_scr[g] = prev * cdec[:, gs] + st
        ys.append(yd + yoff + dsk_ref[:, gs] * xs[:, gs])

    z = z_ref[...]
    for g in range(SSD_GROUPS):
        gs = slice(g * gw, (g + 1) * gw)
        y_ref[:, gs] = _rms(ys[g] * _silu(z[:, gs]), ng_ref[:, gs])


def _ssd(z, xbc, dtp, conv_w, conv_b, dt_bias, a_log, d_skip, norm_g, *, bsz):
    t, d_ssd = z.shape
    cch = xbc.shape[1]
    q = SSD_CHUNK
    nc = t // bsz // q
    heads = d_ssd // SSD_HEAD_DIM
    pad = lambda v, fill: jnp.concatenate([v, jnp.full((LANE - heads,), fill, F32)]).reshape(1, LANE)
    row = lambda c: pl.BlockSpec((q, c), lambda b, i: (b * nc + i, 0))
    return pl.pallas_call(
        functools.partial(_ssd_body, q=q, d_ssd=d_ssd),
        grid=(bsz, nc),
        in_specs=[row(d_ssd), row(cch), row(LANE), _const_spec((CONV_W, cch)), _const_spec((1, cch)),
                  _const_spec((1, LANE)), _const_spec((1, LANE)), _const_spec((1, d_ssd)),
                  _const_spec((1, d_ssd))],
        out_specs=row(d_ssd),
        out_shape=jax.ShapeDtypeStruct((t, d_ssd), F32),
        scratch_shapes=[pltpu.VMEM((q + SUBLANE, cch), F32),
                        pltpu.VMEM((SSD_GROUPS, SSD_STATE, d_ssd // SSD_GROUPS), F32)],
        compiler_params=_cparams("parallel", "arbitrary"),
        name="ssd",
    )(z, xbc, dtp, conv_w, conv_b.reshape(1, cch), pad(dt_bias, 0.0), pad(a_log, 0.0),
      jnp.repeat(d_skip, SSD_HEAD_DIM).reshape(1, d_ssd), norm_g.reshape(1, d_ssd))


def _lru_body(xl_ref, gl_ref, cw_ref, cb_ref, wa_ref, ba_ref, wi_ref, bi_ref, lam_ref,
              y_ref, xe_scr, h_scr, *, tl):
    step = pl.program_id(1)

    @pl.when(step == 0)
    def _():
        h_scr[...] = jnp.zeros(h_scr.shape, F32)

    xr = _causal_conv(xl_ref, cw_ref, cb_ref, xe_scr, step, tl)
    xrb = xr.astype(BF16)
    r = jax.nn.sigmoid(jnp.dot(xrb, wa_ref[...], preferred_element_type=F32) + ba_ref[...])
    i = jax.nn.sigmoid(jnp.dot(xrb, wi_ref[...], preferred_element_type=F32) + bi_ref[...])
    log_a = -LRU_C * r * _softplus(-lam_ref[...])
    a = jnp.exp(log_a)
    b = jnp.sqrt(-jnp.tanh(log_a) * (a * a + 1.0)) * (i * xr)
    rows = lax.broadcasted_iota(jnp.int32, a.shape, 0)
    s = 1
    while s < tl:
        keep = rows >= s
        a_sh = jnp.where(keep, pltpu.roll(a, s, 0), 1.0)
        b_sh = jnp.where(keep, pltpu.roll(b, s, 0), 0.0)
        b = a * b_sh + b
        a = a * a_sh
        s *= 2
    hs = a * h_scr[0:1, :] + b
    h_scr[...] = jnp.broadcast_to(hs[tl - 1:tl, :], h_scr.shape)
    y_ref[...] = hs * jax.nn.gelu(gl_ref[...])


def _block_diag(w):
    nb, bw, _ = w.shape
    out = jnp.zeros((nb * bw, nb * bw), w.dtype)
    for k in range(nb):
        out = out.at[k * bw:(k + 1) * bw, k * bw:(k + 1) * bw].set(w[k])
    return out


def _lru(xl, gl, conv_w, conv_b, wa, ba, wi, bi, lam, *, bsz, tl=256):
    t, dl = xl.shape
    nt = t // bsz // tl
    row = pl.BlockSpec((tl, dl), lambda b, i: (b * nt + i, 0))
    vec = _const_spec((1, dl))
    return pl.pallas_call(
        functools.partial(_lru_body, tl=tl),
        grid=(bsz, nt),
        in_specs=[row, row, _const_spec((CONV_W, dl)), vec, _const_spec((dl, dl)), vec,
                  _const_spec((dl, dl)), vec, vec],
        out_specs=row,
        out_shape=jax.ShapeDtypeStruct((t, dl), F32),
        scratch_shapes=[pltpu.VMEM((tl + SUBLANE, dl), F32), pltpu.VMEM((SUBLANE, dl), F32)],
        compiler_params=_cparams("parallel", "arbitrary"),
        name="rglru",
    )(xl, gl, conv_w, conv_b.reshape(1, dl), _block_diag(wa).astype(BF16), ba.reshape(1, dl),
      _block_diag(wi).astype(BF16), bi.reshape(1, dl), lam.reshape(1, dl))


def _rope(x, ct, st, axis):
    width = x.shape[axis]
    fm = lax.broadcasted_iota(jnp.int32, x.shape, axis) % ATT_HEAD_DIM
    half = ROPE_ROT // 2
    partner = jnp.where(fm < half, pltpu.roll(x, width - half, axis), pltpu.roll(x, half, axis))
    return x * ct + partner * st


def _dot_nt(a, b):
    return lax.dot_general(a, b, (((1,), (1,)), ((), ())), preferred_element_type=F32)


def _prep_body(cq_ref, ckv_ref, misc_ref, ct_ref, st_ref, ctt_ref, stt_ref, cqg_ref, ckvg_ref, wuqt_ref,
               wuk_ref, wuvt_ref, qg_ref, kg_ref, wqit_ref, kig_ref,
               k_ref, ki_ref, vt_ref, qt_ref, qit_ref, wc_ref, *, tm):
    ct = ct_ref[...]
    st = st_ref[...]
    d_att = wuk_ref.shape[1]
    gr = lax.broadcasted_iota(jnp.int32, (d_att, d_att), 0) // ATT_HEAD_DIM
    gc = lax.broadcasted_iota(jnp.int32, (d_att, d_att), 1) // ATT_HEAD_DIM
    head_mean = jnp.where(gr == gc, 1.0 / ATT_HEAD_DIM, 0.0).astype(F32)

    cqn = _rms(cq_ref[...], cqg_ref[...]).astype(BF16)
    ckvn = _rms(ckv_ref[...], ckvg_ref[...]).astype(BF16)

    qt = _dot_nt(wuqt_ref[...], cqn)
    ms = jnp.dot(head_mean, qt * qt, precision=HIGHEST, preferred_element_type=F32)
    qt = qt * lax.rsqrt(ms + RMS_EPS) * qg_ref[...]
    qt_ref[...] = (_rope(qt, ctt_ref[...], stt_ref[...], 0) * ATT_HEAD_DIM ** -0.5).astype(BF16)
    vt_ref[...] = _dot_nt(wuvt_ref[...], ckvn).astype(BF16)

    k = jnp.dot(ckvn, wuk_ref[...], preferred_element_type=F32)
    ms = jnp.dot(k * k, head_mean, precision=HIGHEST, preferred_element_type=F32)
    k_ref[...] = _rope(k * lax.rsqrt(ms + RMS_EPS) * kg_ref[...], ct, st, 1).astype(BF16)

    qit = _rope(_dot_nt(wqit_ref[...], cqn), ctt_ref[...], stt_ref[...], 0)
    misc = misc_ref[...]
    misc_t = misc.T
    wscale = IDX_HEADS ** -0.5 * IDX_DIM ** -0.5
    for j in range(tm // Q_BLOCK):
        cols = slice(j * Q_BLOCK, (j + 1) * Q_BLOCK)
        for h in range(IDX_HEADS):
            qit_ref[j, 0:IDX_DIM, h * Q_BLOCK:(h + 1) * Q_BLOCK] = qit[h * IDX_DIM:(h + 1) * IDX_DIM, cols]
        qit_ref[j, IDX_DIM:, :] = jnp.zeros((LANE - IDX_DIM, IDX_HEADS * Q_BLOCK), F32)
        wrow = jnp.concatenate([misc_t[IDX_DIM + h:IDX_DIM + h + 1, cols] for h in range(IDX_HEADS)], axis=1)
        wc_ref[j] = jnp.broadcast_to(wrow * wscale, (SUBLANE, IDX_HEADS * Q_BLOCK))

    lane = lax.broadcasted_iota(jnp.int32, misc.shape, 1)
    km = jnp.where(lane < IDX_DIM, misc, 0.0)
    ms = jnp.sum(km * km, axis=-1, keepdims=True) * (1.0 / IDX_DIM)
    kin = km * lax.rsqrt(ms + RMS_EPS) * kig_ref[...]
    ki_ref[...] = _rope(kin, ct[:, :LANE], st[:, :LANE], 1)


def _rope_lane_tables(length, width):
    half = ROPE_ROT // 2
    inv = ROPE_THETA ** (-jnp.arange(half, dtype=F32) * 2.0 / ROPE_ROT)
    ang = jnp.arange(length, dtype=F32)[:, None] * inv[None, :]
    cos, sin = jnp.cos(ang), jnp.sin(ang)
    ones = jnp.ones((length, ATT_HEAD_DIM - ROPE_ROT), F32)
    ct = jnp.concatenate([cos, cos, ones], axis=1)
    st = jnp.concatenate([-sin, sin, 0.0 * ones], axis=1)
    reps = width // ATT_HEAD_DIM
    return jnp.tile(ct, (1, reps)), jnp.tile(st, (1, reps))


def _dsa_prep(cq, ckv, misc, p, *, bsz):
    t, qr = cq.shape
    kvr = ckv.shape[1]
    tm = KEY_CHUNK
    length = t // bsz
    nt = length // tm
    d_att = ATT_HEADS * ATT_HEAD_DIM
    ct, st = _rope_lane_tables(length, d_att)
    wukv = p["w_ukv"].reshape(kvr, ATT_HEADS, 2, ATT_HEAD_DIM)
    wuk = wukv[:, :, 0, :].reshape(kvr, d_att).astype(BF16)
    wuvt = wukv[:, :, 1, :].reshape(kvr, d_att).T.astype(BF16)
    kig = jnp.concatenate([p["kidx_norm"], jnp.zeros((LANE - IDX_DIM,), F32)]).reshape(1, LANE)
    row = lambda c: pl.BlockSpec((tm, c), lambda b, i: (b * nt + i, 0))
    tab = pl.BlockSpec((tm, d_att), lambda b, i: (i, 0))
    tab_t = pl.BlockSpec((d_att, tm), lambda b, i: (0, i))
    nqb = tm // Q_BLOCK
    hq = IDX_HEADS * Q_BLOCK
    return pl.pallas_call(
        functools.partial(_prep_body, tm=tm),
        grid=(bsz, nt),
        in_specs=[row(qr), row(kvr), row(LANE), tab, tab, tab_t, tab_t, _const_spec((1, qr)),
                  _const_spec((1, kvr)), _const_spec((d_att, qr)), _const_spec((kvr, d_att)),
                  _const_spec((d_att, kvr)), _const_spec((d_att, 1)), _const_spec((1, d_att)),
                  _const_spec((IDX_HEADS * IDX_DIM, qr)), _const_spec((1, LANE))],
        out_specs=[row(d_att), row(LANE),
                   pl.BlockSpec((None, d_att, tm), lambda b, i: (b * nt + i, 0, 0)),
                   pl.BlockSpec((None, d_att, tm), lambda b, i: (b * nt + i, 0, 0)),
                   pl.BlockSpec((nqb, LANE, hq), lambda b, i: (b * nt + i, 0, 0)),
                   pl.BlockSpec((nqb, SUBLANE, hq), lambda b, i: (b * nt + i, 0, 0))],
        out_shape=[jax.ShapeDtypeStruct((t, d_att), BF16),
                   jax.ShapeDtypeStruct((t, LANE), F32),
                   jax.ShapeDtypeStruct((bsz * nt, d_att, tm), BF16),
                   jax.ShapeDtypeStruct((bsz * nt, d_att, tm), BF16),
                   jax.ShapeDtypeStruct((t // Q_BLOCK, LANE, hq), F32),
                   jax.ShapeDtypeStruct((t // Q_BLOCK, SUBLANE, hq), F32)],
        compiler_params=_cparams("parallel", "parallel"),
        name="dsa_prep",
    )(cq, ckv, misc, ct, st, ct.T, st.T, p["cq_norm"].reshape(1, qr), p["ckv_norm"].reshape(1, kvr),
      p["w_uq"].T.astype(BF16), wuk, wuvt, jnp.tile(p["q_norm"], ATT_HEADS).reshape(d_att, 1),
      jnp.tile(p["k_norm"], ATT_HEADS).reshape(1, d_att), p["w_qidx"].T.astype(BF16), kig)


def _order_key_to_float(u):
    sk = u ^ jnp.int32(-2 ** 31)
    fb = sk ^ ((sk >> 31) & jnp.int32(0x7FFFFFFF))
    return lax.bitcast_convert_type(fb, F32)


def _dsa_body(qit_ref, wc_ref, qt_ref, ki_ref, k_ref, vt_ref, o_ref,
              sc_scr, bias_scr, lg_scr, qbd_scr, oacc_scr, *, n_sel):
    kc = KEY_CHUNK
    qb = pl.program_id(1)
    nk = qb // (kc // Q_BLOCK) + 1
    qpos = qb * Q_BLOCK + lax.broadcasted_iota(jnp.int32, (1, Q_BLOCK), 1)
    hq = IDX_HEADS * Q_BLOCK

    def chunk(c):
        return pl.ds(pl.multiple_of(c * kc, kc), kc)

    qit = qit_ref[...]
    wc = wc_ref[0:1, :]

    def score_chunk(c, carry):
        s4 = jnp.dot(ki_ref[chunk(c), :], qit, preferred_element_type=F32)
        s4 = jnp.maximum(s4, 0.0) * wc
        s = (s4[:, 0:Q_BLOCK] + s4[:, Q_BLOCK:2 * Q_BLOCK]) + (s4[:, 2 * Q_BLOCK:3 * Q_BLOCK] + s4[:, 3 * Q_BLOCK:])
        kpos = c * kc + lax.broadcasted_iota(jnp.int32, (kc, Q_BLOCK), 0)
        sc_scr[chunk(c), :] = jnp.where(kpos <= qpos, s, -jnp.inf)
        return carry

    lax.fori_loop(0, nk, score_chunk, 0)

    def count_ge(thr):
        def body(c, acc):
            ind = jnp.where(sc_scr[chunk(c), :] >= thr, 1.0, 0.0)
            return acc + jnp.sum(ind.reshape(kc // SUBLANE, SUBLANE, Q_BLOCK), axis=0)
        acc = lax.fori_loop(0, nk, body, jnp.zeros((SUBLANE, Q_BLOCK), F32))
        return jnp.sum(acc, axis=0, keepdims=True)

    def bisect(i, t):
        cand = t | (jnp.int32(1) << (31 - i))
        return jnp.where(count_ge(_order_key_to_float(cand)) >= n_sel, cand, t)

    t_key = lax.fori_loop(0, 32, bisect, jnp.zeros((1, Q_BLOCK), jnp.int32))
    lo = _order_key_to_float(t_key)
    hi = _order_key_to_float(t_key + 1)
    need = n_sel - count_ge(hi)
    take_all = (qpos + 1) <= n_sel

    ri = lax.broadcasted_iota(jnp.int32, (kc, kc), 0)
    ci = lax.broadcasted_iota(jnp.int32, (kc, kc), 1)
    before = (ri > ci).astype(BF16)

    def select_chunk(c, seen):
        blk = sc_scr[chunk(c), :]
        in_hi = blk >= hi
        tie = (blk >= lo) & jnp.logical_not(in_hi)
        tie_f = jnp.where(tie, 1.0, 0.0)
        rank = jnp.dot(before, tie_f.astype(BF16), preferred_element_type=F32) + seen
        sel = in_hi | (tie & (rank < need))
        sel = (take_all & (blk > -jnp.inf)) | (jnp.logical_not(take_all) & sel)
        bias_scr[chunk(c), :] = jnp.where(sel, 0.0, -jnp.inf)
        return seen + jnp.sum(tie_f, axis=0, keepdims=True)

    lax.fori_loop(0, nk, select_chunk, jnp.zeros((1, Q_BLOCK), F32))

    qbd_scr[...] = jnp.zeros(qbd_scr.shape, BF16)
    for h in range(ATT_HEADS):
        rows = slice(h * ATT_HEAD_DIM, (h + 1) * ATT_HEAD_DIM)
        qbd_scr[rows, h * Q_BLOCK:(h + 1) * Q_BLOCK] = qt_ref[rows, :]
    qbd = qbd_scr[...]

    def logit_chunk(c, m):
        lg = jnp.dot(k_ref[chunk(c), :], qbd, preferred_element_type=F32)
        b = bias_scr[chunk(c), :]
        lg = lg + jnp.concatenate([b] * ATT_HEADS, axis=1)
        lg_scr[chunk(c), :] = lg
        return jnp.maximum(m, jnp.max(lg, axis=0, keepdims=True))

    m = lax.fori_loop(0, nk, logit_chunk, jnp.full((1, hq), -jnp.inf, F32))
    oacc_scr[...] = jnp.zeros(oacc_scr.shape, F32)

    def value_chunk(c, ssum):
        p = jnp.exp(lg_scr[chunk(c), :] - m)
        oacc_scr[...] += jnp.dot(vt_ref[c], p.astype(BF16), preferred_element_type=F32)
        return ssum + jnp.sum(p, axis=0, keepdims=True)

    ssum = lax.fori_loop(0, nk, value_chunk, jnp.zeros((1, hq), F32))
    outs = []
    for h in range(ATT_HEADS):
        cols = slice(h * Q_BLOCK, (h + 1) * Q_BLOCK)
        outs.append(oacc_scr[h * ATT_HEAD_DIM:(h + 1) * ATT_HEAD_DIM, cols] / ssum[:, cols])
    o_ref[...] = jnp.concatenate(outs, axis=0).T


def _dsa(k, ki, vt, qt, qit, wc, *, bsz):
    t, d_att = k.shape
    length = t // bsz
    nb = length // Q_BLOCK
    kc = KEY_CHUNK
    nkc = length // kc
    hq = IDX_HEADS * Q_BLOCK
    per_q = kc // Q_BLOCK
    vt = vt.reshape(bsz, nkc, d_att, kc)
    return pl.pallas_call(
        functools.partial(_dsa_body, n_sel=min(TOP_K, length // 4)),
        grid=(bsz, nb),
        in_specs=[pl.BlockSpec((None, LANE, hq), lambda b, i: (b * nb + i, 0, 0)),
                  pl.BlockSpec((None, SUBLANE, hq), lambda b, i: (b * nb + i, 0, 0)),
                  pl.BlockSpec((None, d_att, Q_BLOCK), lambda b, i: (b * nkc + i // per_q, 0, i % per_q)),
                  pl.BlockSpec((length, LANE), lambda b, i: (b, 0)),
                  pl.BlockSpec((length, d_att), lambda b, i: (b, 0)),
                  pl.BlockSpec((None, nkc, d_att, kc), lambda b, i: (b, 0, 0, 0))],
        out_specs=pl.BlockSpec((Q_BLOCK, d_att), lambda b, i: (b * nb + i, 0)),
        out_shape=jax.ShapeDtypeStruct((t, d_att), F32),
        scratch_shapes=[pltpu.VMEM((length, Q_BLOCK), F32), pltpu.VMEM((length, Q_BLOCK), F32),
                        pltpu.VMEM((length, hq), F32), pltpu.VMEM((d_att, hq), BF16),
                        pltpu.VMEM((d_att, hq), F32)],
        compiler_params=_cparams("parallel", "arbitrary"),
        name="dsa",
    )(qit, wc, qt, ki, k, vt)


def _pad_cols(w, width):
    return jnp.concatenate([w, jnp.zeros((w.shape[0], width - w.shape[1]), w.dtype)], axis=1)


def kernel(x, norm_ffn1, ffn1_w13, ffn1_w2, norm_mix, w_in, ssd_conv_w, ssd_conv_b, ssd_dt_bias, ssd_a_log, ssd_d, ssd_norm, cq_norm, ckv_norm, w_uq, w_ukv, q_norm, k_norm, w_qidx, kidx_norm, lru_conv_w, lru_conv_b, lru_wa, lru_ba, lru_wi, lru_bi, lru_lambda, w_out, norm_ffn2, ffn2_w13, ffn2_w2):
    bsz, length, d = x.shape
    depth = w_in.shape[0]
    d_ssd = ssd_norm.shape[1]
    conv_ch = ssd_conv_b.shape[1]
    heads = ssd_d.shape[1]
    q_rank, kv_rank = cq_norm.shape[1], ckv_norm.shape[1]
    d_lru = lru_lambda.shape[1]
    sizes = [d_ssd, conv_ch, heads, q_rank, kv_rank, IDX_DIM, IDX_HEADS, d_lru, d_lru]
    offs = [0]
    for s in sizes:
        offs.append(offs[-1] + s)
    col = lambda wl, k: wl[:, offs[k]:offs[k + 1]]
    widths = (d_ssd, conv_ch, LANE, q_rank, kv_rank, LANE, d_lru, d_lru)

    xt = x.reshape(bsz * length, d)
    for l in range(depth):
        xt = _ffn(xt, norm_ffn1[l], ffn1_w13[l], ffn1_w2[l])
        wl = w_in[l]
        w_pad = jnp.concatenate(
            [col(wl, 0), col(wl, 1), _pad_cols(col(wl, 2), LANE), col(wl, 3), col(wl, 4),
             _pad_cols(jnp.concatenate([col(wl, 5), col(wl, 6)], axis=1), LANE), col(wl, 7), col(wl, 8)],
            axis=1).astype(BF16)
        z, xbc, dtp, cq, ckv, misc, xl, gl = _inproj(xt, norm_mix[l], w_pad, widths)
        y_ssd = _ssd(z, xbc, dtp, ssd_conv_w[l], ssd_conv_b[l], ssd_dt_bias[l], ssd_a_log[l], ssd_d[l],
                     ssd_norm[l], bsz=bsz)
        y_lru = _lru(xl, gl, lru_conv_w[l], lru_conv_b[l], lru_wa[l], lru_ba[l], lru_wi[l], lru_bi[l],
                     lru_lambda[l], bsz=bsz)
        prm = dict(cq_norm=cq_norm[l], ckv_norm=ckv_norm[l], w_uq=w_uq[l], w_ukv=w_ukv[l], q_norm=q_norm[l],
                   k_norm=k_norm[l], w_qidx=w_qidx[l], kidx_norm=kidx_norm[l])
        k, ki, vt, qt, qit, wc = _dsa_prep(cq, ckv, misc, prm, bsz=bsz)
        y_att = _dsa(k, ki, vt, qt, qit, wc, bsz=bsz)
        xt = _outproj(xt, y_ssd, y_att, y_lru, w_out[l])
        xt = _ffn(xt, norm_ffn2[l], ffn2_w13[l], ffn2_w2[l])
    return xt.reshape(bsz, length, d)
```

```python
import functools

import jax
import jax.numpy as jnp
from jax import lax
from jax.experimental import pallas as pl
from jax.experimental.pallas import tpu as pltpu

F32 = jnp.float32
BF16 = jnp.bfloat16
HIGHEST = lax.Precision.HIGHEST

RMS_EPS = 1e-6
SSD_HEAD_DIM = 64
SSD_GROUPS = 2
SSD_STATE = 128
SSD_CHUNK = 128
ATT_HEADS = 4
ATT_HEAD_DIM = 64
IDX_HEADS = 4
IDX_DIM = 64
TOP_K = 256
Q_BLOCK = 128
ROPE_THETA = 500000.0
ROPE_ROT = 16
LRU_C = 8.0
CONV_W = 4
LANE = 128
SUBLANE = 8
KEY_CHUNK = 512
VMEM_LIMIT = 56 * 1024 * 1024


def _cparams(*sem):
    return pltpu.CompilerParams(dimension_semantics=sem, vmem_limit_bytes=VMEM_LIMIT)


def _const_spec(shape):
    nd = len(shape)
    return pl.BlockSpec(shape, lambda *_: (0,) * nd, pipeline_mode=pl.Buffered(1))


def _rms(x, g):
    ms = jnp.mean(x * x, axis=-1, keepdims=True)
    return x * lax.rsqrt(ms + RMS_EPS) * g


def _softplus(x):
    return jnp.maximum(x, 0.0) + jnp.log1p(jnp.exp(-jnp.abs(x)))


def _silu(x):
    return x * jax.nn.sigmoid(x)


def _ffn_body(x_ref, g_ref, w13_ref, w2_ref, o_ref, *, d_ff, tf):
    x = x_ref[...]
    h = _rms(x, g_ref[...]).astype(BF16)
    acc = jnp.zeros(x.shape, F32)
    for j in range(d_ff // tf):
        g = jnp.dot(h, w13_ref[:, j * tf:(j + 1) * tf], preferred_element_type=F32)
        u = jnp.dot(h, w13_ref[:, d_ff + j * tf:d_ff + (j + 1) * tf], preferred_element_type=F32)
        a = (_silu(g) * u).astype(BF16)
        acc = acc + jnp.dot(a, w2_ref[j * tf:(j + 1) * tf, :], preferred_element_type=F32)
    o_ref[...] = x + 0.5 * acc


def _ffn(x, g, w13, w2, *, tm=512, tf=256):
    t, d = x.shape
    d_ff = w2.shape[0]
    return pl.pallas_call(
        functools.partial(_ffn_body, d_ff=d_ff, tf=tf),
        grid=(t // tm,),
        in_specs=[pl.BlockSpec((tm, d), lambda i: (i, 0)), _const_spec((1, d)),
                  _const_spec((d, 2 * d_ff)), _const_spec((d_ff, d))],
        out_specs=pl.BlockSpec((tm, d), lambda i: (i, 0)),
        out_shape=jax.ShapeDtypeStruct((t, d), F32),
        compiler_params=_cparams("parallel"),
        name="ffn",
    )(x, g.reshape(1, d), w13.astype(BF16), w2.astype(BF16))


def _inproj_body(x_ref, g_ref, w_ref, *o_refs, widths):
    h = _rms(x_ref[...], g_ref[...]).astype(BF16)
    off = 0
    for o_ref, wd in zip(o_refs, widths):
        o_ref[...] = jnp.dot(h, w_ref[:, off:off + wd], preferred_element_type=F32)
        off += wd


def _inproj(x, g, w_pad, widths, *, tm=512):
    t, d = x.shape
    return pl.pallas_call(
        functools.partial(_inproj_body, widths=widths),
        grid=(t // tm,),
        in_specs=[pl.BlockSpec((tm, d), lambda i: (i, 0)), _const_spec((1, d)), _const_spec(w_pad.shape)],
        out_specs=[pl.BlockSpec((tm, wd), lambda i: (i, 0)) for wd in widths],
        out_shape=[jax.ShapeDtypeStruct((t, wd), F32) for wd in widths],
        compiler_params=_cparams("parallel"),
        name="inproj",
    )(x, g.reshape(1, d), w_pad)


def _outproj_body(x_ref, ys_ref, ya_ref, yl_ref, w_ref, o_ref):
    d_s, d_a = ys_ref.shape[1], ya_ref.shape[1]
    acc = jnp.dot(ys_ref[...].astype(BF16), w_ref[0:d_s, :], preferred_element_type=F32)
    acc = acc + jnp.dot(ya_ref[...].astype(BF16), w_ref[d_s:d_s + d_a, :], preferred_element_type=F32)
    acc = acc + jnp.dot(yl_ref[...].astype(BF16), w_ref[d_s + d_a:, :], preferred_element_type=F32)
    o_ref[...] = x_ref[...] + acc


def _outproj(x, ys, ya, yl, w, *, tm=512):
    t, d = x.shape
    row = lambda c: pl.BlockSpec((tm, c), lambda i: (i, 0))
    return pl.pallas_call(
        _outproj_body,
        grid=(t // tm,),
        in_specs=[row(d), row(ys.shape[1]), row(ya.shape[1]), row(yl.shape[1]), _const_spec(w.shape)],
        out_specs=row(d),
        out_shape=jax.ShapeDtypeStruct((t, d), F32),
        compiler_params=_cparams("parallel"),
        name="outproj",
    )(x, ys, ya, yl, w.astype(BF16))


def _causal_conv(x_ref, cw_ref, cb_ref, xe_scr, step, rows):
    @pl.when(step == 0)
    def _():
        xe_scr[0:SUBLANE, :] = jnp.zeros((SUBLANE, xe_scr.shape[1]), F32)

    @pl.when(step > 0)
    def _():
        xe_scr[0:SUBLANE, :] = xe_scr[rows:rows + SUBLANE, :]

    xe_scr[SUBLANE:SUBLANE + rows, :] = x_ref[...]
    y = cb_ref[...] + cw_ref[CONV_W - 1:CONV_W, :] * x_ref[...]
    for w in range(CONV_W - 1):
        off = SUBLANE - (CONV_W - 1) + w
        y = y + cw_ref[w:w + 1, :] * xe_scr[off:off + rows, :]
    return y


def _ssd_body(z_ref, xbc_ref, dtp_ref, cw_ref, cb_ref, dtb_ref, alog_ref, dsk_ref, ng_ref,
              y_ref, xe_scr, st_scr, *, q, d_ssd):
    step = pl.program_id(1)
    n = SSD_STATE
    gw = d_ssd // SSD_GROUPS
    rpg = gw // SSD_HEAD_DIM

    @pl.when(step == 0)
    def _():
        st_scr[...] = jnp.zeros(st_scr.shape, F32)

    xbc = _silu(_causal_conv(xbc_ref, cw_ref, cb_ref, xe_scr, step, q))
    xs = xbc[:, :d_ssd]
    bm = xbc[:, d_ssd:d_ssd + SSD_GROUPS * n]
    cm = xbc[:, d_ssd + SSD_GROUPS * n:]

    dt = _softplus(dtp_ref[...] + dtb_ref[...])
    adt = dt * (-jnp.exp(alog_ref[...]))
    ri = lax.broadcasted_iota(jnp.int32, (q, q), 0)
    ci = lax.broadcasted_iota(jnp.int32, (q, q), 1)
    causal = ri >= ci
    acum = jnp.dot(causal.astype(F32), adt, precision=HIGHEST, preferred_element_type=F32)
    er = lax.broadcasted_iota(jnp.int32, (LANE, d_ssd), 0)
    ec = lax.broadcasted_iota(jnp.int32, (LANE, d_ssd), 1)
    expand = (ec // SSD_HEAD_DIM == er).astype(F32)
    a_x = jnp.dot(acum, expand, precision=HIGHEST, preferred_element_type=F32)
    dt_x = jnp.dot(dt, expand, precision=HIGHEST, preferred_element_type=F32)
    acum_t = acum.T
    a_last = a_x[q - 1:q, :]
    exp_a = jnp.exp(a_x)
    xdt = xs * dt_x
    xsw = xdt * jnp.exp(a_last - a_x)
    cdec = jnp.exp(a_last)

    ys = []
    for g in range(SSD_GROUPS):
        bm_g = bm[:, g * n:(g + 1) * n]
        cm_g = cm[:, g * n:(g + 1) * n].astype(BF16)
        cb = lax.dot_general(cm_g, bm_g.astype(BF16), (((1,), (1,)), ((), ())), preferred_element_type=F32)
        yd = []
        for r in range(rpg):
            h = g * rpg + r
            seg = acum[:, h:h + 1] - acum_t[h:h + 1, :]
            decay = jnp.exp(jnp.where(causal, seg, -jnp.inf))
            m = (cb * decay).astype(BF16)
            xh = xdt[:, h * SSD_HEAD_DIM:(h + 1) * SSD_HEAD_DIM].astype(BF16)
            yd.append(jnp.dot(m, xh, preferred_element_type=F32))
        yd = jnp.concatenate(yd, axis=1)
        gs = slice(g * gw, (g + 1) * gw)
        prev = st_scr[g]
        yoff = jnp.dot(cm_g, prev.astype(BF16), preferred_element_type=F32) * exp_a[:, gs]
        st = jnp.dot(bm_g.T.astype(BF16), xsw[:, gs].astype(BF16), preferred_element_type=F32)
        st_scr[g] = prev * cdec[:, gs] + st
        ys.append(yd + yoff + dsk_ref[:, gs] * xs[:, gs])

    z = z_ref[...]
    for g in range(SSD_GROUPS):
        gs = slice(g * gw, (g + 1) * gw)
        y_ref[:, gs] = _rms(ys[g] * _silu(z[:, gs]), ng_ref[:, gs])


def _ssd(z, xbc, dtp, conv_w, conv_b, dt_bias, a_log, d_skip, norm_g, *, bsz):
    t, d_ssd = z.shape
    cch = xbc.shape[1]
    q = SSD_CHUNK
    nc = t // bsz // q
    heads = d_ssd // SSD_HEAD_DIM
    pad = lambda v, fill: jnp.concatenate([v, jnp.full((LANE - heads,), fill, F32)]).reshape(1, LANE)
    row = lambda c: pl.BlockSpec((q, c), lambda b, i: (b * nc + i, 0))
    return pl.pallas_call(
        functools.partial(_ssd_body, q=q, d_ssd=d_ssd),
        grid=(bsz, nc),
        in_specs=[row(d_ssd), row(cch), row(LANE), _const_spec((CONV_W, cch)), _const_spec((1, cch)),
                  _const_spec((1, LANE)), _const_spec((1, LANE)), _const_spec((1, d_ssd)),
                  _const_spec((1, d_ssd))],
        out_specs=row(d_ssd),
        out_shape=jax.ShapeDtypeStruct((t, d_ssd), F32),
        scratch_shapes=[pltpu.VMEM((q + SUBLANE, cch), F32),
                        pltpu.VMEM((SSD_GROUPS, SSD_STATE, d_ssd // SSD_GROUPS), F32)],
        compiler_params=_cparams("parallel", "arbitrary"),
        name="ssd",
    )(z, xbc, dtp, conv_w, conv_b.reshape(1, cch), pad(dt_bias, 0.0), pad(a_log, 0.0),
      jnp.repeat(d_skip, SSD_HEAD_DIM).reshape(1, d_ssd), norm_g.reshape(1, d_ssd))


def _lru_body(xl_ref, gl_ref, cw_ref, cb_ref, wa_ref, ba_ref, wi_ref, bi_ref, lam_ref,
              y_ref, xe_scr, h_scr, *, tl):
    step = pl.program_id(1)

    @pl.when(step == 0)
    def _():
        h_scr[...] = jnp.zeros(h_scr.shape, F32)

    xr = _causal_conv(xl_ref, cw_ref, cb_ref, xe_scr, step, tl)
    xrb = xr.astype(BF16)
    r = jax.nn.sigmoid(jnp.dot(xrb, wa_ref[...], preferred_element_type=F32) + ba_ref[...])
    i = jax.nn.sigmoid(jnp.dot(xrb, wi_ref[...], preferred_element_type=F32) + bi_ref[...])
    log_a = -LRU_C * r * _softplus(-lam_ref[...])
    a = jnp.exp(log_a)
    b = jnp.sqrt(-jnp.tanh(log_a) * (a * a + 1.0)) * (i * xr)
    rows = lax.broadcasted_iota(jnp.int32, a.shape, 0)
    s = 1
    while s < tl:
        keep = rows >= s
        a_sh = jnp.where(keep, pltpu.roll(a, s, 0), 1.0)
        b_sh = jnp.where(keep, pltpu.roll(b, s, 0), 0.0)
        b = a * b_sh + b
        a = a * a_sh
        s *= 2
    hs = a * h_scr[0:1, :] + b
    h_scr[...] = jnp.broadcast_to(hs[tl - 1:tl, :], h_scr.shape)
    y_ref[...] = hs * jax.nn.gelu(gl_ref[...])


def _block_diag(w):
    nb, bw, _ = w.shape
    out = jnp.zeros((nb * bw, nb * bw), w.dtype)
    for k in range(nb):
        out = out.at[k * bw:(k + 1) * bw, k * bw:(k + 1) * bw].set(w[k])
    return out


def _lru(xl, gl, conv_w, conv_b, wa, ba, wi, bi, lam, *, bsz, tl=256):
    t, dl = xl.shape
    nt = t // bsz // tl
    row = pl.BlockSpec((tl, dl), lambda b, i: (b * nt + i, 0))
    vec = _const_spec((1, dl))
    return pl.pallas_call(
        functools.partial(_lru_body, tl=tl),
        grid=(bsz, nt),
        in_specs=[row, row, _const_spec((CONV_W, dl)), vec, _const_spec((dl, dl)), vec,
                  _const_spec((dl, dl)), vec, vec],
        out_specs=row,
        out_shape=jax.ShapeDtypeStruct((t, dl), F32),
        scratch_shapes=[pltpu.VMEM((tl + SUBLANE, dl), F32), pltpu.VMEM((SUBLANE, dl), F32)],
        compiler_params=_cparams("parallel", "arbitrary"),
        name="rglru",
    )(xl, gl, conv_w, conv_b.reshape(1, dl), _block_diag(wa).astype(BF16), ba.reshape(1, dl),
      _block_diag(wi).astype(BF16), bi.reshape(1, dl), lam.reshape(1, dl))


def _rope(x, ct, st, axis):
    width = x.shape[axis]
    fm = lax.broadcasted_iota(jnp.int32, x.shape, axis) % ATT_HEAD_DIM
    half = ROPE_ROT // 2
    partner = jnp.where(fm < half, pltpu.roll(x, width - half, axis), pltpu.roll(x, half, axis))
    return x * ct + partner * st


def _dot_nt(a, b):
    return lax.dot_general(a, b, (((1,), (1,)), ((), ())), preferred_element_type=F32)


def _prep_body(cq_ref, ckv_ref, misc_ref, ct_ref, st_ref, ctt_ref, stt_ref, cqg_ref, ckvg_ref, wuqt_ref,
               wuk_ref, wuvt_ref, qg_ref, kg_ref, wqit_ref, kig_ref,
               k_ref, ki_ref, vt_ref, qt_ref, qit_ref, wc_ref, *, tm):
    ct = ct_ref[...]
    st = st_ref[...]
    d_att = wuk_ref.shape[1]
    gr = lax.broadcasted_iota(jnp.int32, (d_att, d_att), 0) // ATT_HEAD_DIM
    gc = lax.broadcasted_iota(jnp.int32, (d_att, d_att), 1) // ATT_HEAD_DIM
    head_mean = jnp.where(gr == gc, 1.0 / ATT_HEAD_DIM, 0.0).astype(F32)

    cqn = _rms(cq_ref[...], cqg_ref[...]).astype(BF16)
    ckvn = _rms(ckv_ref[...], ckvg_ref[...]).astype(BF16)

    qt = _dot_nt(wuqt_ref[...], cqn)
    ms = jnp.dot(head_mean, qt * qt, precision=HIGHEST, preferred_element_type=F32)
    qt = qt * lax.rsqrt(ms + RMS_EPS) * qg_ref[...]
    qt_ref[...] = (_rope(qt, ctt_ref[...], stt_ref[...], 0) * ATT_HEAD_DIM ** -0.5).astype(BF16)
    vt_ref[...] = _dot_nt(wuvt_ref[...], ckvn).astype(BF16)

    k = jnp.dot(ckvn, wuk_ref[...], preferred_element_type=F32)
    ms = jnp.dot(k * k, head_mean, precision=HIGHEST, preferred_element_type=F32)
    k_ref[...] = _rope(k * lax.rsqrt(ms + RMS_EPS) * kg_ref[...], ct, st, 1).astype(BF16)

    qit = _rope(_dot_nt(wqit_ref[...], cqn), ctt_ref[...], stt_ref[...], 0)
    misc = misc_ref[...]
    misc_t = misc.T
    wscale = IDX_HEADS ** -0.5 * IDX_DIM ** -0.5
    for j in range(tm // Q_BLOCK):
        cols = slice(j * Q_BLOCK, (j + 1) * Q_BLOCK)
        for h in range(IDX_HEADS):
            qit_ref[j, 0:IDX_DIM, h * Q_BLOCK:(h + 1) * Q_BLOCK] = qit[h * IDX_DIM:(h + 1) * IDX_DIM, cols]
        qit_ref[j, IDX_DIM:, :] = jnp.zeros((LANE - IDX_DIM, IDX_HEADS * Q_BLOCK), F32)
        wrow = jnp.concatenate([misc_t[IDX_DIM + h:IDX_DIM + h + 1, cols] for h in range(IDX_HEADS)], axis=1)
        wc_ref[j] = jnp.broadcast_to(wrow * wscale, (SUBLANE, IDX_HEADS * Q_BLOCK))

    lane = lax.broadcasted_iota(jnp.int32, misc.shape, 1)
    km = jnp.where(lane < IDX_DIM, misc, 0.0)
    ms = jnp.sum(km * km, axis=-1, keepdims=True) * (1.0 / IDX_DIM)
    kin = km * lax.rsqrt(ms + RMS_EPS) * kig_ref[...]
    ki_ref[...] = _rope(kin, ct[:, :LANE], st[:, :LANE], 1)


def _rope_lane_tables(length, width):
    half = ROPE_ROT // 2
    inv = ROPE_THETA ** (-jnp.arange(half, dtype=F32) * 2.0 / ROPE_ROT)
    ang = jnp.arange(length, dtype=F32)[:, None] * inv[None, :]
    cos, sin = jnp.cos(ang), jnp.sin(ang)
    ones = jnp.ones((length, ATT_HEAD_DIM - ROPE_ROT), F32)
    ct = jnp.concatenate([cos, cos, ones], axis=1)
    st = jnp.concatenate([-sin, sin, 0.0 * ones], axis=1)
    reps = width // ATT_HEAD_DIM
    return jnp.tile(ct, (1, reps)), jnp.tile(st, (1, reps))


def _dsa_prep(cq, ckv, misc, p, *, bsz):
    t, qr = cq.shape
    kvr = ckv.shape[1]
    tm = KEY_CHUNK
    length = t // bsz
    nt = length // tm
    d_att = ATT_HEADS * ATT_HEAD_DIM
    ct, st = _rope_lane_tables(length, d_att)
    wukv = p["w_ukv"].reshape(kvr, ATT_HEADS, 2, ATT_HEAD_DIM)
    wuk = wukv[:, :, 0, :].reshape(kvr, d_att).astype(BF16)
    wuvt = wukv[:, :, 1, :].reshape(kvr, d_att).T.astype(BF16)
    kig = jnp.concatenate([p["kidx_norm"], jnp.zeros((LANE - IDX_DIM,), F32)]).reshape(1, LANE)
    row = lambda c: pl.BlockSpec((tm, c), lambda b, i: (b * nt + i, 0))
    tab = pl.BlockSpec((tm, d_att), lambda b, i: (i, 0))
    tab_t = pl.BlockSpec((d_att, tm), lambda b, i: (0, i))
    nqb = tm // Q_BLOCK
    hq = IDX_HEADS * Q_BLOCK
    return pl.pallas_call(
        functools.partial(_prep_body, tm=tm),
        grid=(bsz, nt),
        in_specs=[row(qr), row(kvr), row(LANE), tab, tab, tab_t, tab_t, _const_spec((1, qr)),
                  _const_spec((1, kvr)), _const_spec((d_att, qr)), _const_spec((kvr, d_att)),
                  _const_spec((d_att, kvr)), _const_spec((d_att, 1)), _const_spec((1, d_att)),
                  _const_spec((IDX_HEADS * IDX_DIM, qr)), _const_spec((1, LANE))],
        out_specs=[row(d_att), row(LANE),
                   pl.BlockSpec((None, d_att, tm), lambda b, i: (b * nt + i, 0, 0)),
                   pl.BlockSpec((None, d_att, tm), lambda b, i: (b * nt + i, 0, 0)),
                   pl.BlockSpec((nqb, LANE, hq), lambda b, i: (b * nt + i, 0, 0)),
                   pl.BlockSpec((nqb, SUBLANE, hq), lambda b, i: (b * nt + i, 0, 0))],
        out_shape=[jax.ShapeDtypeStruct((t, d_att), BF16),
                   jax.ShapeDtypeStruct((t, LANE), F32),
                   jax.ShapeDtypeStruct((bsz * nt, d_att, tm), BF16),
                   jax.ShapeDtypeStruct((bsz * nt, d_att, tm), BF16),
                   jax.ShapeDtypeStruct((t // Q_BLOCK, LANE, hq), F32),
                   jax.ShapeDtypeStruct((t // Q_BLOCK, SUBLANE, hq), F32)],
        compiler_params=_cparams("parallel", "parallel"),
        name="dsa_prep",
    )(cq, ckv, misc, ct, st, ct.T, st.T, p["cq_norm"].reshape(1, qr), p["ckv_norm"].reshape(1, kvr),
      p["w_uq"].T.astype(BF16), wuk, wuvt, jnp.tile(p["q_norm"], ATT_HEADS).reshape(d_att, 1),
      jnp.tile(p["k_norm"], ATT_HEADS).reshape(1, d_att), p["w_qidx"].T.astype(BF16), kig)


INT_MIN = -2 ** 31
F32_TINY = 2.0 ** -126
F32_LOWEST = -3.4028234663852886e38
SEARCH_VALUE_STEPS = 12
SEARCH_CAP = 80


def _order_key_to_float(u):
    sk = u ^ jnp.int32(INT_MIN)
    fb = sk ^ ((sk >> 31) & jnp.int32(0x7FFFFFFF))
    return lax.bitcast_convert_type(fb, F32)


def _float_to_order_key(f):
    b = lax.bitcast_convert_type(f, jnp.int32)
    return (b ^ ((b >> 31) & jnp.int32(0x7FFFFFFF))) ^ jnp.int32(INT_MIN)


def _fold_rows(x, op):
    parts = [x[i * SUBLANE:(i + 1) * SUBLANE, :] for i in range(x.shape[0] // SUBLANE)]
    while len(parts) > 1:
        parts = [op(parts[i], parts[i + 1]) for i in range(0, len(parts), 2)]
    return parts[0]


def _dsa_body(qit_ref, wc_ref, qt_ref, ki_ref, k_ref, vt_ref, o_ref,
              sc_scr, bias_scr, lg_scr, qbd_scr, oacc_scr, *, n_sel):
    kc = KEY_CHUNK
    qb = pl.program_id(1)
    nk = qb // (kc // Q_BLOCK) + 1
    qpos = qb * Q_BLOCK + lax.broadcasted_iota(jnp.int32, (1, Q_BLOCK), 1)
    hq = IDX_HEADS * Q_BLOCK
    k_f = jnp.float32(n_sel)

    def chunk(c):
        return pl.ds(pl.multiple_of(c * kc, kc), kc)

    qit = qit_ref[...]
    wc = wc_ref[0:1, :]

    def score_chunk(c, amax):
        s4 = jnp.dot(ki_ref[chunk(c), :], qit, preferred_element_type=F32)
        s4 = jnp.maximum(s4, 0.0) * wc
        s = (s4[:, 0:Q_BLOCK] + s4[:, Q_BLOCK:2 * Q_BLOCK]) + (s4[:, 2 * Q_BLOCK:3 * Q_BLOCK] + s4[:, 3 * Q_BLOCK:])
        kpos = c * kc + lax.broadcasted_iota(jnp.int32, (kc, Q_BLOCK), 0)
        sc_scr[chunk(c), :] = jnp.where(kpos <= qpos, s, -jnp.inf)
        return jnp.maximum(amax, _fold_rows(jnp.abs(s), jnp.maximum))

    amax = lax.fori_loop(0, nk, score_chunk, jnp.zeros((SUBLANE, Q_BLOCK), F32))
    amax = jnp.max(amax, axis=0, keepdims=True)

    def count_ge(*thrs):
        def body(c, accs):
            blk = sc_scr[chunk(c), :]
            return tuple(a + _fold_rows(jnp.where(blk >= t, 1.0, 0.0), jnp.add) for a, t in zip(accs, thrs))
        accs = lax.fori_loop(0, nk, body, tuple(jnp.zeros((SUBLANE, Q_BLOCK), F32) for _ in thrs))
        return tuple(jnp.sum(a, axis=0, keepdims=True) for a in accs)

    n_valid = (qpos + 1).astype(F32)
    take_all = n_valid <= k_f
    c_pos, c_nn = count_ge(jnp.full((1, Q_BLOCK), F32_TINY, F32), jnp.zeros((1, Q_BLOCK), F32))
    zero_tie = (c_pos < k_f) & (c_nn >= k_f)
    pos = c_pos >= k_f
    hi_top = _order_key_to_float(_float_to_order_key(amax) + 1)
    lo = jnp.where(take_all, F32_LOWEST, jnp.where(zero_tie, 0.0, jnp.where(pos, F32_TINY, -amax)))
    hi = jnp.where(take_all, jnp.inf, jnp.where(zero_tie, F32_TINY, jnp.where(pos, hi_top, 0.0)))
    cnt_lo = jnp.where(take_all | (~zero_tie & ~pos), n_valid, jnp.where(zero_tie, c_nn, c_pos))
    cnt_hi = jnp.where(take_all | (~zero_tie & pos), 0.0, jnp.where(zero_tie, c_pos, c_nn))
    settled = take_all | zero_tie

    def unsettled(lo, hi, cnt_lo):
        gap = _float_to_order_key(hi) - _float_to_order_key(lo)
        open_ = jnp.logical_not(settled | (cnt_lo == k_f) | (gap <= 1))
        return jnp.sum(jnp.where(open_, 1.0, 0.0)) > 0.0

    def search_cond(st):
        it, lo, hi, cnt_lo, _ = st
        return (it < SEARCH_CAP) & unsettled(lo, hi, cnt_lo)

    def search_step(st):
        it, lo, hi, cnt_lo, cnt_hi = st
        lk = _float_to_order_key(lo)
        kmid = _order_key_to_float(lk + ((_float_to_order_key(hi) - lk) >> 1))
        vmid = lo + (hi - lo) * 0.5
        mid = jnp.where((it < SEARCH_VALUE_STEPS) & (vmid > lo) & (vmid < hi), vmid, kmid)
        cnt, = count_ge(mid)
        ge = cnt >= k_f
        return (it + 1, jnp.where(ge, mid, lo), jnp.where(ge, hi, mid),
                jnp.where(ge, cnt, cnt_lo), jnp.where(ge, cnt_hi, cnt))

    _, lo, hi, cnt_lo, cnt_hi = lax.while_loop(search_cond, search_step, (jnp.int32(0), lo, hi, cnt_lo, cnt_hi))
    need = k_f - cnt_hi

    sub = Q_BLOCK
    ri = lax.broadcasted_iota(jnp.int32, (sub, sub), 0)
    ci = lax.broadcasted_iota(jnp.int32, (sub, sub), 1)
    before = (ri > ci).astype(BF16)

    def select_chunk(c, seen):
        for j in range(kc // sub):
            rows = pl.ds(pl.multiple_of(c * kc + j * sub, sub), sub)
            blk = sc_scr[rows, :]
            in_hi = blk >= hi
            tie = (blk >= lo) & jnp.logical_not(in_hi)
            tie_f = jnp.where(tie, 1.0, 0.0)
            rank = jnp.dot(before, tie_f.astype(BF16), preferred_element_type=F32) + seen
            sel = in_hi | (tie & (rank < need))
            bias_scr[rows, :] = jnp.where(sel, 0.0, -jnp.inf)
            seen = seen + jnp.sum(tie_f, axis=0, keepdims=True)
        return seen

    lax.fori_loop(0, nk, select_chunk, jnp.zeros((1, Q_BLOCK), F32))

    qbd_scr[...] = jnp.zeros(qbd_scr.shape, BF16)
    for h in range(ATT_HEADS):
        rows = slice(h * ATT_HEAD_DIM, (h + 1) * ATT_HEAD_DIM)
        qbd_scr[rows, h * Q_BLOCK:(h + 1) * Q_BLOCK] = qt_ref[rows, :]
    qbd = qbd_scr[...]

    def logit_chunk(c, m):
        lg = jnp.dot(k_ref[chunk(c), :], qbd, preferred_element_type=F32)
        b = bias_scr[chunk(c), :]
        lg = lg + jnp.concatenate([b] * ATT_HEADS, axis=1)
        lg_scr[chunk(c), :] = lg
        return jnp.maximum(m, jnp.max(lg, axis=0, keepdims=True))

    m = lax.fori_loop(0, nk, logit_chunk, jnp.full((1, hq), -jnp.inf, F32))
    oacc_scr[...] = jnp.zeros(oacc_scr.shape, F32)

    def value_chunk(c, ssum):
        p = jnp.exp(lg_scr[chunk(c), :] - m)
        oacc_scr[...] += jnp.dot(vt_ref[c], p.astype(BF16), preferred_element_type=F32)
        return ssum + jnp.sum(p, axis=0, keepdims=True)

    ssum = lax.fori_loop(0, nk, value_chunk, jnp.zeros((1, hq), F32))
    outs = []
    for h in range(ATT_HEADS):
        cols = slice(h * Q_BLOCK, (h + 1) * Q_BLOCK)
        outs.append(oacc_scr[h * ATT_HEAD_DIM:(h + 1) * ATT_HEAD_DIM, cols] / ssum[:, cols])
    o_ref[...] = jnp.concatenate(outs, axis=0).T


def _dsa(k, ki, vt, qt, qit, wc, *, bsz):
    t, d_att = k.shape
    length = t // bsz
    nb = length // Q_BLOCK
    kc = KEY_CHUNK
    nkc = length // kc
    hq = IDX_HEADS * Q_BLOCK
    per_q = kc // Q_BLOCK
    vt = vt.reshape(bsz, nkc, d_att, kc)
    return pl.pallas_call(
        functools.partial(_dsa_body, n_sel=min(TOP_K, length // 4)),
        grid=(bsz, nb),
        in_specs=[pl.BlockSpec((None, LANE, hq), lambda b, i: (b * nb + i, 0, 0)),
                  pl.BlockSpec((None, SUBLANE, hq), lambda b, i: (b * nb + i, 0, 0)),
                  pl.BlockSpec((None, d_att, Q_BLOCK), lambda b, i: (b * nkc + i // per_q, 0, i % per_q)),
                  pl.BlockSpec((length, LANE), lambda b, i: (b, 0)),
                  pl.BlockSpec((length, d_att), lambda b, i: (b, 0)),
                  pl.BlockSpec((None, nkc, d_att, kc), lambda b, i: (b, 0, 0, 0))],
        out_specs=pl.BlockSpec((Q_BLOCK, d_att), lambda b, i: (b * nb + i, 0)),
        out_shape=jax.ShapeDtypeStruct((t, d_att), F32),
        scratch_shapes=[pltpu.VMEM((length, Q_BLOCK), F32), pltpu.VMEM((length, Q_BLOCK), F32),
                        pltpu.VMEM((length, hq), F32), pltpu.VMEM((d_att, hq), BF16),
                        pltpu.VMEM((d_att, hq), F32)],
        compiler_params=_cparams("parallel", "arbitrary"),
        name="dsa",
    )(qit, wc, qt, ki, k, vt)


def _pad_cols(w, width):
    return jnp.concatenate([w, jnp.zeros((w.shape[0], width - w.shape[1]), w.dtype)], axis=1)


def kernel(x, norm_ffn1, ffn1_w13, ffn1_w2, norm_mix, w_in, ssd_conv_w, ssd_conv_b, ssd_dt_bias, ssd_a_log, ssd_d, ssd_norm, cq_norm, ckv_norm, w_uq, w_ukv, q_norm, k_norm, w_qidx, kidx_norm, lru_conv_w, lru_conv_b, lru_wa, lru_ba, lru_wi, lru_bi, lru_lambda, w_out, norm_ffn2, ffn2_w13, ffn2_w2):
    bsz, length, d = x.shape
    depth = w_in.shape[0]
    d_ssd = ssd_norm.shape[1]
    conv_ch = ssd_conv_b.shape[1]
    heads = ssd_d.shape[1]
    q_rank, kv_rank = cq_norm.shape[1], ckv_norm.shape[1]
    d_lru = lru_lambda.shape[1]
    sizes = [d_ssd, conv_ch, heads, q_rank, kv_rank, IDX_DIM, IDX_HEADS, d_lru, d_lru]
    offs = [0]
    for s in sizes:
        offs.append(offs[-1] + s)
    col = lambda wl, k: wl[:, offs[k]:offs[k + 1]]
    widths = (d_ssd, conv_ch, LANE, q_rank, kv_rank, LANE, d_lru, d_lru)

    xt = x.reshape(bsz * length, d)
    for l in range(depth):
        xt = _ffn(xt, norm_ffn1[l], ffn1_w13[l], ffn1_w2[l])
        wl = w_in[l]
        w_pad = jnp.concatenate(
            [col(wl, 0), col(wl, 1), _pad_cols(col(wl, 2), LANE), col(wl, 3), col(wl, 4),
             _pad_cols(jnp.concatenate([col(wl, 5), col(wl, 6)], axis=1), LANE), col(wl, 7), col(wl, 8)],
            axis=1).astype(BF16)
        z, xbc, dtp, cq, ckv, misc, xl, gl = _inproj(xt, norm_mix[l], w_pad, widths)
        y_ssd = _ssd(z, xbc, dtp, ssd_conv_w[l], ssd_conv_b[l], ssd_dt_bias[l], ssd_a_log[l], ssd_d[l],
                     ssd_norm[l], bsz=bsz)
        y_lru = _lru(xl, gl, lru_conv_w[l], lru_conv_b[l], lru_wa[l], lru_ba[l], lru_wi[l], lru_bi[l],
                     lru_lambda[l], bsz=bsz)
        prm = dict(cq_norm=cq_norm[l], ckv_norm=ckv_norm[l], w_uq=w_uq[l], w_ukv=w_ukv[l], q_norm=q_norm[l],
                   k_norm=k_norm[l], w_qidx=w_qidx[l], kidx_norm=kidx_norm[l])
        k, ki, vt, qt, qit, wc = _dsa_prep(cq, ckv, misc, prm, bsz=bsz)
        y_att = _dsa(k, ki, vt, qt, qit, wc, bsz=bsz)
        xt = _outproj(xt, y_ssd, y_att, y_lru, w_out[l])
        xt = _ffn(xt, norm_ffn2[l], ffn2_w13[l], ffn2_w2[l])
    return xt.reshape(bsz, length, d)
```

```python
import functools

import jax
import jax.numpy as jnp
from jax import lax
from jax.experimental import pallas as pl
from jax.experimental.pallas import tpu as pltpu

F32 = jnp.float32
BF16 = jnp.bfloat16
HIGHEST = lax.Precision.HIGHEST

RMS_EPS = 1e-6
SSD_HEAD_DIM = 64
SSD_GROUPS = 2
SSD_STATE = 128
SSD_CHUNK = 128
ATT_HEADS = 4
ATT_HEAD_DIM = 64
IDX_HEADS = 4
IDX_DIM = 64
TOP_K = 256
Q_BLOCK = 128
ROPE_THETA = 500000.0
ROPE_ROT = 16
LRU_C = 8.0
CONV_W = 4
LANE = 128
SUBLANE = 8
KEY_CHUNK = 512
V_ONES_ROWS = 16
VMEM_LIMIT = 56 * 1024 * 1024


def _cparams(*sem):
    return pltpu.CompilerParams(dimension_semantics=sem, vmem_limit_bytes=VMEM_LIMIT)


def _const_spec(shape):
    nd = len(shape)
    return pl.BlockSpec(shape, lambda *_: (0,) * nd, pipeline_mode=pl.Buffered(1))


def _rms(x, g):
    ms = jnp.mean(x * x, axis=-1, keepdims=True)
    return x * lax.rsqrt(ms + RMS_EPS) * g


def _softplus(x):
    return jnp.maximum(x, 0.0) + jnp.log1p(jnp.exp(-jnp.abs(x)))


def _silu(x):
    return x * jax.nn.sigmoid(x)


def _ffn_body(x_ref, g_ref, w13_ref, w2_ref, o_ref, *, d_ff, tf):
    x = x_ref[...]
    h = _rms(x, g_ref[...]).astype(BF16)
    acc = jnp.zeros(x.shape, F32)
    for j in range(d_ff // tf):
        g = jnp.dot(h, w13_ref[:, j * tf:(j + 1) * tf], preferred_element_type=F32)
        u = jnp.dot(h, w13_ref[:, d_ff + j * tf:d_ff + (j + 1) * tf], preferred_element_type=F32)
        a = (_silu(g) * u).astype(BF16)
        acc = acc + jnp.dot(a, w2_ref[j * tf:(j + 1) * tf, :], preferred_element_type=F32)
    o_ref[...] = x + 0.5 * acc


def _ffn(x, g, w13, w2, *, tm=512, tf=256):
    t, d = x.shape
    d_ff = w2.shape[0]
    return pl.pallas_call(
        functools.partial(_ffn_body, d_ff=d_ff, tf=tf),
        grid=(t // tm,),
        in_specs=[pl.BlockSpec((tm, d), lambda i: (i, 0)), _const_spec((1, d)),
                  _const_spec((d, 2 * d_ff)), _const_spec((d_ff, d))],
        out_specs=pl.BlockSpec((tm, d), lambda i: (i, 0)),
        out_shape=jax.ShapeDtypeStruct((t, d), F32),
        compiler_params=_cparams("parallel"),
        name="ffn",
    )(x, g.reshape(1, d), w13.astype(BF16), w2.astype(BF16))


def _inproj_body(x_ref, g_ref, w_ref, *o_refs, widths):
    h = _rms(x_ref[...], g_ref[...]).astype(BF16)
    off = 0
    for o_ref, wd in zip(o_refs, widths):
        o_ref[...] = jnp.dot(h, w_ref[:, off:off + wd], preferred_element_type=F32)
        off += wd


def _inproj(x, g, w_pad, widths, *, tm=512):
    t, d = x.shape
    return pl.pallas_call(
        functools.partial(_inproj_body, widths=widths),
        grid=(t // tm,),
        in_specs=[pl.BlockSpec((tm, d), lambda i: (i, 0)), _const_spec((1, d)), _const_spec(w_pad.shape)],
        out_specs=[pl.BlockSpec((tm, wd), lambda i: (i, 0)) for wd in widths],
        out_shape=[jax.ShapeDtypeStruct((t, wd), F32) for wd in widths],
        compiler_params=_cparams("parallel"),
        name="inproj",
    )(x, g.reshape(1, d), w_pad)


def _outproj_body(x_ref, ys_ref, ya_ref, yl_ref, w_ref, o_ref):
    d_s, d_a = ys_ref.shape[1], ya_ref.shape[1]
    acc = jnp.dot(ys_ref[...].astype(BF16), w_ref[0:d_s, :], preferred_element_type=F32)
    acc = acc + jnp.dot(ya_ref[...].astype(BF16), w_ref[d_s:d_s + d_a, :], preferred_element_type=F32)
    acc = acc + jnp.dot(yl_ref[...].astype(BF16), w_ref[d_s + d_a:, :], preferred_element_type=F32)
    o_ref[...] = x_ref[...] + acc


def _outproj(x, ys, ya, yl, w, *, tm=512):
    t, d = x.shape
    row = lambda c: pl.BlockSpec((tm, c), lambda i: (i, 0))
    return pl.pallas_call(
        _outproj_body,
        grid=(t // tm,),
        in_specs=[row(d), row(ys.shape[1]), row(ya.shape[1]), row(yl.shape[1]), _const_spec(w.shape)],
        out_specs=row(d),
        out_shape=jax.ShapeDtypeStruct((t, d), F32),
        compiler_params=_cparams("parallel"),
        name="outproj",
    )(x, ys, ya, yl, w.astype(BF16))


def _causal_conv(x_ref, cw_ref, cb_ref, xe_scr, step, rows):
    @pl.when(step == 0)
    def _():
        xe_scr[0:SUBLANE, :] = jnp.zeros((SUBLANE, xe_scr.shape[1]), F32)

    @pl.when(step > 0)
    def _():
        xe_scr[0:SUBLANE, :] = xe_scr[rows:rows + SUBLANE, :]

    xe_scr[SUBLANE:SUBLANE + rows, :] = x_ref[...]
    y = cb_ref[...] + cw_ref[CONV_W - 1:CONV_W, :] * x_ref[...]
    for w in range(CONV_W - 1):
        off = SUBLANE - (CONV_W - 1) + w
        y = y + cw_ref[w:w + 1, :] * xe_scr[off:off + rows, :]
    return y


def _ssd_body(z_ref, xbc_ref, dtp_ref, cw_ref, cb_ref, dtb_ref, alog_ref, dsk_ref, ng_ref,
              y_ref, xe_scr, st_scr, *, q, d_ssd):
    step = pl.program_id(1)
    n = SSD_STATE
    gw = d_ssd // SSD_GROUPS
    rpg = gw // SSD_HEAD_DIM

    @pl.when(step == 0)
    def _():
        st_scr[...] = jnp.zeros(st_scr.shape, F32)

    xbc = _silu(_causal_conv(xbc_ref, cw_ref, cb_ref, xe_scr, step, q))
    xs = xbc[:, :d_ssd]
    bm = xbc[:, d_ssd:d_ssd + SSD_GROUPS * n]
    cm = xbc[:, d_ssd + SSD_GROUPS * n:]

    dt = _softplus(dtp_ref[...] + dtb_ref[...])
    adt = dt * (-jnp.exp(alog_ref[...]))
    ri = lax.broadcasted_iota(jnp.int32, (q, q), 0)
    ci = lax.broadcasted_iota(jnp.int32, (q, q), 1)
    causal = ri >= ci
    acum = jnp.dot(causal.astype(F32), adt, precision=HIGHEST, preferred_element_type=F32)
    er = lax.broadcasted_iota(jnp.int32, (LANE, d_ssd), 0)
    ec = lax.broadcasted_iota(jnp.int32, (LANE, d_ssd), 1)
    expand = (ec // SSD_HEAD_DIM == er).astype(F32)
    a_x = jnp.dot(acum, expand, precision=HIGHEST, preferred_element_type=F32)
    dt_x = jnp.dot(dt, expand, precision=HIGHEST, preferred_element_type=F32)
    acum_t = acum.T
    a_last = a_x[q - 1:q, :]
    exp_a = jnp.exp(a_x)
    xdt = xs * dt_x
    xsw = xdt * jnp.exp(a_last - a_x)
    cdec = jnp.exp(a_last)

    ys = []
    for g in range(SSD_GROUPS):
        bm_g = bm[:, g * n:(g + 1) * n]
        cm_g = cm[:, g * n:(g + 1) * n].astype(BF16)
        cb = lax.dot_general(cm_g, bm_g.astype(BF16), (((1,), (1,)), ((), ())), preferred_element_type=F32)
        yd = []
        for r in range(rpg):
            h = g * rpg + r
            seg = acum[:, h:h + 1] - acum_t[h:h + 1, :]
            decay = jnp.exp(jnp.where(causal, seg, -jnp.inf))
            m = (cb * decay).astype(BF16)
            xh = xdt[:, h * SSD_HEAD_DIM:(h + 1) * SSD_HEAD_DIM].astype(BF16)
            yd.append(jnp.dot(m, xh, preferred_element_type=F32))
        yd = jnp.concatenate(yd, axis=1)
        gs = slice(g * gw, (g + 1) * gw)
        prev = st_scr[g]
        yoff = jnp.dot(cm_g, prev.astype(BF16), preferred_element_type=F32) * exp_a[:, gs]
        st = jnp.dot(bm_g.T.astype(BF16), xsw[:, gs].astype(BF16), preferred_element_type=F32)
        st_scr[g] = prev * cdec[:, gs] + st
        ys.append(yd + yoff + dsk_ref[:, gs] * xs[:, gs])

    z = z_ref[...]
    for g in range(SSD_GROUPS):
        gs = slice(g * gw, (g + 1) * gw)
        y_ref[:, gs] = _rms(ys[g] * _silu(z[:, gs]), ng_ref[:, gs])


def _ssd(z, xbc, dtp, conv_w, conv_b, dt_bias, a_log, d_skip, norm_g, *, bsz):
    t, d_ssd = z.shape
    cch = xbc.shape[1]
    q = SSD_CHUNK
    nc = t // bsz // q
    heads = d_ssd // SSD_HEAD_DIM
    pad = lambda v, fill: jnp.concatenate([v, jnp.full((LANE - heads,), fill, F32)]).reshape(1, LANE)
    row = lambda c: pl.BlockSpec((q, c), lambda b, i: (b * nc + i, 0))
    return pl.pallas_call(
        functools.partial(_ssd_body, q=q, d_ssd=d_ssd),
        grid=(bsz, nc),
        in_specs=[row(d_ssd), row(cch), row(LANE), _const_spec((CONV_W, cch)), _const_spec((1, cch)),
                  _const_spec((1, LANE)), _const_spec((1, LANE)), _const_spec((1, d_ssd)),
                  _const_spec((1, d_ssd))],
        out_specs=row(d_ssd),
        out_shape=jax.ShapeDtypeStruct((t, d_ssd), F32),
        scratch_shapes=[pltpu.VMEM((q + SUBLANE, cch), F32),
                        pltpu.VMEM((SSD_GROUPS, SSD_STATE, d_ssd // SSD_GROUPS), F32)],
        compiler_params=_cparams("parallel", "arbitrary"),
        name="ssd",
    )(z, xbc, dtp, conv_w, conv_b.reshape(1, cch), pad(dt_bias, 0.0), pad(a_log, 0.0),
      jnp.repeat(d_skip, SSD_HEAD_DIM).reshape(1, d_ssd), norm_g.reshape(1, d_ssd))


def _lru_body(xl_ref, gl_ref, cw_ref, cb_ref, wa_ref, ba_ref, wi_ref, bi_ref, lam_ref,
              y_ref, xe_scr, h_scr, *, tl):
    step = pl.program_id(1)

    @pl.when(step == 0)
    def _():
        h_scr[...] = jnp.zeros(h_scr.shape, F32)

    xr = _causal_conv(xl_ref, cw_ref, cb_ref, xe_scr, step, tl)
    xrb = xr.astype(BF16)
    r = jax.nn.sigmoid(jnp.dot(xrb, wa_ref[...], preferred_element_type=F32) + ba_ref[...])
    i = jax.nn.sigmoid(jnp.dot(xrb, wi_ref[...], preferred_element_type=F32) + bi_ref[...])
    log_a = -LRU_C * r * _softplus(-lam_ref[...])
    a = jnp.exp(log_a)
    b = jnp.sqrt(-jnp.tanh(log_a) * (a * a + 1.0)) * (i * xr)
    rows = lax.broadcasted_iota(jnp.int32, a.shape, 0)
    s = 1
    while s < tl:
        keep = rows >= s
        a_sh = jnp.where(keep, pltpu.roll(a, s, 0), 1.0)
        b_sh = jnp.where(keep, pltpu.roll(b, s, 0), 0.0)
        b = a * b_sh + b
        a = a * a_sh
        s *= 2
    hs = a * h_scr[0:1, :] + b
    h_scr[...] = jnp.broadcast_to(hs[tl - 1:tl, :], h_scr.shape)
    y_ref[...] = hs * jax.nn.gelu(gl_ref[...])


def _block_diag(w):
    nb, bw, _ = w.shape
    out = jnp.zeros((nb * bw, nb * bw), w.dtype)
    for k in range(nb):
        out = out.at[k * bw:(k + 1) * bw, k * bw:(k + 1) * bw].set(w[k])
    return out


def _lru(xl, gl, conv_w, conv_b, wa, ba, wi, bi, lam, *, bsz, tl=256):
    t, dl = xl.shape
    nt = t // bsz // tl
    row = pl.BlockSpec((tl, dl), lambda b, i: (b * nt + i, 0))
    vec = _const_spec((1, dl))
    return pl.pallas_call(
        functools.partial(_lru_body, tl=tl),
        grid=(bsz, nt),
        in_specs=[row, row, _const_spec((CONV_W, dl)), vec, _const_spec((dl, dl)), vec,
                  _const_spec((dl, dl)), vec, vec],
        out_specs=row,
        out_shape=jax.ShapeDtypeStruct((t, dl), F32),
        scratch_shapes=[pltpu.VMEM((tl + SUBLANE, dl), F32), pltpu.VMEM((SUBLANE, dl), F32)],
        compiler_params=_cparams("parallel", "arbitrary"),
        name="rglru",
    )(xl, gl, conv_w, conv_b.reshape(1, dl), _block_diag(wa).astype(BF16), ba.reshape(1, dl),
      _block_diag(wi).astype(BF16), bi.reshape(1, dl), lam.reshape(1, dl))


def _rope(x, ct, st, axis):
    width = x.shape[axis]
    fm = lax.broadcasted_iota(jnp.int32, x.shape, axis) % ATT_HEAD_DIM
    half = ROPE_ROT // 2
    partner = jnp.where(fm < half, pltpu.roll(x, width - half, axis), pltpu.roll(x, half, axis))
    return x * ct + partner * st


def _dot_nt(a, b):
    return lax.dot_general(a, b, (((1,), (1,)), ((), ())), preferred_element_type=F32)


def _prep_body(cq_ref, ckv_ref, misc_ref, ct_ref, st_ref, ctt_ref, stt_ref, cqg_ref, ckvg_ref, wuqt_ref,
               wuk_ref, wuvt_ref, qg_ref, kg_ref, wqit_ref, kig_ref,
               k_ref, ki_ref, vt_ref, qt_ref, qit_ref, wc_ref, *, tm):
    ct = ct_ref[...]
    st = st_ref[...]
    d_att = wuk_ref.shape[1]
    gr = lax.broadcasted_iota(jnp.int32, (d_att, d_att), 0) // ATT_HEAD_DIM
    gc = lax.broadcasted_iota(jnp.int32, (d_att, d_att), 1) // ATT_HEAD_DIM
    head_mean = jnp.where(gr == gc, 1.0 / ATT_HEAD_DIM, 0.0).astype(F32)

    cqn = _rms(cq_ref[...], cqg_ref[...]).astype(BF16)
    ckvn = _rms(ckv_ref[...], ckvg_ref[...]).astype(BF16)

    qt = _dot_nt(wuqt_ref[...], cqn)
    ms = jnp.dot(head_mean, qt * qt, precision=HIGHEST, preferred_element_type=F32)
    qt = qt * lax.rsqrt(ms + RMS_EPS) * qg_ref[...]
    qt_ref[...] = (_rope(qt, ctt_ref[...], stt_ref[...], 0) * ATT_HEAD_DIM ** -0.5).astype(BF16)
    vt_ref[0:d_att, :] = _dot_nt(wuvt_ref[...], ckvn).astype(BF16)
    vt_ref[d_att:, :] = jnp.ones((vt_ref.shape[0] - d_att, tm), BF16)

    k = jnp.dot(ckvn, wuk_ref[...], preferred_element_type=F32)
    ms = jnp.dot(k * k, head_mean, precision=HIGHEST, preferred_element_type=F32)
    k_ref[...] = _rope(k * lax.rsqrt(ms + RMS_EPS) * kg_ref[...], ct, st, 1).astype(BF16)

    qit = _rope(_dot_nt(wqit_ref[...], cqn), ctt_ref[...], stt_ref[...], 0)
    misc = misc_ref[...]
    misc_t = misc.T
    wscale = IDX_HEADS ** -0.5 * IDX_DIM ** -0.5
    for j in range(tm // Q_BLOCK):
        cols = slice(j * Q_BLOCK, (j + 1) * Q_BLOCK)
        for h in range(IDX_HEADS):
            qit_ref[j, 0:IDX_DIM, h * Q_BLOCK:(h + 1) * Q_BLOCK] = (
                qit[h * IDX_DIM:(h + 1) * IDX_DIM, cols].astype(BF16))
        qit_ref[j, IDX_DIM:, :] = jnp.zeros((LANE - IDX_DIM, IDX_HEADS * Q_BLOCK), BF16)
        wrow = jnp.concatenate([misc_t[IDX_DIM + h:IDX_DIM + h + 1, cols] for h in range(IDX_HEADS)], axis=1)
        wc_ref[j] = jnp.broadcast_to(wrow * wscale, (SUBLANE, IDX_HEADS * Q_BLOCK))

    lane = lax.broadcasted_iota(jnp.int32, misc.shape, 1)
    km = jnp.where(lane < IDX_DIM, misc, 0.0)
    ms = jnp.sum(km * km, axis=-1, keepdims=True) * (1.0 / IDX_DIM)
    kin = km * lax.rsqrt(ms + RMS_EPS) * kig_ref[...]
    ki_ref[...] = _rope(kin, ct[:, :LANE], st[:, :LANE], 1).astype(BF16)


def _rope_lane_tables(length, width):
    half = ROPE_ROT // 2
    inv = ROPE_THETA ** (-jnp.arange(half, dtype=F32) * 2.0 / ROPE_ROT)
    ang = jnp.arange(length, dtype=F32)[:, None] * inv[None, :]
    cos, sin = jnp.cos(ang), jnp.sin(ang)
    ones = jnp.ones((length, ATT_HEAD_DIM - ROPE_ROT), F32)
    ct = jnp.concatenate([cos, cos, ones], axis=1)
    st = jnp.concatenate([-sin, sin, 0.0 * ones], axis=1)
    reps = width // ATT_HEAD_DIM
    return jnp.tile(ct, (1, reps)), jnp.tile(st, (1, reps))


def _dsa_prep(cq, ckv, misc, p, *, bsz):
    t, qr = cq.shape
    kvr = ckv.shape[1]
    tm = KEY_CHUNK
    length = t // bsz
    nt = length // tm
    d_att = ATT_HEADS * ATT_HEAD_DIM
    ct, st = _rope_lane_tables(length, d_att)
    wukv = p["w_ukv"].reshape(kvr, ATT_HEADS, 2, ATT_HEAD_DIM)
    wuk = wukv[:, :, 0, :].reshape(kvr, d_att).astype(BF16)
    wuvt = wukv[:, :, 1, :].reshape(kvr, d_att).T.astype(BF16)
    kig = jnp.concatenate([p["kidx_norm"], jnp.zeros((LANE - IDX_DIM,), F32)]).reshape(1, LANE)
    row = lambda c: pl.BlockSpec((tm, c), lambda b, i: (b * nt + i, 0))
    tab = pl.BlockSpec((tm, d_att), lambda b, i: (i, 0))
    tab_t = pl.BlockSpec((d_att, tm), lambda b, i: (0, i))
    nqb = tm // Q_BLOCK
    hq = IDX_HEADS * Q_BLOCK
    return pl.pallas_call(
        functools.partial(_prep_body, tm=tm),
        grid=(bsz, nt),
        in_specs=[row(qr), row(kvr), row(LANE), tab, tab, tab_t, tab_t, _const_spec((1, qr)),
                  _const_spec((1, kvr)), _const_spec((d_att, qr)), _const_spec((kvr, d_att)),
                  _const_spec((d_att, kvr)), _const_spec((d_att, 1)), _const_spec((1, d_att)),
                  _const_spec((IDX_HEADS * IDX_DIM, qr)), _const_spec((1, LANE))],
        out_specs=[row(d_att), row(LANE),
                   pl.BlockSpec((None, d_att + V_ONES_ROWS, tm), lambda b, i: (b * nt + i, 0, 0)),
                   pl.BlockSpec((None, d_att, tm), lambda b, i: (b * nt + i, 0, 0)),
                   pl.BlockSpec((nqb, LANE, hq), lambda b, i: (b * nt + i, 0, 0)),
                   pl.BlockSpec((nqb, SUBLANE, hq), lambda b, i: (b * nt + i, 0, 0))],
        out_shape=[jax.ShapeDtypeStruct((t, d_att), BF16),
                   jax.ShapeDtypeStruct((t, LANE), BF16),
                   jax.ShapeDtypeStruct((bsz * nt, d_att + V_ONES_ROWS, tm), BF16),
                   jax.ShapeDtypeStruct((bsz * nt, d_att, tm), BF16),
                   jax.ShapeDtypeStruct((t // Q_BLOCK, LANE, hq), BF16),
                   jax.ShapeDtypeStruct((t // Q_BLOCK, SUBLANE, hq), F32)],
        compiler_params=_cparams("parallel", "parallel"),
        name="dsa_prep",
    )(cq, ckv, misc, ct, st, ct.T, st.T, p["cq_norm"].reshape(1, qr), p["ckv_norm"].reshape(1, kvr),
      p["w_uq"].T.astype(BF16), wuk, wuvt, jnp.tile(p["q_norm"], ATT_HEADS).reshape(d_att, 1),
      jnp.tile(p["k_norm"], ATT_HEADS).reshape(1, d_att), p["w_qidx"].T.astype(BF16), kig)


INT_MIN = -2 ** 31
F32_TINY = 2.0 ** -126
F32_LOWEST = -3.4028234663852886e38
SEARCH_VALUE_STEPS = 16
SEARCH_CAP = 80


def _order_key_to_float(u):
    sk = u ^ jnp.int32(INT_MIN)
    fb = sk ^ ((sk >> 31) & jnp.int32(0x7FFFFFFF))
    return lax.bitcast_convert_type(fb, F32)


def _float_to_order_key(f):
    b = lax.bitcast_convert_type(f, jnp.int32)
    return (b ^ ((b >> 31) & jnp.int32(0x7FFFFFFF))) ^ jnp.int32(INT_MIN)


def _fold_rows(x, op):
    parts = [x[i * SUBLANE:(i + 1) * SUBLANE, :] for i in range(x.shape[0] // SUBLANE)]
    while len(parts) > 1:
        parts = [op(parts[i], parts[i + 1]) for i in range(0, len(parts), 2)]
    return parts[0]


def _dsa_body(qit_ref, wc_ref, qt_ref, ki_ref, k_ref, vt_ref, o_ref,
              sc_scr, lg_scr, qbd_scr, oacc_scr, *, n_sel):
    kc = KEY_CHUNK
    qb = pl.program_id(1)
    nk = qb // (kc // Q_BLOCK) + 1
    qpos = qb * Q_BLOCK + lax.broadcasted_iota(jnp.int32, (1, Q_BLOCK), 1)
    hq = IDX_HEADS * Q_BLOCK
    k_f = jnp.float32(n_sel)

    def chunk(c):
        return pl.ds(pl.multiple_of(c * kc, kc), kc)

    qit = qit_ref[...]
    wc = wc_ref[0:1, :]

    def score_chunk(c, amax):
        s4 = jnp.dot(ki_ref[chunk(c), :], qit, preferred_element_type=F32)
        s4 = jnp.maximum(s4, 0.0) * wc
        s = (s4[:, 0:Q_BLOCK] + s4[:, Q_BLOCK:2 * Q_BLOCK]) + (s4[:, 2 * Q_BLOCK:3 * Q_BLOCK] + s4[:, 3 * Q_BLOCK:])
        kpos = c * kc + lax.broadcasted_iota(jnp.int32, (kc, Q_BLOCK), 0)
        sc_scr[chunk(c), :] = jnp.where(kpos <= qpos, s, -jnp.inf)
        return jnp.maximum(amax, _fold_rows(jnp.abs(s), jnp.maximum))

    amax = lax.fori_loop(0, nk, score_chunk, jnp.zeros((SUBLANE, Q_BLOCK), F32))
    amax = jnp.max(amax, axis=0, keepdims=True)

    def count_ge(*thrs):
        def body(c, accs):
            blk = sc_scr[chunk(c), :]
            return tuple(a + _fold_rows(jnp.where(blk >= t, 1.0, 0.0), jnp.add) for a, t in zip(accs, thrs))
        accs = lax.fori_loop(0, nk, body, tuple(jnp.zeros((SUBLANE, Q_BLOCK), F32) for _ in thrs))
        return tuple(jnp.sum(a, axis=0, keepdims=True) for a in accs)

    n_valid = (qpos + 1).astype(F32)
    take_all = n_valid <= k_f
    c_pos, c_nn = count_ge(jnp.full((1, Q_BLOCK), F32_TINY, F32), jnp.zeros((1, Q_BLOCK), F32))
    zero_tie = (c_pos < k_f) & (c_nn >= k_f)
    pos = c_pos >= k_f
    hi_top = _order_key_to_float(_float_to_order_key(amax) + 1)
    lo = jnp.where(take_all, F32_LOWEST, jnp.where(zero_tie, 0.0, jnp.where(pos, F32_TINY, -amax)))
    hi = jnp.where(take_all, jnp.inf, jnp.where(zero_tie, F32_TINY, jnp.where(pos, hi_top, 0.0)))
    cnt_lo = jnp.where(take_all | (~zero_tie & ~pos), n_valid, jnp.where(zero_tie, c_nn, c_pos))
    cnt_hi = jnp.where(take_all | (~zero_tie & pos), 0.0, jnp.where(zero_tie, c_pos, c_nn))
    settled = take_all | zero_tie

    def search_step(st, on_values):
        lk, hk, lo, hi, cnt_lo, cnt_hi = st
        midk = lk + ((hk - lk) >> 1)
        mid = _order_key_to_float(midk)
        if on_values:
            vmid = lo + (hi - lo) * 0.5
            inside = (vmid > lo) & (vmid < hi)
            mid = jnp.where(inside, vmid, mid)
            midk = jnp.where(inside, _float_to_order_key(vmid), midk)
        cnt, = count_ge(mid)
        ge = cnt >= k_f
        return (jnp.where(ge, midk, lk), jnp.where(ge, hk, midk), jnp.where(ge, mid, lo), jnp.where(ge, hi, mid),
                jnp.where(ge, cnt, cnt_lo), jnp.where(ge, cnt_hi, cnt))

    def search_cond(carry):
        it, (lk, hk, _, _, cnt_lo, _) = carry
        open_ = jnp.logical_not(settled | (cnt_lo == k_f) | (hk - lk <= 1))
        return (it < SEARCH_CAP) & (jnp.sum(jnp.where(open_, 1.0, 0.0)) > 0.0)

    def search_pair(carry):
        it, st = carry
        return it + 1, search_step(search_step(st, False), False)

    st = (_float_to_order_key(lo), _float_to_order_key(hi), lo, hi, cnt_lo, cnt_hi)
    st = lax.fori_loop(0, SEARCH_VALUE_STEPS, lambda _, s: search_step(s, True), st)
    _, (_, _, lo, hi, cnt_lo, cnt_hi) = lax.while_loop(search_cond, search_pair, (jnp.int32(0), st))
    need = k_f - cnt_hi

    sub = Q_BLOCK
    ri = lax.broadcasted_iota(jnp.int32, (sub, sub), 0)
    ci = lax.broadcasted_iota(jnp.int32, (sub, sub), 1)
    before = (ri > ci).astype(BF16)

    qbd_scr[...] = jnp.zeros(qbd_scr.shape, BF16)
    for h in range(ATT_HEADS):
        rows = slice(h * ATT_HEAD_DIM, (h + 1) * ATT_HEAD_DIM)
        qbd_scr[rows, h * Q_BLOCK:(h + 1) * Q_BLOCK] = qt_ref[rows, :]
    qbd = qbd_scr[...]

    def logit_chunk(c, carry):
        m, seen = carry
        for j in range(kc // sub):
            rows = pl.ds(pl.multiple_of(c * kc + j * sub, sub), sub)
            blk = sc_scr[rows, :]
            in_hi = blk >= hi
            tie = (blk >= lo) & jnp.logical_not(in_hi)
            tie_f = jnp.where(tie, 1.0, 0.0)
            rank = jnp.dot(before, tie_f.astype(BF16), preferred_element_type=F32) + seen
            sel = in_hi | (tie & (rank < need))
            seen = seen + jnp.sum(tie_f, axis=0, keepdims=True)
            b = jnp.where(sel, 0.0, -jnp.inf)
            lg = jnp.dot(k_ref[rows, :], qbd, preferred_element_type=F32) + jnp.concatenate([b] * ATT_HEADS, axis=1)
            lg_scr[rows, :] = lg
            m = jnp.maximum(m, jnp.max(lg, axis=0, keepdims=True))
        return m, seen

    m, _ = lax.fori_loop(0, nk, logit_chunk, (jnp.full((1, hq), -jnp.inf, F32), jnp.zeros((1, Q_BLOCK), F32)))
    oacc_scr[...] = jnp.zeros(oacc_scr.shape, F32)

    def value_chunk(c, carry):
        acc = oacc_scr[...]
        depth = 2 * Q_BLOCK
        for j in range(kc // depth):
            rows = pl.ds(pl.multiple_of(c * kc + j * depth, depth), depth)
            p = jnp.exp(lg_scr[rows, :] - m).astype(BF16)
            acc = acc + jnp.dot(vt_ref[c, :, j * depth:(j + 1) * depth], p, preferred_element_type=F32)
        oacc_scr[...] = acc
        return carry

    lax.fori_loop(0, nk, value_chunk, 0)
    d_att = ATT_HEADS * ATT_HEAD_DIM
    outs = []
    for h in range(ATT_HEADS):
        cols = slice(h * Q_BLOCK, (h + 1) * Q_BLOCK)
        ssum = oacc_scr[d_att:d_att + 1, cols]
        outs.append(oacc_scr[h * ATT_HEAD_DIM:(h + 1) * ATT_HEAD_DIM, cols] / ssum)
    o_ref[...] = jnp.concatenate(outs, axis=0).T


def _dsa(k, ki, vt, qt, qit, wc, *, bsz):
    t, d_att = k.shape
    length = t // bsz
    nb = length // Q_BLOCK
    kc = KEY_CHUNK
    nkc = length // kc
    hq = IDX_HEADS * Q_BLOCK
    per_q = kc // Q_BLOCK
    v_rows = d_att + V_ONES_ROWS
    vt = vt.reshape(bsz, nkc, v_rows, kc)
    return pl.pallas_call(
        functools.partial(_dsa_body, n_sel=min(TOP_K, length // 4)),
        grid=(bsz, nb),
        in_specs=[pl.BlockSpec((None, LANE, hq), lambda b, i: (b * nb + i, 0, 0)),
                  pl.BlockSpec((None, SUBLANE, hq), lambda b, i: (b * nb + i, 0, 0)),
                  pl.BlockSpec((None, d_att, Q_BLOCK), lambda b, i: (b * nkc + i // per_q, 0, i % per_q)),
                  pl.BlockSpec((length, LANE), lambda b, i: (b, 0)),
                  pl.BlockSpec((length, d_att), lambda b, i: (b, 0)),
                  pl.BlockSpec((None, nkc, v_rows, kc), lambda b, i: (b, 0, 0, 0))],
        out_specs=pl.BlockSpec((Q_BLOCK, d_att), lambda b, i: (b * nb + i, 0)),
        out_shape=jax.ShapeDtypeStruct((t, d_att), F32),
        scratch_shapes=[pltpu.VMEM((length, Q_BLOCK), F32), pltpu.VMEM((length, hq), F32),
                        pltpu.VMEM((d_att, hq), BF16), pltpu.VMEM((v_rows, hq), F32)],
        compiler_params=_cparams("parallel", "arbitrary"),
        name="dsa",
    )(qit, wc, qt, ki, k, vt)


def _pad_cols(w, width):
    return jnp.concatenate([w, jnp.zeros((w.shape[0], width - w.shape[1]), w.dtype)], axis=1)


def kernel(x, norm_ffn1, ffn1_w13, ffn1_w2, norm_mix, w_in, ssd_conv_w, ssd_conv_b, ssd_dt_bias, ssd_a_log, ssd_d, ssd_norm, cq_norm, ckv_norm, w_uq, w_ukv, q_norm, k_norm, w_qidx, kidx_norm, lru_conv_w, lru_conv_b, lru_wa, lru_ba, lru_wi, lru_bi, lru_lambda, w_out, norm_ffn2, ffn2_w13, ffn2_w2):
    bsz, length, d = x.shape
    depth = w_in.shape[0]
    d_ssd = ssd_norm.shape[1]
    conv_ch = ssd_conv_b.shape[1]
    heads = ssd_d.shape[1]
    q_rank, kv_rank = cq_norm.shape[1], ckv_norm.shape[1]
    d_lru = lru_lambda.shape[1]
    sizes = [d_ssd, conv_ch, heads, q_rank, kv_rank, IDX_DIM, IDX_HEADS, d_lru, d_lru]
    offs = [0]
    for s in sizes:
        offs.append(offs[-1] + s)
    col = lambda wl, k: wl[:, offs[k]:offs[k + 1]]
    widths = (d_ssd, conv_ch, LANE, q_rank, kv_rank, LANE, d_lru, d_lru)

    xt = x.reshape(bsz * length, d)
    for l in range(depth):
        xt = _ffn(xt, norm_ffn1[l], ffn1_w13[l], ffn1_w2[l])
        wl = w_in[l]
        w_pad = jnp.concatenate(
            [col(wl, 0), col(wl, 1), _pad_cols(col(wl, 2), LANE), col(wl, 3), col(wl, 4),
             _pad_cols(jnp.concatenate([col(wl, 5), col(wl, 6)], axis=1), LANE), col(wl, 7), col(wl, 8)],
            axis=1).astype(BF16)
        z, xbc, dtp, cq, ckv, misc, xl, gl = _inproj(xt, norm_mix[l], w_pad, widths)
        y_ssd = _ssd(z, xbc, dtp, ssd_conv_w[l], ssd_conv_b[l], ssd_dt_bias[l], ssd_a_log[l], ssd_d[l],
                     ssd_norm[l], bsz=bsz)
        y_lru = _lru(xl, gl, lru_conv_w[l], lru_conv_b[l], lru_wa[l], lru_ba[l], lru_wi[l], lru_bi[l],
                     lru_lambda[l], bsz=bsz)
        prm = dict(cq_norm=cq_norm[l], ckv_norm=ckv_norm[l], w_uq=w_uq[l], w_ukv=w_ukv[l], q_norm=q_norm[l],
                   k_norm=k_norm[l], w_qidx=w_qidx[l], kidx_norm=kidx_norm[l])
        k, ki, vt, qt, qit, wc = _dsa_prep(cq, ckv, misc, prm, bsz=bsz)
        y_att = _dsa(k, ki, vt, qt, qit, wc, bsz=bsz)
        xt = _outproj(xt, y_ssd, y_att, y_lru, w_out[l])
        xt = _ffn(xt, norm_ffn2[l], ffn2_w13[l], ffn2_w2[l])
    return xt.reshape(bsz, length, d)
```

```python
import functools

import jax
import jax.numpy as jnp
from jax import lax
from jax.experimental import pallas as pl
from jax.experimental.pallas import tpu as pltpu

F32 = jnp.float32
BF16 = jnp.bfloat16

RMS_EPS = 1e-6
SSD_HEAD_DIM = 64
SSD_GROUPS = 2
SSD_STATE = 128
SSD_CHUNK = 128
ATT_HEADS = 4
ATT_HEAD_DIM = 64
IDX_HEADS = 4
IDX_DIM = 64
TOP_K = 256
Q_BLOCK = 128
ROPE_THETA = 500000.0
ROPE_ROT = 16
LRU_C = 8.0
CONV_W = 4
LANE = 128
SUBLANE = 8
KEY_CHUNK = 512
V_ONES_ROWS = 16
VMEM_LIMIT = 56 * 1024 * 1024


def _cparams(*sem):
    return pltpu.CompilerParams(dimension_semantics=sem, vmem_limit_bytes=VMEM_LIMIT)


def _const_spec(shape):
    nd = len(shape)
    return pl.BlockSpec(shape, lambda *_: (0,) * nd, pipeline_mode=pl.Buffered(1))


def _rms(x, g):
    ms = jnp.mean(x * x, axis=-1, keepdims=True)
    return x * lax.rsqrt(ms + RMS_EPS) * g


def _softplus(x):
    return jnp.maximum(x, 0.0) + jnp.log1p(jnp.exp(-jnp.abs(x)))


def _silu(x):
    return x * jax.nn.sigmoid(x)


def _split3(x):
    hi = x.astype(BF16)
    r = x - hi.astype(F32)
    mid = r.astype(BF16)
    return hi, mid, (r - mid.astype(F32)).astype(BF16)


def _dot_sel_rhs(x, sel):
    sel = sel.astype(BF16)
    return sum(jnp.dot(p, sel, preferred_element_type=F32) for p in _split3(x))


def _dot_sel_lhs(sel, x):
    sel = sel.astype(BF16)
    return sum(jnp.dot(sel, p, preferred_element_type=F32) for p in _split3(x))


def _ffn_body(*refs, d_ff, tf, n_mix, proj_widths):
    refs = list(refs)
    x = refs.pop(0)[...]
    if n_mix:
        y_refs = [refs.pop(0) for _ in range(n_mix)]
        wo_ref = refs.pop(0)
        off = 0
        for y_ref in y_refs:
            wd = y_ref.shape[1]
            x = x + jnp.dot(y_ref[...].astype(BF16), wo_ref[off:off + wd, :], preferred_element_type=F32)
            off += wd
    g_ref, w13_ref, w2_ref = refs.pop(0), refs.pop(0), refs.pop(0)
    if proj_widths:
        gp_ref, wi_ref = refs.pop(0), refs.pop(0)
    o_ref = refs.pop(0)

    h = _rms(x, g_ref[...]).astype(BF16)
    acc = jnp.zeros(x.shape, F32)
    for j in range(d_ff // tf):
        g = jnp.dot(h, w13_ref[:, j * tf:(j + 1) * tf], preferred_element_type=F32)
        u = jnp.dot(h, w13_ref[:, d_ff + j * tf:d_ff + (j + 1) * tf], preferred_element_type=F32)
        a = (_silu(g) * u).astype(BF16)
        acc = acc + jnp.dot(a, w2_ref[j * tf:(j + 1) * tf, :], preferred_element_type=F32)
    x = x + 0.5 * acc
    o_ref[...] = x

    if proj_widths:
        hp = _rms(x, gp_ref[...]).astype(BF16)
        off = 0
        for p_ref, wd in zip(refs, proj_widths):
            p_ref[...] = jnp.dot(hp, wi_ref[:, off:off + wd], preferred_element_type=F32)
            off += wd


def _ffn(x, g, w13, w2, *, mixed=None, proj=None, tm=512, tf=256):
    t, d = x.shape
    d_ff = w2.shape[0]
    row = lambda c: pl.BlockSpec((tm, c), lambda i: (i, 0))
    args, in_specs = [x], [row(d)]
    if mixed is not None:
        ys, w_out = mixed
        args += [*ys, w_out]
        in_specs += [row(y.shape[1]) for y in ys] + [_const_spec(w_out.shape)]
    args += [g.reshape(1, d), w13.astype(BF16), w2.astype(BF16)]
    in_specs += [_const_spec((1, d)), _const_spec((d, 2 * d_ff)), _const_spec((d_ff, d))]
    out_specs, out_shape, widths = [row(d)], [jax.ShapeDtypeStruct((t, d), F32)], ()
    if proj is not None:
        gp, w_in, widths = proj
        args += [gp.reshape(1, d), w_in]
        in_specs += [_const_spec((1, d)), _const_spec(w_in.shape)]
        out_specs += [row(wd) for wd in widths]
        out_shape += [jax.ShapeDtypeStruct((t, wd), F32) for wd in widths]
    outs = pl.pallas_call(
        functools.partial(_ffn_body, d_ff=d_ff, tf=tf, n_mix=len(mixed[0]) if mixed is not None else 0,
                          proj_widths=tuple(widths)),
        grid=(t // tm,),
        in_specs=in_specs,
        out_specs=out_specs,
        out_shape=out_shape,
        compiler_params=_cparams("parallel"),
        name="ffn",
    )(*args)
    return outs if proj is not None else outs[0]


def _causal_conv(x_ref, cw_ref, cb_ref, xe_scr, step, rows):
    @pl.when(step == 0)
    def _():
        xe_scr[0:SUBLANE, :] = jnp.zeros((SUBLANE, xe_scr.shape[1]), F32)

    @pl.when(step > 0)
    def _():
        xe_scr[0:SUBLANE, :] = xe_scr[rows:rows + SUBLANE, :]

    xe_scr[SUBLANE:SUBLANE + rows, :] = x_ref[...]
    y = cb_ref[...] + cw_ref[CONV_W - 1:CONV_W, :] * x_ref[...]
    for w in range(CONV_W - 1):
        off = SUBLANE - (CONV_W - 1) + w
        y = y + cw_ref[w:w + 1, :] * xe_scr[off:off + rows, :]
    return y


def _ssd_body(z_ref, xbc_ref, dtp_ref, cw_ref, cb_ref, dtb_ref, alog_ref, dsk_ref, ng_ref,
              y_ref, xe_scr, st_scr, *, q, d_ssd):
    step = pl.program_id(1)
    n = SSD_STATE
    gw = d_ssd // SSD_GROUPS
    rpg = gw // SSD_HEAD_DIM

    @pl.when(step == 0)
    def _():
        st_scr[...] = jnp.zeros(st_scr.shape, F32)

    xbc = _silu(_causal_conv(xbc_ref, cw_ref, cb_ref, xe_scr, step, q))
    xs = xbc[:, :d_ssd]
    bm = xbc[:, d_ssd:d_ssd + SSD_GROUPS * n]
    cm = xbc[:, d_ssd + SSD_GROUPS * n:]

    dt = _softplus(dtp_ref[...] + dtb_ref[...])
    adt = dt * (-jnp.exp(alog_ref[...]))
    ri = lax.broadcasted_iota(jnp.int32, (q, q), 0)
    ci = lax.broadcasted_iota(jnp.int32, (q, q), 1)
    causal = ri >= ci
    acum = _dot_sel_lhs(causal.astype(F32), adt)
    er = lax.broadcasted_iota(jnp.int32, (LANE, d_ssd), 0)
    ec = lax.broadcasted_iota(jnp.int32, (LANE, d_ssd), 1)
    expand = (ec // SSD_HEAD_DIM == er).astype(F32)
    a_x = _dot_sel_rhs(acum, expand)
    dt_x = _dot_sel_rhs(dt, expand)
    acum_t = acum.T
    a_last = a_x[q - 1:q, :]
    exp_a = jnp.exp(a_x)
    xdt = xs * dt_x
    xsw = xdt * jnp.exp(a_last - a_x)
    cdec = jnp.exp(a_last)

    ys = []
    for g in range(SSD_GROUPS):
        bm_g = bm[:, g * n:(g + 1) * n]
        cm_g = cm[:, g * n:(g + 1) * n].astype(BF16)
        cb = lax.dot_general(cm_g, bm_g.astype(BF16), (((1,), (1,)), ((), ())), preferred_element_type=F32)
        yd = []
        for r in range(rpg):
            h = g * rpg + r
            seg = acum[:, h:h + 1] - acum_t[h:h + 1, :]
            decay = jnp.exp(jnp.where(causal, seg, -jnp.inf))
            m = (cb * decay).astype(BF16)
            xh = xdt[:, h * SSD_HEAD_DIM:(h + 1) * SSD_HEAD_DIM].astype(BF16)
            yd.append(jnp.dot(m, xh, preferred_element_type=F32))
        yd = jnp.concatenate(yd, axis=1)
        gs = slice(g * gw, (g + 1) * gw)
        prev = st_scr[g]
        yoff = jnp.dot(cm_g, prev.astype(BF16), preferred_element_type=F32) * exp_a[:, gs]
        st = jnp.dot(bm_g.T.astype(BF16), xsw[:, gs].astype(BF16), preferred_element_type=F32)
        st_scr[g] = prev * cdec[:, gs] + st
        ys.append(yd + yoff + dsk_ref[:, gs] * xs[:, gs])

    z = z_ref[...]
    for g in range(SSD_GROUPS):
        gs = slice(g * gw, (g + 1) * gw)
        y_ref[:, gs] = _rms(ys[g] * _silu(z[:, gs]), ng_ref[:, gs])


def _ssd(z, xbc, dtp, conv_w, conv_b, dt_bias, a_log, d_skip, norm_g, *, bsz):
    t, d_ssd = z.shape
    cch = xbc.shape[1]
    q = SSD_CHUNK
    nc = t // bsz // q
    heads = d_ssd // SSD_HEAD_DIM
    pad = lambda v, fill: jnp.concatenate([v, jnp.full((LANE - heads,), fill, F32)]).reshape(1, LANE)
    row = lambda c: pl.BlockSpec((q, c), lambda b, i: (b * nc + i, 0))
    return pl.pallas_call(
        functools.partial(_ssd_body, q=q, d_ssd=d_ssd),
        grid=(bsz, nc),
        in_specs=[row(d_ssd), row(cch), row(LANE), _const_spec((CONV_W, cch)), _const_spec((1, cch)),
                  _const_spec((1, LANE)), _const_spec((1, LANE)), _const_spec((1, d_ssd)),
                  _const_spec((1, d_ssd))],
        out_specs=row(d_ssd),
        out_shape=jax.ShapeDtypeStruct((t, d_ssd), F32),
        scratch_shapes=[pltpu.VMEM((q + SUBLANE, cch), F32),
                        pltpu.VMEM((SSD_GROUPS, SSD_STATE, d_ssd // SSD_GROUPS), F32)],
        compiler_params=_cparams("parallel", "arbitrary"),
        name="ssd",
    )(z, xbc, dtp, conv_w, conv_b.reshape(1, cch), pad(dt_bias, 0.0), pad(a_log, 0.0),
      jnp.repeat(d_skip, SSD_HEAD_DIM).reshape(1, d_ssd), norm_g.reshape(1, d_ssd))


def _lru_body(xl_ref, gl_ref, cw_ref, cb_ref, wa_ref, ba_ref, wi_ref, bi_ref, lam_ref,
              y_ref, xe_scr, h_scr, *, tl):
    step = pl.program_id(1)

    @pl.when(step == 0)
    def _():
        h_scr[...] = jnp.zeros(h_scr.shape, F32)

    xr = _causal_conv(xl_ref, cw_ref, cb_ref, xe_scr, step, tl)
    xrb = xr.astype(BF16)
    r = jax.nn.sigmoid(jnp.dot(xrb, wa_ref[...], preferred_element_type=F32) + ba_ref[...])
    i = jax.nn.sigmoid(jnp.dot(xrb, wi_ref[...], preferred_element_type=F32) + bi_ref[...])
    log_a = -LRU_C * r * _softplus(-lam_ref[...])
    a = jnp.exp(log_a)
    b = jnp.sqrt(-jnp.tanh(log_a) * (a * a + 1.0)) * (i * xr)
    rows = lax.broadcasted_iota(jnp.int32, a.shape, 0)
    s = 1
    while s < tl:
        keep = rows >= s
        a_sh = jnp.where(keep, pltpu.roll(a, s, 0), 1.0)
        b_sh = jnp.where(keep, pltpu.roll(b, s, 0), 0.0)
        b = a * b_sh + b
        a = a * a_sh
        s *= 2
    hs = a * h_scr[0:1, :] + b
    h_scr[...] = jnp.broadcast_to(hs[tl - 1:tl, :], h_scr.shape)
    y_ref[...] = hs * jax.nn.gelu(gl_ref[...])


def _block_diag(w):
    nb, bw, _ = w.shape
    out = jnp.zeros((nb * bw, nb * bw), w.dtype)
    for k in range(nb):
        out = out.at[k * bw:(k + 1) * bw, k * bw:(k + 1) * bw].set(w[k])
    return out


def _lru(xl, gl, conv_w, conv_b, wa, ba, wi, bi, lam, *, bsz, tl=256):
    t, dl = xl.shape
    nt = t // bsz // tl
    row = pl.BlockSpec((tl, dl), lambda b, i: (b * nt + i, 0))
    vec = _const_spec((1, dl))
    return pl.pallas_call(
        functools.partial(_lru_body, tl=tl),
        grid=(bsz, nt),
        in_specs=[row, row, _const_spec((CONV_W, dl)), vec, _const_spec((dl, dl)), vec,
                  _const_spec((dl, dl)), vec, vec],
        out_specs=row,
        out_shape=jax.ShapeDtypeStruct((t, dl), F32),
        scratch_shapes=[pltpu.VMEM((tl + SUBLANE, dl), F32), pltpu.VMEM((SUBLANE, dl), F32)],
        compiler_params=_cparams("parallel", "arbitrary"),
        name="rglru",
    )(xl, gl, conv_w, conv_b.reshape(1, dl), _block_diag(wa).astype(BF16), ba.reshape(1, dl),
      _block_diag(wi).astype(BF16), bi.reshape(1, dl), lam.reshape(1, dl))


def _rope(x, ct, st, axis):
    width = x.shape[axis]
    fm = lax.broadcasted_iota(jnp.int32, x.shape, axis) % ATT_HEAD_DIM
    half = ROPE_ROT // 2
    partner = jnp.where(fm < half, pltpu.roll(x, width - half, axis), pltpu.roll(x, half, axis))
    return x * ct + partner * st


def _dot_nt(a, b):
    return lax.dot_general(a, b, (((1,), (1,)), ((), ())), preferred_element_type=F32)


def _prep_body(cq_ref, ckv_ref, misc_ref, ct_ref, st_ref, ctt_ref, stt_ref, cqg_ref, ckvg_ref, wuqt_ref,
               wuk_ref, wuvt_ref, qg_ref, kg_ref, wqit_ref, kig_ref,
               k_ref, ki_ref, vt_ref, qt_ref, qit_ref, wc_ref, *, tm):
    ct = ct_ref[...]
    st = st_ref[...]
    d_att = wuk_ref.shape[1]
    gr = lax.broadcasted_iota(jnp.int32, (d_att, d_att), 0) // ATT_HEAD_DIM
    gc = lax.broadcasted_iota(jnp.int32, (d_att, d_att), 1) // ATT_HEAD_DIM
    head_mean = jnp.where(gr == gc, 1.0 / ATT_HEAD_DIM, 0.0).astype(F32)

    cqn = _rms(cq_ref[...], cqg_ref[...]).astype(BF16)
    ckvn = _rms(ckv_ref[...], ckvg_ref[...]).astype(BF16)

    qt = _dot_nt(wuqt_ref[...], cqn)
    ms = _dot_sel_lhs(head_mean, qt * qt)
    qt = qt * lax.rsqrt(ms + RMS_EPS) * qg_ref[...]
    qt_ref[...] = (_rope(qt, ctt_ref[...], stt_ref[...], 0) * ATT_HEAD_DIM ** -0.5).astype(BF16)
    vt_ref[0:d_att, :] = _dot_nt(wuvt_ref[...], ckvn).astype(BF16)
    vt_ref[d_att:, :] = jnp.ones((vt_ref.shape[0] - d_att, tm), BF16)

    k = jnp.dot(ckvn, wuk_ref[...], preferred_element_type=F32)
    ms = _dot_sel_rhs(k * k, head_mean)
    k_ref[...] = _rope(k * lax.rsqrt(ms + RMS_EPS) * kg_ref[...], ct, st, 1).astype(BF16)

    qit = _rope(_dot_nt(wqit_ref[...], cqn), ctt_ref[...], stt_ref[...], 0)
    misc = misc_ref[...]
    misc_t = misc.T
    wscale = IDX_HEADS ** -0.5 * IDX_DIM ** -0.5
    for j in range(tm // Q_BLOCK):
        cols = slice(j * Q_BLOCK, (j + 1) * Q_BLOCK)
        for h in range(IDX_HEADS):
            qit_ref[j, 0:IDX_DIM, h * Q_BLOCK:(h + 1) * Q_BLOCK] = (
                qit[h * IDX_DIM:(h + 1) * IDX_DIM, cols].astype(BF16))
        qit_ref[j, IDX_DIM:, :] = jnp.zeros((LANE - IDX_DIM, IDX_HEADS * Q_BLOCK), BF16)
        wrow = jnp.concatenate([misc_t[IDX_DIM + h:IDX_DIM + h + 1, cols] for h in range(IDX_HEADS)], axis=1)
        wc_ref[j] = jnp.broadcast_to(wrow * wscale, (SUBLANE, IDX_HEADS * Q_BLOCK))

    lane = lax.broadcasted_iota(jnp.int32, misc.shape, 1)
    km = jnp.where(lane < IDX_DIM, misc, 0.0)
    ms = jnp.sum(km * km, axis=-1, keepdims=True) * (1.0 / IDX_DIM)
    kin = km * lax.rsqrt(ms + RMS_EPS) * kig_ref[...]
    ki_ref[...] = _rope(kin, ct[:, :LANE], st[:, :LANE], 1).astype(BF16)


def _rope_lane_tables(length, width):
    half = ROPE_ROT // 2
    inv = ROPE_THETA ** (-jnp.arange(half, dtype=F32) * 2.0 / ROPE_ROT)
    ang = jnp.arange(length, dtype=F32)[:, None] * inv[None, :]
    cos, sin = jnp.cos(ang), jnp.sin(ang)
    ones = jnp.ones((length, ATT_HEAD_DIM - ROPE_ROT), F32)
    ct = jnp.concatenate([cos, cos, ones], axis=1)
    st = jnp.concatenate([-sin, sin, 0.0 * ones], axis=1)
    reps = width // ATT_HEAD_DIM
    return jnp.tile(ct, (1, reps)), jnp.tile(st, (1, reps))


def _dsa_prep(cq, ckv, misc, p, *, bsz):
    t, qr = cq.shape
    kvr = ckv.shape[1]
    tm = KEY_CHUNK
    length = t // bsz
    nt = length // tm
    d_att = ATT_HEADS * ATT_HEAD_DIM
    ct, st = _rope_lane_tables(length, d_att)
    wukv = p["w_ukv"].reshape(kvr, ATT_HEADS, 2, ATT_HEAD_DIM)
    wuk = wukv[:, :, 0, :].reshape(kvr, d_att).astype(BF16)
    wuvt = wukv[:, :, 1, :].reshape(kvr, d_att).T.astype(BF16)
    kig = jnp.concatenate([p["kidx_norm"], jnp.zeros((LANE - IDX_DIM,), F32)]).reshape(1, LANE)
    row = lambda c: pl.BlockSpec((tm, c), lambda b, i: (b * nt + i, 0))
    tab = pl.BlockSpec((tm, d_att), lambda b, i: (i, 0))
    tab_t = pl.BlockSpec((d_att, tm), lambda b, i: (0, i))
    nqb = tm // Q_BLOCK
    hq = IDX_HEADS * Q_BLOCK
    return pl.pallas_call(
        functools.partial(_prep_body, tm=tm),
        grid=(bsz, nt),
        in_specs=[row(qr), row(kvr), row(LANE), tab, tab, tab_t, tab_t, _const_spec((1, qr)),
                  _const_spec((1, kvr)), _const_spec((d_att, qr)), _const_spec((kvr, d_att)),
                  _const_spec((d_att, kvr)), _const_spec((d_att, 1)), _const_spec((1, d_att)),
                  _const_spec((IDX_HEADS * IDX_DIM, qr)), _const_spec((1, LANE))],
        out_specs=[row(d_att), row(LANE),
                   pl.BlockSpec((None, d_att + V_ONES_ROWS, tm), lambda b, i: (b * nt + i, 0, 0)),
                   pl.BlockSpec((None, d_att, tm), lambda b, i: (b * nt + i, 0, 0)),
                   pl.BlockSpec((nqb, LANE, hq), lambda b, i: (b * nt + i, 0, 0)),
                   pl.BlockSpec((nqb, SUBLANE, hq), lambda b, i: (b * nt + i, 0, 0))],
        out_shape=[jax.ShapeDtypeStruct((t, d_att), BF16),
                   jax.ShapeDtypeStruct((t, LANE), BF16),
                   jax.ShapeDtypeStruct((bsz * nt, d_att + V_ONES_ROWS, tm), BF16),
                   jax.ShapeDtypeStruct((bsz * nt, d_att, tm), BF16),
                   jax.ShapeDtypeStruct((t // Q_BLOCK, LANE, hq), BF16),
                   jax.ShapeDtypeStruct((t // Q_BLOCK, SUBLANE, hq), F32)],
        compiler_params=_cparams("parallel", "parallel"),
        name="dsa_prep",
    )(cq, ckv, misc, ct, st, ct.T, st.T, p["cq_norm"].reshape(1, qr), p["ckv_norm"].reshape(1, kvr),
      p["w_uq"].T.astype(BF16), wuk, wuvt, jnp.tile(p["q_norm"], ATT_HEADS).reshape(d_att, 1),
      jnp.tile(p["k_norm"], ATT_HEADS).reshape(1, d_att), p["w_qidx"].T.astype(BF16), kig)


INT_MIN = -2 ** 31
F32_TINY = 2.0 ** -126
F32_LOWEST = -3.4028234663852886e38
SEARCH_VALUE_STEPS = 16
SEARCH_CAP = 80


def _order_key_to_float(u):
    sk = u ^ jnp.int32(INT_MIN)
    fb = sk ^ ((sk >> 31) & jnp.int32(0x7FFFFFFF))
    return lax.bitcast_convert_type(fb, F32)


def _float_to_order_key(f):
    b = lax.bitcast_convert_type(f, jnp.int32)
    return (b ^ ((b >> 31) & jnp.int32(0x7FFFFFFF))) ^ jnp.int32(INT_MIN)


def _fold_rows(x, op):
    parts = [x[i * SUBLANE:(i + 1) * SUBLANE, :] for i in range(x.shape[0] // SUBLANE)]
    while len(parts) > 1:
        parts = [op(parts[i], parts[i + 1]) for i in range(0, len(parts), 2)]
    return parts[0]


def _dsa_body(qit_ref, wc_ref, qt_ref, ki_ref, k_ref, vt_ref, o_ref,
              sc_scr, lg_scr, qbd_scr, oacc_scr, *, n_sel):
    kc = KEY_CHUNK
    qb = pl.program_id(1)
    nk = qb // (kc // Q_BLOCK) + 1
    qpos = qb * Q_BLOCK + lax.broadcasted_iota(jnp.int32, (1, Q_BLOCK), 1)
    hq = IDX_HEADS * Q_BLOCK
    k_f = jnp.float32(n_sel)

    def chunk(c):
        return pl.ds(pl.multiple_of(c * kc, kc), kc)

    qit = qit_ref[...]
    wc = wc_ref[0:1, :]

    def score_chunk(c, amax):
        s4 = jnp.dot(ki_ref[chunk(c), :], qit, preferred_element_type=F32)
        s4 = jnp.maximum(s4, 0.0) * wc
        s = (s4[:, 0:Q_BLOCK] + s4[:, Q_BLOCK:2 * Q_BLOCK]) + (s4[:, 2 * Q_BLOCK:3 * Q_BLOCK] + s4[:, 3 * Q_BLOCK:])
        kpos = c * kc + lax.broadcasted_iota(jnp.int32, (kc, Q_BLOCK), 0)
        sc_scr[chunk(c), :] = jnp.where(kpos <= qpos, s, -jnp.inf)
        return jnp.maximum(amax, _fold_rows(jnp.abs(s), jnp.maximum))

    amax = lax.fori_loop(0, nk, score_chunk, jnp.zeros((SUBLANE, Q_BLOCK), F32))
    amax = jnp.max(amax, axis=0, keepdims=True)

    def count_ge(*thrs):
        def body(c, accs):
            blk = sc_scr[chunk(c), :]
            return tuple(a + _fold_rows(jnp.where(blk >= t, 1.0, 0.0), jnp.add) for a, t in zip(accs, thrs))
        accs = lax.fori_loop(0, nk, body, tuple(jnp.zeros((SUBLANE, Q_BLOCK), F32) for _ in thrs))
        return tuple(jnp.sum(a, axis=0, keepdims=True) for a in accs)

    n_valid = (qpos + 1).astype(F32)
    take_all = n_valid <= k_f
    c_pos, c_nn = count_ge(jnp.full((1, Q_BLOCK), F32_TINY, F32), jnp.zeros((1, Q_BLOCK), F32))
    zero_tie = (c_pos < k_f) & (c_nn >= k_f)
    pos = c_pos >= k_f
    hi_top = _order_key_to_float(_float_to_order_key(amax) + 1)
    lo = jnp.where(take_all, F32_LOWEST, jnp.where(zero_tie, 0.0, jnp.where(pos, F32_TINY, -amax)))
    hi = jnp.where(take_all, jnp.inf, jnp.where(zero_tie, F32_TINY, jnp.where(pos, hi_top, 0.0)))
    cnt_lo = jnp.where(take_all | (~zero_tie & ~pos), n_valid, jnp.where(zero_tie, c_nn, c_pos))
    cnt_hi = jnp.where(take_all | (~zero_tie & pos), 0.0, jnp.where(zero_tie, c_pos, c_nn))
    settled = take_all | zero_tie

    def search_step(st, on_values):
        lk, hk, lo, hi, cnt_lo, cnt_hi = st
        midk = lk + ((hk - lk) >> 1)
        mid = _order_key_to_float(midk)
        if on_values:
            vmid = lo + (hi - lo) * 0.5
            inside = (vmid > lo) & (vmid < hi)
            mid = jnp.where(inside, vmid, mid)
            midk = jnp.where(inside, _float_to_order_key(vmid), midk)
        cnt, = count_ge(mid)
        ge = cnt >= k_f
        return (jnp.where(ge, midk, lk), jnp.where(ge, hk, midk), jnp.where(ge, mid, lo), jnp.where(ge, hi, mid),
                jnp.where(ge, cnt, cnt_lo), jnp.where(ge, cnt_hi, cnt))

    def search_cond(carry):
        it, (lk, hk, _, _, cnt_lo, _) = carry
        open_ = jnp.logical_not(settled | (cnt_lo == k_f) | (hk - lk <= 1))
        return (it < SEARCH_CAP) & (jnp.sum(jnp.where(open_, 1.0, 0.0)) > 0.0)

    def search_pair(carry):
        it, st = carry
        return it + 1, search_step(search_step(st, False), False)

    st = (_float_to_order_key(lo), _float_to_order_key(hi), lo, hi, cnt_lo, cnt_hi)
    st = lax.fori_loop(0, SEARCH_VALUE_STEPS, lambda _, s: search_step(s, True), st)
    _, (_, _, lo, hi, cnt_lo, cnt_hi) = lax.while_loop(search_cond, search_pair, (jnp.int32(0), st))
    need = k_f - cnt_hi

    sub = Q_BLOCK
    ri = lax.broadcasted_iota(jnp.int32, (sub, sub), 0)
    ci = lax.broadcasted_iota(jnp.int32, (sub, sub), 1)
    before = (ri > ci).astype(BF16)

    qbd_scr[...] = jnp.zeros(qbd_scr.shape, BF16)
    for h in range(ATT_HEADS):
        rows = slice(h * ATT_HEAD_DIM, (h + 1) * ATT_HEAD_DIM)
        qbd_scr[rows, h * Q_BLOCK:(h + 1) * Q_BLOCK] = qt_ref[rows, :]
    qbd = qbd_scr[...]

    def logit_chunk(c, carry):
        m, seen = carry
        for j in range(kc // sub):
            rows = pl.ds(pl.multiple_of(c * kc + j * sub, sub), sub)
            blk = sc_scr[rows, :]
            in_hi = blk >= hi
            tie = (blk >= lo) & jnp.logical_not(in_hi)
            tie_f = jnp.where(tie, 1.0, 0.0)
            rank = jnp.dot(before, tie_f.astype(BF16), preferred_element_type=F32) + seen
            sel = in_hi | (tie & (rank < need))
            seen = seen + jnp.sum(tie_f, axis=0, keepdims=True)
            b = jnp.where(sel, 0.0, -jnp.inf)
            lg = jnp.dot(k_ref[rows, :], qbd, preferred_element_type=F32) + jnp.concatenate([b] * ATT_HEADS, axis=1)
            lg_scr[rows, :] = lg
            m = jnp.maximum(m, jnp.max(lg, axis=0, keepdims=True))
        return m, seen

    m, _ = lax.fori_loop(0, nk, logit_chunk, (jnp.full((1, hq), -jnp.inf, F32), jnp.zeros((1, Q_BLOCK), F32)))
    oacc_scr[...] = jnp.zeros(oacc_scr.shape, F32)

    def value_chunk(c, carry):
        acc = oacc_scr[...]
        depth = 2 * Q_BLOCK
        for j in range(kc // depth):
            rows = pl.ds(pl.multiple_of(c * kc + j * depth, depth), depth)
            p = jnp.exp(lg_scr[rows, :] - m).astype(BF16)
            acc = acc + jnp.dot(vt_ref[c, :, j * depth:(j + 1) * depth], p, preferred_element_type=F32)
        oacc_scr[...] = acc
        return carry

    lax.fori_loop(0, nk, value_chunk, 0)
    d_att = ATT_HEADS * ATT_HEAD_DIM
    outs = []
    for h in range(ATT_HEADS):
        cols = slice(h * Q_BLOCK, (h + 1) * Q_BLOCK)
        ssum = oacc_scr[d_att:d_att + 1, cols]
        outs.append(oacc_scr[h * ATT_HEAD_DIM:(h + 1) * ATT_HEAD_DIM, cols] / ssum)
    o_ref[...] = jnp.concatenate(outs, axis=0).T


def _dsa(k, ki, vt, qt, qit, wc, *, bsz):
    t, d_att = k.shape
    length = t // bsz
    nb = length // Q_BLOCK
    kc = KEY_CHUNK
    nkc = length // kc
    hq = IDX_HEADS * Q_BLOCK
    per_q = kc // Q_BLOCK
    v_rows = d_att + V_ONES_ROWS
    vt = vt.reshape(bsz, nkc, v_rows, kc)
    return pl.pallas_call(
        functools.partial(_dsa_body, n_sel=min(TOP_K, length // 4)),
        grid=(bsz, nb),
        in_specs=[pl.BlockSpec((None, LANE, hq), lambda b, i: (b * nb + i, 0, 0)),
                  pl.BlockSpec((None, SUBLANE, hq), lambda b, i: (b * nb + i, 0, 0)),
                  pl.BlockSpec((None, d_att, Q_BLOCK), lambda b, i: (b * nkc + i // per_q, 0, i % per_q)),
                  pl.BlockSpec((length, LANE), lambda b, i: (b, 0)),
                  pl.BlockSpec((length, d_att), lambda b, i: (b, 0)),
                  pl.BlockSpec((None, nkc, v_rows, kc), lambda b, i: (b, 0, 0, 0))],
        out_specs=pl.BlockSpec((Q_BLOCK, d_att), lambda b, i: (b * nb + i, 0)),
        out_shape=jax.ShapeDtypeStruct((t, d_att), F32),
        scratch_shapes=[pltpu.VMEM((length, Q_BLOCK), F32), pltpu.VMEM((length, hq), F32),
                        pltpu.VMEM((d_att, hq), BF16), pltpu.VMEM((v_rows, hq), F32)],
        compiler_params=_cparams("parallel", "arbitrary"),
        name="dsa",
    )(qit, wc, qt, ki, k, vt)


def _pad_cols(w, width):
    return jnp.concatenate([w, jnp.zeros((w.shape[0], width - w.shape[1]), w.dtype)], axis=1)


def kernel(x, norm_ffn1, ffn1_w13, ffn1_w2, norm_mix, w_in, ssd_conv_w, ssd_conv_b, ssd_dt_bias, ssd_a_log, ssd_d, ssd_norm, cq_norm, ckv_norm, w_uq, w_ukv, q_norm, k_norm, w_qidx, kidx_norm, lru_conv_w, lru_conv_b, lru_wa, lru_ba, lru_wi, lru_bi, lru_lambda, w_out, norm_ffn2, ffn2_w13, ffn2_w2):
    bsz, length, d = x.shape
    depth = w_in.shape[0]
    d_ssd = ssd_norm.shape[1]
    conv_ch = ssd_conv_b.shape[1]
    heads = ssd_d.shape[1]
    q_rank, kv_rank = cq_norm.shape[1], ckv_norm.shape[1]
    d_lru = lru_lambda.shape[1]
    sizes = [d_ssd, conv_ch, heads, q_rank, kv_rank, IDX_DIM, IDX_HEADS, d_lru, d_lru]
    offs = [0]
    for s in sizes:
        offs.append(offs[-1] + s)
    col = lambda wl, k: wl[:, offs[k]:offs[k + 1]]
    widths = (d_ssd, conv_ch, LANE, q_rank, kv_rank, LANE, d_lru, d_lru)

    xt = x.reshape(bsz * length, d)
    for l in range(depth):
        wl = w_in[l]
        w_pad = jnp.concatenate(
            [col(wl, 0), col(wl, 1), _pad_cols(col(wl, 2), LANE), col(wl, 3), col(wl, 4),
             _pad_cols(jnp.concatenate([col(wl, 5), col(wl, 6)], axis=1), LANE), col(wl, 7), col(wl, 8)],
            axis=1).astype(BF16)
        xt, z, xbc, dtp, cq, ckv, misc, xl, gl = _ffn(xt, norm_ffn1[l], ffn1_w13[l], ffn1_w2[l],
                                                      proj=(norm_mix[l], w_pad, widths))
        y_ssd = _ssd(z, xbc, dtp, ssd_conv_w[l], ssd_conv_b[l], ssd_dt_bias[l], ssd_a_log[l], ssd_d[l],
                     ssd_norm[l], bsz=bsz)
        y_lru = _lru(xl, gl, lru_conv_w[l], lru_conv_b[l], lru_wa[l], lru_ba[l], lru_wi[l], lru_bi[l],
                     lru_lambda[l], bsz=bsz)
        prm = dict(cq_norm=cq_norm[l], ckv_norm=ckv_norm[l], w_uq=w_uq[l], w_ukv=w_ukv[l], q_norm=q_norm[l],
                   k_norm=k_norm[l], w_qidx=w_qidx[l], kidx_norm=kidx_norm[l])
        k, ki, vt, qt, qit, wc = _dsa_prep(cq, ckv, misc, prm, bsz=bsz)
        y_att = _dsa(k, ki, vt, qt, qit, wc, bsz=bsz)
        xt = _ffn(xt, norm_ffn2[l], ffn2_w13[l], ffn2_w2[l],
                  mixed=([y_ssd, y_att, y_lru], w_out[l].astype(BF16)))
    return xt.reshape(bsz, length, d)
```

```python
import functools

import jax
import jax.numpy as jnp
from jax import lax
from jax.experimental import pallas as pl
from jax.experimental.pallas import tpu as pltpu

F32 = jnp.float32
BF16 = jnp.bfloat16

RMS_EPS = 1e-6
SSD_HEAD_DIM = 64
SSD_GROUPS = 2
SSD_STATE = 128
SSD_CHUNK = 128
ATT_HEADS = 4
ATT_HEAD_DIM = 64
IDX_HEADS = 4
IDX_DIM = 64
TOP_K = 256
Q_BLOCK = 128
ROPE_THETA = 500000.0
ROPE_ROT = 16
LRU_C = 8.0
CONV_W = 4
LANE = 128
SUBLANE = 8
KEY_CHUNK = 512
V_ONES_ROWS = 16
VMEM_LIMIT = 56 * 1024 * 1024


def _cparams(*sem):
    return pltpu.CompilerParams(dimension_semantics=sem, vmem_limit_bytes=VMEM_LIMIT)


def _const_spec(shape):
    nd = len(shape)
    return pl.BlockSpec(shape, lambda *_: (0,) * nd, pipeline_mode=pl.Buffered(1))


def _rms(x, g):
    ms = jnp.mean(x * x, axis=-1, keepdims=True)
    return x * lax.rsqrt(ms + RMS_EPS) * g


def _softplus(x):
    return jnp.maximum(x, 0.0) + jnp.log1p(jnp.exp(-jnp.abs(x)))


def _silu(x):
    return x * jax.nn.sigmoid(x)


def _split3(x):
    hi = x.astype(BF16)
    r = x - hi.astype(F32)
    mid = r.astype(BF16)
    return hi, mid, (r - mid.astype(F32)).astype(BF16)


def _dot_sel_rhs(x, sel):
    sel = sel.astype(BF16)
    return sum(jnp.dot(p, sel, preferred_element_type=F32) for p in _split3(x))


def _dot_sel_lhs(sel, x):
    sel = sel.astype(BF16)
    return sum(jnp.dot(sel, p, preferred_element_type=F32) for p in _split3(x))


def _ffn_body(*refs, d_ff, tf, n_mix, proj_widths):
    refs = list(refs)
    x = refs.pop(0)[...]
    if n_mix:
        y_refs = [refs.pop(0) for _ in range(n_mix)]
        wo_ref = refs.pop(0)
        off = 0
        for y_ref in y_refs:
            wd = y_ref.shape[1]
            x = x + jnp.dot(y_ref[...].astype(BF16), wo_ref[off:off + wd, :], preferred_element_type=F32)
            off += wd
    g_ref, w13_ref, w2_ref = refs.pop(0), refs.pop(0), refs.pop(0)
    if proj_widths:
        gp_ref, wi_ref = refs.pop(0), refs.pop(0)
    o_ref = refs.pop(0)

    h = _rms(x, g_ref[...]).astype(BF16)
    acc = jnp.zeros(x.shape, F32)
    for j in range(d_ff // tf):
        g = jnp.dot(h, w13_ref[:, j * tf:(j + 1) * tf], preferred_element_type=F32)
        u = jnp.dot(h, w13_ref[:, d_ff + j * tf:d_ff + (j + 1) * tf], preferred_element_type=F32)
        a = (_silu(g) * u).astype(BF16)
        acc = acc + jnp.dot(a, w2_ref[j * tf:(j + 1) * tf, :], preferred_element_type=F32)
    x = x + 0.5 * acc
    o_ref[...] = x

    if proj_widths:
        hp = _rms(x, gp_ref[...]).astype(BF16)
        off = 0
        for p_ref, wd in zip(refs, proj_widths):
            p_ref[...] = jnp.dot(hp, wi_ref[:, off:off + wd], preferred_element_type=F32)
            off += wd


def _ffn(x, g, w13, w2, *, mixed=None, proj=None, tm=512, tf=256):
    t, d = x.shape
    d_ff = w2.shape[0]
    row = lambda c: pl.BlockSpec((tm, c), lambda i: (i, 0))
    args, in_specs = [x], [row(d)]
    if mixed is not None:
        ys, w_out = mixed
        args += [*ys, w_out]
        in_specs += [row(y.shape[1]) for y in ys] + [_const_spec(w_out.shape)]
    args += [g.reshape(1, d), w13.astype(BF16), w2.astype(BF16)]
    in_specs += [_const_spec((1, d)), _const_spec((d, 2 * d_ff)), _const_spec((d_ff, d))]
    out_specs, out_shape, widths = [row(d)], [jax.ShapeDtypeStruct((t, d), F32)], ()
    if proj is not None:
        gp, w_in, widths = proj
        args += [gp.reshape(1, d), w_in]
        in_specs += [_const_spec((1, d)), _const_spec(w_in.shape)]
        out_specs += [row(wd) for wd in widths]
        out_shape += [jax.ShapeDtypeStruct((t, wd), F32) for wd in widths]
    outs = pl.pallas_call(
        functools.partial(_ffn_body, d_ff=d_ff, tf=tf, n_mix=len(mixed[0]) if mixed is not None else 0,
                          proj_widths=tuple(widths)),
        grid=(t // tm,),
        in_specs=in_specs,
        out_specs=out_specs,
        out_shape=out_shape,
        compiler_params=_cparams("parallel"),
        name="ffn",
    )(*args)
    return outs if proj is not None else outs[0]


def _causal_conv(x_ref, cw_ref, cb_ref, xe_scr, step, rows):
    @pl.when(step == 0)
    def _():
        xe_scr[0:SUBLANE, :] = jnp.zeros((SUBLANE, xe_scr.shape[1]), F32)

    @pl.when(step > 0)
    def _():
        xe_scr[0:SUBLANE, :] = xe_scr[rows:rows + SUBLANE, :]

    xe_scr[SUBLANE:SUBLANE + rows, :] = x_ref[...]
    y = cb_ref[...] + cw_ref[CONV_W - 1:CONV_W, :] * x_ref[...]
    for w in range(CONV_W - 1):
        off = SUBLANE - (CONV_W - 1) + w
        y = y + cw_ref[w:w + 1, :] * xe_scr[off:off + rows, :]
    return y


def _ssd_body(z_ref, xbc_ref, dtp_ref, cw_ref, cb_ref, dtb_ref, alog_ref, dsk_ref, ng_ref,
              y_ref, xe_scr, st_scr, *, q, d_ssd):
    step = pl.program_id(1)
    n = SSD_STATE
    gw = d_ssd // SSD_GROUPS
    rpg = gw // SSD_HEAD_DIM

    @pl.when(step == 0)
    def _():
        st_scr[...] = jnp.zeros(st_scr.shape, F32)

    xbc = _silu(_causal_conv(xbc_ref, cw_ref, cb_ref, xe_scr, step, q))
    xs = xbc[:, :d_ssd]
    bm = xbc[:, d_ssd:d_ssd + SSD_GROUPS * n]
    cm = xbc[:, d_ssd + SSD_GROUPS * n:]

    dt = _softplus(dtp_ref[...] + dtb_ref[...])
    adt = dt * (-jnp.exp(alog_ref[...]))
    ri = lax.broadcasted_iota(jnp.int32, (q, q), 0)
    ci = lax.broadcasted_iota(jnp.int32, (q, q), 1)
    causal = ri >= ci
    acum = _dot_sel_lhs(causal.astype(F32), adt)
    er = lax.broadcasted_iota(jnp.int32, (LANE, d_ssd), 0)
    ec = lax.broadcasted_iota(jnp.int32, (LANE, d_ssd), 1)
    expand = (ec // SSD_HEAD_DIM == er).astype(F32)
    a_x = _dot_sel_rhs(acum, expand)
    dt_x = _dot_sel_rhs(dt, expand)
    acum_t = acum.T
    a_last = a_x[q - 1:q, :]
    exp_a = jnp.exp(a_x)
    xdt = xs * dt_x
    xsw = xdt * jnp.exp(a_last - a_x)
    cdec = jnp.exp(a_last)

    ys = []
    for g in range(SSD_GROUPS):
        bm_g = bm[:, g * n:(g + 1) * n]
        cm_g = cm[:, g * n:(g + 1) * n].astype(BF16)
        cb = lax.dot_general(cm_g, bm_g.astype(BF16), (((1,), (1,)), ((), ())), preferred_element_type=F32)
        yd = []
        for r in range(rpg):
            h = g * rpg + r
            seg = acum[:, h:h + 1] - acum_t[h:h + 1, :]
            decay = jnp.exp(jnp.where(causal, seg, -jnp.inf))
            m = (cb * decay).astype(BF16)
            xh = xdt[:, h * SSD_HEAD_DIM:(h + 1) * SSD_HEAD_DIM].astype(BF16)
            yd.append(jnp.dot(m, xh, preferred_element_type=F32))
        yd = jnp.concatenate(yd, axis=1)
        gs = slice(g * gw, (g + 1) * gw)
        prev = st_scr[g]
        yoff = jnp.dot(cm_g, prev.astype(BF16), preferred_element_type=F32) * exp_a[:, gs]
        st = jnp.dot(bm_g.T.astype(BF16), xsw[:, gs].astype(BF16), preferred_element_type=F32)
        st_scr[g] = prev * cdec[:, gs] + st
        ys.append(yd + yoff + dsk_ref[:, gs] * xs[:, gs])

    z = z_ref[...]
    for g in range(SSD_GROUPS):
        gs = slice(g * gw, (g + 1) * gw)
        y_ref[:, gs] = _rms(ys[g] * _silu(z[:, gs]), ng_ref[:, gs])


def _ssd(z, xbc, dtp, conv_w, conv_b, dt_bias, a_log, d_skip, norm_g, *, bsz):
    t, d_ssd = z.shape
    cch = xbc.shape[1]
    q = SSD_CHUNK
    nc = t // bsz // q
    heads = d_ssd // SSD_HEAD_DIM
    pad = lambda v, fill: jnp.concatenate([v, jnp.full((LANE - heads,), fill, F32)]).reshape(1, LANE)
    row = lambda c: pl.BlockSpec((q, c), lambda b, i: (b * nc + i, 0))
    return pl.pallas_call(
        functools.partial(_ssd_body, q=q, d_ssd=d_ssd),
        grid=(bsz, nc),
        in_specs=[row(d_ssd), row(cch), row(LANE), _const_spec((CONV_W, cch)), _const_spec((1, cch)),
                  _const_spec((1, LANE)), _const_spec((1, LANE)), _const_spec((1, d_ssd)),
                  _const_spec((1, d_ssd))],
        out_specs=row(d_ssd),
        out_shape=jax.ShapeDtypeStruct((t, d_ssd), F32),
        scratch_shapes=[pltpu.VMEM((q + SUBLANE, cch), F32),
                        pltpu.VMEM((SSD_GROUPS, SSD_STATE, d_ssd // SSD_GROUPS), F32)],
        compiler_params=_cparams("parallel", "arbitrary"),
        name="ssd",
    )(z, xbc, dtp, conv_w, conv_b.reshape(1, cch), pad(dt_bias, 0.0), pad(a_log, 0.0),
      jnp.repeat(d_skip, SSD_HEAD_DIM).reshape(1, d_ssd), norm_g.reshape(1, d_ssd))


def _lru_body(xl_ref, gl_ref, cw_ref, cb_ref, wa_ref, ba_ref, wi_ref, bi_ref, lam_ref,
              y_ref, xe_scr, h_scr, *, tl):
    step = pl.program_id(1)

    @pl.when(step == 0)
    def _():
        h_scr[...] = jnp.zeros(h_scr.shape, F32)

    xr = _causal_conv(xl_ref, cw_ref, cb_ref, xe_scr, step, tl)
    xrb = xr.astype(BF16)
    r = jax.nn.sigmoid(jnp.dot(xrb, wa_ref[...], preferred_element_type=F32) + ba_ref[...])
    i = jax.nn.sigmoid(jnp.dot(xrb, wi_ref[...], preferred_element_type=F32) + bi_ref[...])
    log_a = -LRU_C * r * _softplus(-lam_ref[...])
    a = jnp.exp(log_a)
    b = jnp.sqrt(-jnp.tanh(log_a) * (a * a + 1.0)) * (i * xr)
    rows = lax.broadcasted_iota(jnp.int32, a.shape, 0)
    s = 1
    while s < tl:
        keep = rows >= s
        a_sh = jnp.where(keep, pltpu.roll(a, s, 0), 1.0)
        b_sh = jnp.where(keep, pltpu.roll(b, s, 0), 0.0)
        b = a * b_sh + b
        a = a * a_sh
        s *= 2
    hs = a * h_scr[0:1, :] + b
    h_scr[...] = jnp.broadcast_to(hs[tl - 1:tl, :], h_scr.shape)
    y_ref[...] = hs * jax.nn.gelu(gl_ref[...])


def _block_diag(w):
    nb, bw, _ = w.shape
    out = jnp.zeros((nb * bw, nb * bw), w.dtype)
    for k in range(nb):
        out = out.at[k * bw:(k + 1) * bw, k * bw:(k + 1) * bw].set(w[k])
    return out


def _lru(xl, gl, conv_w, conv_b, wa, ba, wi, bi, lam, *, bsz, tl=256):
    t, dl = xl.shape
    nt = t // bsz // tl
    row = pl.BlockSpec((tl, dl), lambda b, i: (b * nt + i, 0))
    vec = _const_spec((1, dl))
    return pl.pallas_call(
        functools.partial(_lru_body, tl=tl),
        grid=(bsz, nt),
        in_specs=[row, row, _const_spec((CONV_W, dl)), vec, _const_spec((dl, dl)), vec,
                  _const_spec((dl, dl)), vec, vec],
        out_specs=row,
        out_shape=jax.ShapeDtypeStruct((t, dl), F32),
        scratch_shapes=[pltpu.VMEM((tl + SUBLANE, dl), F32), pltpu.VMEM((SUBLANE, dl), F32)],
        compiler_params=_cparams("parallel", "arbitrary"),
        name="rglru",
    )(xl, gl, conv_w, conv_b.reshape(1, dl), _block_diag(wa).astype(BF16), ba.reshape(1, dl),
      _block_diag(wi).astype(BF16), bi.reshape(1, dl), lam.reshape(1, dl))


def _rope(x, ct, st, axis):
    width = x.shape[axis]
    fm = lax.broadcasted_iota(jnp.int32, x.shape, axis) % ATT_HEAD_DIM
    half = ROPE_ROT // 2
    partner = jnp.where(fm < half, pltpu.roll(x, width - half, axis), pltpu.roll(x, half, axis))
    return x * ct + partner * st


def _dot_nt(a, b):
    return lax.dot_general(a, b, (((1,), (1,)), ((), ())), preferred_element_type=F32)


def _prep_body(cq_ref, ckv_ref, misc_ref, ct_ref, st_ref, ctt_ref, stt_ref, cqg_ref, ckvg_ref, wuqt_ref,
               wuk_ref, wuvt_ref, qg_ref, kg_ref, wqit_ref, kig_ref,
               k_ref, ki_ref, vt_ref, qt_ref, qit_ref, wc_ref, *, tm):
    ct = ct_ref[...]
    st = st_ref[...]
    d_att = wuk_ref.shape[1]
    gr = lax.broadcasted_iota(jnp.int32, (d_att, d_att), 0) // ATT_HEAD_DIM
    gc = lax.broadcasted_iota(jnp.int32, (d_att, d_att), 1) // ATT_HEAD_DIM
    head_mean = jnp.where(gr == gc, 1.0 / ATT_HEAD_DIM, 0.0).astype(F32)

    cqn = _rms(cq_ref[...], cqg_ref[...]).astype(BF16)
    ckvn = _rms(ckv_ref[...], ckvg_ref[...]).astype(BF16)

    qt = _dot_nt(wuqt_ref[...], cqn)
    ms = _dot_sel_lhs(head_mean, qt * qt)
    qt = qt * lax.rsqrt(ms + RMS_EPS) * qg_ref[...]
    qt_ref[...] = (_rope(qt, ctt_ref[...], stt_ref[...], 0) * ATT_HEAD_DIM ** -0.5).astype(BF16)
    vt_ref[0:d_att, :] = _dot_nt(wuvt_ref[...], ckvn).astype(BF16)
    vt_ref[d_att:, :] = jnp.ones((vt_ref.shape[0] - d_att, tm), BF16)

    k = jnp.dot(ckvn, wuk_ref[...], preferred_element_type=F32)
    ms = _dot_sel_rhs(k * k, head_mean)
    k_ref[...] = _rope(k * lax.rsqrt(ms + RMS_EPS) * kg_ref[...], ct, st, 1).astype(BF16)

    qit = _rope(_dot_nt(wqit_ref[...], cqn), ctt_ref[...], stt_ref[...], 0)
    misc = misc_ref[...]
    misc_t = misc.T
    wscale = IDX_HEADS ** -0.5 * IDX_DIM ** -0.5
    for j in range(tm // Q_BLOCK):
        cols = slice(j * Q_BLOCK, (j + 1) * Q_BLOCK)
        for h in range(IDX_HEADS):
            qit_ref[j, 0:IDX_DIM, h * Q_BLOCK:(h + 1) * Q_BLOCK] = (
                qit[h * IDX_DIM:(h + 1) * IDX_DIM, cols].astype(BF16))
        qit_ref[j, IDX_DIM:, :] = jnp.zeros((LANE - IDX_DIM, IDX_HEADS * Q_BLOCK), BF16)
        wrow = jnp.concatenate([misc_t[IDX_DIM + h:IDX_DIM + h + 1, cols] for h in range(IDX_HEADS)], axis=1)
        wc_ref[j] = jnp.broadcast_to(wrow * wscale, (SUBLANE, IDX_HEADS * Q_BLOCK))

    lane = lax.broadcasted_iota(jnp.int32, misc.shape, 1)
    km = jnp.where(lane < IDX_DIM, misc, 0.0)
    ms = jnp.sum(km * km, axis=-1, keepdims=True) * (1.0 / IDX_DIM)
    kin = km * lax.rsqrt(ms + RMS_EPS) * kig_ref[...]
    ki_ref[...] = _rope(kin, ct[:, :LANE], st[:, :LANE], 1).astype(BF16)


def _rope_lane_tables(length, width):
    half = ROPE_ROT // 2
    inv = ROPE_THETA ** (-jnp.arange(half, dtype=F32) * 2.0 / ROPE_ROT)
    ang = jnp.arange(length, dtype=F32)[:, None] * inv[None, :]
    cos, sin = jnp.cos(ang), jnp.sin(ang)
    ones = jnp.ones((length, ATT_HEAD_DIM - ROPE_ROT), F32)
    ct = jnp.concatenate([cos, cos, ones], axis=1)
    st = jnp.concatenate([-sin, sin, 0.0 * ones], axis=1)
    reps = width // ATT_HEAD_DIM
    return jnp.tile(ct, (1, reps)), jnp.tile(st, (1, reps))


def _dsa_prep(cq, ckv, misc, p, *, bsz):
    t, qr = cq.shape
    kvr = ckv.shape[1]
    tm = KEY_CHUNK
    length = t // bsz
    nt = length // tm
    d_att = ATT_HEADS * ATT_HEAD_DIM
    ct, st = _rope_lane_tables(length, d_att)
    wukv = p["w_ukv"].reshape(kvr, ATT_HEADS, 2, ATT_HEAD_DIM)
    wuk = wukv[:, :, 0, :].reshape(kvr, d_att).astype(BF16)
    wuvt = wukv[:, :, 1, :].reshape(kvr, d_att).T.astype(BF16)
    kig = jnp.concatenate([p["kidx_norm"], jnp.zeros((LANE - IDX_DIM,), F32)]).reshape(1, LANE)
    row = lambda c: pl.BlockSpec((tm, c), lambda b, i: (b * nt + i, 0))
    tab = pl.BlockSpec((tm, d_att), lambda b, i: (i, 0))
    tab_t = pl.BlockSpec((d_att, tm), lambda b, i: (0, i))
    nqb = tm // Q_BLOCK
    hq = IDX_HEADS * Q_BLOCK
    return pl.pallas_call(
        functools.partial(_prep_body, tm=tm),
        grid=(bsz, nt),
        in_specs=[row(qr), row(kvr), row(LANE), tab, tab, tab_t, tab_t, _const_spec((1, qr)),
                  _const_spec((1, kvr)), _const_spec((d_att, qr)), _const_spec((kvr, d_att)),
                  _const_spec((d_att, kvr)), _const_spec((d_att, 1)), _const_spec((1, d_att)),
                  _const_spec((IDX_HEADS * IDX_DIM, qr)), _const_spec((1, LANE))],
        out_specs=[row(d_att), row(LANE),
                   pl.BlockSpec((None, d_att + V_ONES_ROWS, tm), lambda b, i: (b * nt + i, 0, 0)),
                   pl.BlockSpec((None, d_att, tm), lambda b, i: (b * nt + i, 0, 0)),
                   pl.BlockSpec((nqb, LANE, hq), lambda b, i: (b * nt + i, 0, 0)),
                   pl.BlockSpec((nqb, SUBLANE, hq), lambda b, i: (b * nt + i, 0, 0))],
        out_shape=[jax.ShapeDtypeStruct((t, d_att), BF16),
                   jax.ShapeDtypeStruct((t, LANE), BF16),
                   jax.ShapeDtypeStruct((bsz * nt, d_att + V_ONES_ROWS, tm), BF16),
                   jax.ShapeDtypeStruct((bsz * nt, d_att, tm), BF16),
                   jax.ShapeDtypeStruct((t // Q_BLOCK, LANE, hq), BF16),
                   jax.ShapeDtypeStruct((t // Q_BLOCK, SUBLANE, hq), F32)],
        compiler_params=_cparams("parallel", "parallel"),
        name="dsa_prep",
    )(cq, ckv, misc, ct, st, ct.T, st.T, p["cq_norm"].reshape(1, qr), p["ckv_norm"].reshape(1, kvr),
      p["w_uq"].T.astype(BF16), wuk, wuvt, jnp.tile(p["q_norm"], ATT_HEADS).reshape(d_att, 1),
      jnp.tile(p["k_norm"], ATT_HEADS).reshape(1, d_att), p["w_qidx"].T.astype(BF16), kig)


INT_MIN = -2 ** 31
F32_TINY = 2.0 ** -126
F32_LOWEST = -3.4028234663852886e38
SEARCH_VALUE_STEPS = 16
SEARCH_CAP = 80


def _order_key_to_float(u):
    sk = u ^ jnp.int32(INT_MIN)
    fb = sk ^ ((sk >> 31) & jnp.int32(0x7FFFFFFF))
    return lax.bitcast_convert_type(fb, F32)


def _float_to_order_key(f):
    b = lax.bitcast_convert_type(f, jnp.int32)
    return (b ^ ((b >> 31) & jnp.int32(0x7FFFFFFF))) ^ jnp.int32(INT_MIN)


def _fold_rows(x, op):
    parts = [x[i * SUBLANE:(i + 1) * SUBLANE, :] for i in range(x.shape[0] // SUBLANE)]
    while len(parts) > 1:
        parts = [op(parts[i], parts[i + 1]) for i in range(0, len(parts), 2)]
    return parts[0]


def _dsa_body(qit_ref, wc_ref, qt_ref, ki_ref, k_ref, vt_ref, o_ref,
              sc_scr, lg_scr, qbd_scr, oacc_scr, *, n_sel):
    kc = KEY_CHUNK
    qb = pl.program_id(1)
    nk = qb // (kc // Q_BLOCK) + 1
    qpos = qb * Q_BLOCK + lax.broadcasted_iota(jnp.int32, (1, Q_BLOCK), 1)
    hq = IDX_HEADS * Q_BLOCK
    k_f = jnp.float32(n_sel)

    def chunk(c):
        return pl.ds(pl.multiple_of(c * kc, kc), kc)

    def over_chunks(body, carry):
        carry = lax.fori_loop(0, nk // 2, lambda i, cr: body(2 * i + 1, body(2 * i, cr)), carry)
        return lax.cond(nk % 2 == 1, lambda cr: body(nk - 1, cr), lambda cr: cr, carry)

    qit = qit_ref[...]
    wc = wc_ref[0:1, :]

    def score_chunk(c, carry):
        amax, c_pos, c_nn = carry
        s4 = jnp.dot(ki_ref[chunk(c), :], qit, preferred_element_type=F32)
        s4 = jnp.maximum(s4, 0.0) * wc
        s = (s4[:, 0:Q_BLOCK] + s4[:, Q_BLOCK:2 * Q_BLOCK]) + (s4[:, 2 * Q_BLOCK:3 * Q_BLOCK] + s4[:, 3 * Q_BLOCK:])
        kpos = c * kc + lax.broadcasted_iota(jnp.int32, (kc, Q_BLOCK), 0)
        sm = jnp.where(kpos <= qpos, s, -jnp.inf)
        sc_scr[chunk(c), :] = sm
        return (jnp.maximum(amax, _fold_rows(jnp.abs(s), jnp.maximum)),
                c_pos + _fold_rows(jnp.where(sm >= F32_TINY, 1.0, 0.0), jnp.add),
                c_nn + _fold_rows(jnp.where(sm >= 0.0, 1.0, 0.0), jnp.add))

    zeros8 = jnp.zeros((SUBLANE, Q_BLOCK), F32)
    amax, c_pos, c_nn = over_chunks(score_chunk, (zeros8, zeros8, zeros8))
    amax = jnp.max(amax, axis=0, keepdims=True)
    c_pos = jnp.sum(c_pos, axis=0, keepdims=True)
    c_nn = jnp.sum(c_nn, axis=0, keepdims=True)

    def count_ge(*thrs):
        def body(c, accs):
            blk = sc_scr[chunk(c), :]
            return tuple(a + _fold_rows(jnp.where(blk >= t, 1.0, 0.0), jnp.add) for a, t in zip(accs, thrs))
        accs = over_chunks(body, tuple(jnp.zeros((SUBLANE, Q_BLOCK), F32) for _ in thrs))
        return tuple(jnp.sum(a, axis=0, keepdims=True) for a in accs)

    n_valid = (qpos + 1).astype(F32)
    take_all = n_valid <= k_f
    zero_tie = (c_pos < k_f) & (c_nn >= k_f)
    pos = c_pos >= k_f
    hi_top = _order_key_to_float(_float_to_order_key(amax) + 1)
    lo = jnp.where(take_all, F32_LOWEST, jnp.where(zero_tie, 0.0, jnp.where(pos, F32_TINY, -amax)))
    hi = jnp.where(take_all, jnp.inf, jnp.where(zero_tie, F32_TINY, jnp.where(pos, hi_top, 0.0)))
    cnt_lo = jnp.where(take_all | (~zero_tie & ~pos), n_valid, jnp.where(zero_tie, c_nn, c_pos))
    cnt_hi = jnp.where(take_all | (~zero_tie & pos), 0.0, jnp.where(zero_tie, c_pos, c_nn))
    settled = take_all | zero_tie

    def search_step(st, on_values):
        lk, hk, lo, hi, cnt_lo, cnt_hi = st
        midk = lk + ((hk - lk) >> 1)
        mid = _order_key_to_float(midk)
        if on_values:
            vmid = lo + (hi - lo) * 0.5
            inside = (vmid > lo) & (vmid < hi)
            mid = jnp.where(inside, vmid, mid)
            midk = jnp.where(inside, _float_to_order_key(vmid), midk)
        cnt, = count_ge(mid)
        ge = cnt >= k_f
        return (jnp.where(ge, midk, lk), jnp.where(ge, hk, midk), jnp.where(ge, mid, lo), jnp.where(ge, hi, mid),
                jnp.where(ge, cnt, cnt_lo), jnp.where(ge, cnt_hi, cnt))

    def search_cond(carry):
        it, (lk, hk, _, _, cnt_lo, _) = carry
        open_ = jnp.logical_not(settled | (cnt_lo == k_f) | (hk - lk <= 1))
        return (it < SEARCH_CAP) & (jnp.sum(jnp.where(open_, 1.0, 0.0)) > 0.0)

    def search_pair(carry):
        it, st = carry
        return it + 1, search_step(search_step(st, False), False)

    st = (_float_to_order_key(lo), _float_to_order_key(hi), lo, hi, cnt_lo, cnt_hi)
    st = lax.fori_loop(0, SEARCH_VALUE_STEPS, lambda _, s: search_step(s, True), st)
    _, (_, _, lo, hi, cnt_lo, cnt_hi) = lax.while_loop(search_cond, search_pair, (jnp.int32(0), st))
    need = k_f - cnt_hi

    sub = Q_BLOCK
    ri = lax.broadcasted_iota(jnp.int32, (sub, sub), 0)
    ci = lax.broadcasted_iota(jnp.int32, (sub, sub), 1)
    before = (ri > ci).astype(BF16)

    qbd_scr[...] = jnp.zeros(qbd_scr.shape, BF16)
    for h in range(ATT_HEADS):
        rows = slice(h * ATT_HEAD_DIM, (h + 1) * ATT_HEAD_DIM)
        qbd_scr[rows, h * Q_BLOCK:(h + 1) * Q_BLOCK] = qt_ref[rows, :]
    qbd = qbd_scr[...]

    def logit_chunk(c, carry):
        m, seen = carry
        for j in range(kc // sub):
            rows = pl.ds(pl.multiple_of(c * kc + j * sub, sub), sub)
            blk = sc_scr[rows, :]
            in_hi = blk >= hi
            tie = (blk >= lo) & jnp.logical_not(in_hi)
            tie_f = jnp.where(tie, 1.0, 0.0)
            rank = jnp.dot(before, tie_f.astype(BF16), preferred_element_type=F32) + seen
            sel = in_hi | (tie & (rank < need))
            seen = seen + jnp.sum(tie_f, axis=0, keepdims=True)
            b = jnp.where(sel, 0.0, -jnp.inf)
            lg = jnp.dot(k_ref[rows, :], qbd, preferred_element_type=F32) + jnp.concatenate([b] * ATT_HEADS, axis=1)
            lg_scr[rows, :] = lg
            m = jnp.maximum(m, jnp.max(lg, axis=0, keepdims=True))
        return m, seen

    m, _ = over_chunks(logit_chunk, (jnp.full((1, hq), -jnp.inf, F32), jnp.zeros((1, Q_BLOCK), F32)))
    oacc_scr[...] = jnp.zeros(oacc_scr.shape, F32)

    def value_chunk(c, carry):
        acc = oacc_scr[...]
        depth = 2 * Q_BLOCK
        for j in range(kc // depth):
            rows = pl.ds(pl.multiple_of(c * kc + j * depth, depth), depth)
            p = jnp.exp(lg_scr[rows, :] - m).astype(BF16)
            acc = acc + jnp.dot(vt_ref[c, :, j * depth:(j + 1) * depth], p, preferred_element_type=F32)
        oacc_scr[...] = acc
        return carry

    over_chunks(value_chunk, 0)
    d_att = ATT_HEADS * ATT_HEAD_DIM
    outs = []
    for h in range(ATT_HEADS):
        cols = slice(h * Q_BLOCK, (h + 1) * Q_BLOCK)
        ssum = oacc_scr[d_att:d_att + 1, cols]
        outs.append(oacc_scr[h * ATT_HEAD_DIM:(h + 1) * ATT_HEAD_DIM, cols] / ssum)
    o_ref[...] = jnp.concatenate(outs, axis=0).T


def _dsa(k, ki, vt, qt, qit, wc, *, bsz):
    t, d_att = k.shape
    length = t // bsz
    nb = length // Q_BLOCK
    kc = KEY_CHUNK
    nkc = length // kc
    hq = IDX_HEADS * Q_BLOCK
    per_q = kc // Q_BLOCK
    v_rows = d_att + V_ONES_ROWS
    vt = vt.reshape(bsz, nkc, v_rows, kc)
    return pl.pallas_call(
        functools.partial(_dsa_body, n_sel=min(TOP_K, length // 4)),
        grid=(bsz, nb),
        in_specs=[pl.BlockSpec((None, LANE, hq), lambda b, i: (b * nb + i, 0, 0)),
                  pl.BlockSpec((None, SUBLANE, hq), lambda b, i: (b * nb + i, 0, 0)),
                  pl.BlockSpec((None, d_att, Q_BLOCK), lambda b, i: (b * nkc + i // per_q, 0, i % per_q)),
                  pl.BlockSpec((length, LANE), lambda b, i: (b, 0)),
                  pl.BlockSpec((length, d_att), lambda b, i: (b, 0)),
                  pl.BlockSpec((None, nkc, v_rows, kc), lambda b, i: (b, 0, 0, 0))],
        out_specs=pl.BlockSpec((Q_BLOCK, d_att), lambda b, i: (b * nb + i, 0)),
        out_shape=jax.ShapeDtypeStruct((t, d_att), F32),
        scratch_shapes=[pltpu.VMEM((length, Q_BLOCK), F32), pltpu.VMEM((length, hq), F32),
                        pltpu.VMEM((d_att, hq), BF16), pltpu.VMEM((v_rows, hq), F32)],
        compiler_params=_cparams("parallel", "arbitrary"),
        name="dsa",
    )(qit, wc, qt, ki, k, vt)


def _pad_cols(w, width):
    return jnp.concatenate([w, jnp.zeros((w.shape[0], width - w.shape[1]), w.dtype)], axis=1)


def kernel(x, norm_ffn1, ffn1_w13, ffn1_w2, norm_mix, w_in, ssd_conv_w, ssd_conv_b, ssd_dt_bias, ssd_a_log, ssd_d, ssd_norm, cq_norm, ckv_norm, w_uq, w_ukv, q_norm, k_norm, w_qidx, kidx_norm, lru_conv_w, lru_conv_b, lru_wa, lru_ba, lru_wi, lru_bi, lru_lambda, w_out, norm_ffn2, ffn2_w13, ffn2_w2):
    bsz, length, d = x.shape
    depth = w_in.shape[0]
    d_ssd = ssd_norm.shape[1]
    conv_ch = ssd_conv_b.shape[1]
    heads = ssd_d.shape[1]
    q_rank, kv_rank = cq_norm.shape[1], ckv_norm.shape[1]
    d_lru = lru_lambda.shape[1]
    sizes = [d_ssd, conv_ch, heads, q_rank, kv_rank, IDX_DIM, IDX_HEADS, d_lru, d_lru]
    offs = [0]
    for s in sizes:
        offs.append(offs[-1] + s)
    col = lambda wl, k: wl[:, offs[k]:offs[k + 1]]
    widths = (d_ssd, conv_ch, LANE, q_rank, kv_rank, LANE, d_lru, d_lru)

    xt = x.reshape(bsz * length, d)
    for l in range(depth):
        wl = w_in[l]
        w_pad = jnp.concatenate(
            [col(wl, 0), col(wl, 1), _pad_cols(col(wl, 2), LANE), col(wl, 3), col(wl, 4),
             _pad_cols(jnp.concatenate([col(wl, 5), col(wl, 6)], axis=1), LANE), col(wl, 7), col(wl, 8)],
            axis=1).astype(BF16)
        xt, z, xbc, dtp, cq, ckv, misc, xl, gl = _ffn(xt, norm_ffn1[l], ffn1_w13[l], ffn1_w2[l],
                                                      proj=(norm_mix[l], w_pad, widths))
        y_ssd = _ssd(z, xbc, dtp, ssd_conv_w[l], ssd_conv_b[l], ssd_dt_bias[l], ssd_a_log[l], ssd_d[l],
                     ssd_norm[l], bsz=bsz)
        y_lru = _lru(xl, gl, lru_conv_w[l], lru_conv_b[l], lru_wa[l], lru_ba[l], lru_wi[l], lru_bi[l],
                     lru_lambda[l], bsz=bsz)
        prm = dict(cq_norm=cq_norm[l], ckv_norm=ckv_norm[l], w_uq=w_uq[l], w_ukv=w_ukv[l], q_norm=q_norm[l],
                   k_norm=k_norm[l], w_qidx=w_qidx[l], kidx_norm=kidx_norm[l])
        k, ki, vt, qt, qit, wc = _dsa_prep(cq, ckv, misc, prm, bsz=bsz)
        y_att = _dsa(k, ki, vt, qt, qit, wc, bsz=bsz)
        xt = _ffn(xt, norm_ffn2[l], ffn2_w13[l], ffn2_w2[l],
                  mixed=([y_ssd, y_att, y_lru], w_out[l].astype(BF16)))
    return xt.reshape(bsz, length, d)
```

```python
import functools

import jax
import jax.numpy as jnp
from jax import lax
from jax.experimental import pallas as pl
from jax.experimental.pallas import tpu as pltpu

F32 = jnp.float32
BF16 = jnp.bfloat16

RMS_EPS = 1e-6
SSD_HEAD_DIM = 64
SSD_GROUPS = 2
SSD_STATE = 128
SSD_CHUNK = 128
ATT_HEADS = 4
ATT_HEAD_DIM = 64
IDX_HEADS = 4
IDX_DIM = 64
TOP_K = 256
Q_BLOCK = 128
ROPE_THETA = 500000.0
ROPE_ROT = 16
LRU_C = 8.0
CONV_W = 4
LANE = 128
SUBLANE = 8
KEY_CHUNK = 512
V_ONES_ROWS = 16
VMEM_LIMIT = 56 * 1024 * 1024


def _cparams(*sem):
    return pltpu.CompilerParams(dimension_semantics=sem, vmem_limit_bytes=VMEM_LIMIT)


def _const_spec(shape):
    nd = len(shape)
    return pl.BlockSpec(shape, lambda *_: (0,) * nd, pipeline_mode=pl.Buffered(1))


def _rms(x, g):
    ms = jnp.mean(x * x, axis=-1, keepdims=True)
    return x * lax.rsqrt(ms + RMS_EPS) * g


def _softplus(x):
    return jnp.maximum(x, 0.0) + jnp.log1p(jnp.exp(-jnp.abs(x)))


def _silu(x):
    return x * jax.nn.sigmoid(x)


def _split3(x):
    hi = x.astype(BF16)
    r = x - hi.astype(F32)
    mid = r.astype(BF16)
    return hi, mid, (r - mid.astype(F32)).astype(BF16)


def _dot_sel_rhs(x, sel):
    sel = sel.astype(BF16)
    return sum(jnp.dot(p, sel, preferred_element_type=F32) for p in _split3(x))


def _dot_sel_lhs(sel, x):
    sel = sel.astype(BF16)
    return sum(jnp.dot(sel, p, preferred_element_type=F32) for p in _split3(x))


def _ffn_body(*refs, d_ff, tf, n_mix, proj_widths):
    refs = list(refs)
    x = refs.pop(0)[...]
    if n_mix:
        y_refs = [refs.pop(0) for _ in range(n_mix)]
        wo_ref = refs.pop(0)
        off = 0
        for y_ref in y_refs:
            wd = y_ref.shape[1]
            x = x + jnp.dot(y_ref[...].astype(BF16), wo_ref[off:off + wd, :], preferred_element_type=F32)
            off += wd
    g_ref, w13_ref, w2_ref = refs.pop(0), refs.pop(0), refs.pop(0)
    if proj_widths:
        gp_ref, wi_ref = refs.pop(0), refs.pop(0)
    o_ref = refs.pop(0)

    h = _rms(x, g_ref[...]).astype(BF16)
    acc = jnp.zeros(x.shape, F32)
    for j in range(d_ff // tf):
        g = jnp.dot(h, w13_ref[:, j * tf:(j + 1) * tf], preferred_element_type=F32)
        u = jnp.dot(h, w13_ref[:, d_ff + j * tf:d_ff + (j + 1) * tf], preferred_element_type=F32)
        a = (_silu(g) * u).astype(BF16)
        acc = acc + jnp.dot(a, w2_ref[j * tf:(j + 1) * tf, :], preferred_element_type=F32)
    x = x + 0.5 * acc
    o_ref[...] = x

    if proj_widths:
        hp = _rms(x, gp_ref[...]).astype(BF16)
        off = 0
        for p_ref, wd in zip(refs, proj_widths):
            p_ref[...] = jnp.dot(hp, wi_ref[:, off:off + wd], preferred_element_type=F32)
            off += wd


def _ffn(x, g, w13, w2, *, mixed=None, proj=None, tm=512, tf=256):
    t, d = x.shape
    d_ff = w2.shape[0]
    row = lambda c: pl.BlockSpec((tm, c), lambda i: (i, 0))
    args, in_specs = [x], [row(d)]
    if mixed is not None:
        ys, w_out = mixed
        args += [*ys, w_out]
        in_specs += [row(y.shape[1]) for y in ys] + [_const_spec(w_out.shape)]
    args += [g.reshape(1, d), w13.astype(BF16), w2.astype(BF16)]
    in_specs += [_const_spec((1, d)), _const_spec((d, 2 * d_ff)), _const_spec((d_ff, d))]
    out_specs, out_shape, widths = [row(d)], [jax.ShapeDtypeStruct((t, d), F32)], ()
    if proj is not None:
        gp, w_in, widths = proj
        args += [gp.reshape(1, d), w_in]
        in_specs += [_const_spec((1, d)), _const_spec(w_in.shape)]
        out_specs += [row(wd) for wd in widths]
        out_shape += [jax.ShapeDtypeStruct((t, wd), F32) for wd in widths]
    outs = pl.pallas_call(
        functools.partial(_ffn_body, d_ff=d_ff, tf=tf, n_mix=len(mixed[0]) if mixed is not None else 0,
                          proj_widths=tuple(widths)),
        grid=(t // tm,),
        in_specs=in_specs,
        out_specs=out_specs,
        out_shape=out_shape,
        compiler_params=_cparams("parallel"),
        name="ffn",
    )(*args)
    return outs if proj is not None else outs[0]


def _causal_conv(x_ref, cw_ref, cb_ref, xe_scr, step, rows):
    @pl.when(step == 0)
    def _():
        xe_scr[0:SUBLANE, :] = jnp.zeros((SUBLANE, xe_scr.shape[1]), F32)

    @pl.when(step > 0)
    def _():
        xe_scr[0:SUBLANE, :] = xe_scr[rows:rows + SUBLANE, :]

    xe_scr[SUBLANE:SUBLANE + rows, :] = x_ref[...]
    y = cb_ref[...] + cw_ref[CONV_W - 1:CONV_W, :] * x_ref[...]
    for w in range(CONV_W - 1):
        off = SUBLANE - (CONV_W - 1) + w
        y = y + cw_ref[w:w + 1, :] * xe_scr[off:off + rows, :]
    return y


def _ssd_body(z_ref, xbc_ref, dtp_ref, cw_ref, cb_ref, dtb_ref, alog_ref, dsk_ref, ng_ref,
              y_ref, xe_scr, st_scr, *, q, d_ssd):
    step = pl.program_id(1)
    n = SSD_STATE
    gw = d_ssd // SSD_GROUPS
    rpg = gw // SSD_HEAD_DIM

    @pl.when(step == 0)
    def _():
        st_scr[...] = jnp.zeros(st_scr.shape, F32)

    xbc = _silu(_causal_conv(xbc_ref, cw_ref, cb_ref, xe_scr, step, q))
    xs = xbc[:, :d_ssd]
    bm = xbc[:, d_ssd:d_ssd + SSD_GROUPS * n]
    cm = xbc[:, d_ssd + SSD_GROUPS * n:]

    dt = _softplus(dtp_ref[...] + dtb_ref[...])
    adt = dt * (-jnp.exp(alog_ref[...]))
    ri = lax.broadcasted_iota(jnp.int32, (q, q), 0)
    ci = lax.broadcasted_iota(jnp.int32, (q, q), 1)
    causal = ri >= ci
    acum = _dot_sel_lhs(causal.astype(F32), adt)
    er = lax.broadcasted_iota(jnp.int32, (LANE, d_ssd), 0)
    ec = lax.broadcasted_iota(jnp.int32, (LANE, d_ssd), 1)
    expand = (ec // SSD_HEAD_DIM == er).astype(F32)
    a_x = _dot_sel_rhs(acum, expand)
    dt_x = _dot_sel_rhs(dt, expand)
    acum_t = acum.T
    a_last = a_x[q - 1:q, :]
    exp_a = jnp.exp(a_x)
    xdt = xs * dt_x
    xsw = xdt * jnp.exp(a_last - a_x)
    cdec = jnp.exp(a_last)

    ys = []
    for g in range(SSD_GROUPS):
        bm_g = bm[:, g * n:(g + 1) * n]
        cm_g = cm[:, g * n:(g + 1) * n].astype(BF16)
        cb = lax.dot_general(cm_g, bm_g.astype(BF16), (((1,), (1,)), ((), ())), preferred_element_type=F32)
        yd = []
        for r in range(rpg):
            h = g * rpg + r
            seg = acum[:, h:h + 1] - acum_t[h:h + 1, :]
            decay = jnp.exp(jnp.where(causal, seg, -jnp.inf))
            m = (cb * decay).astype(BF16)
            xh = xdt[:, h * SSD_HEAD_DIM:(h + 1) * SSD_HEAD_DIM].astype(BF16)
            yd.append(jnp.dot(m, xh, preferred_element_type=F32))
        yd = jnp.concatenate(yd, axis=1)
        gs = slice(g * gw, (g + 1) * gw)
        prev = st_scr[g]
        yoff = jnp.dot(cm_g, prev.astype(BF16), preferred_element_type=F32) * exp_a[:, gs]
        st = jnp.dot(bm_g.T.astype(BF16), xsw[:, gs].astype(BF16), preferred_element_type=F32)
        st_scr[g] = prev * cdec[:, gs] + st
        ys.append(yd + yoff + dsk_ref[:, gs] * xs[:, gs])

    z = z_ref[...]
    for g in range(SSD_GROUPS):
        gs = slice(g * gw, (g + 1) * gw)
        y_ref[:, gs] = _rms(ys[g] * _silu(z[:, gs]), ng_ref[:, gs])


def _ssd(z, xbc, dtp, conv_w, conv_b, dt_bias, a_log, d_skip, norm_g, *, bsz):
    t, d_ssd = z.shape
    cch = xbc.shape[1]
    q = SSD_CHUNK
    nc = t // bsz // q
    heads = d_ssd // SSD_HEAD_DIM
    pad = lambda v, fill: jnp.concatenate([v, jnp.full((LANE - heads,), fill, F32)]).reshape(1, LANE)
    row = lambda c: pl.BlockSpec((q, c), lambda b, i: (b * nc + i, 0))
    return pl.pallas_call(
        functools.partial(_ssd_body, q=q, d_ssd=d_ssd),
        grid=(bsz, nc),
        in_specs=[row(d_ssd), row(cch), row(LANE), _const_spec((CONV_W, cch)), _const_spec((1, cch)),
                  _const_spec((1, LANE)), _const_spec((1, LANE)), _const_spec((1, d_ssd)),
                  _const_spec((1, d_ssd))],
        out_specs=row(d_ssd),
        out_shape=jax.ShapeDtypeStruct((t, d_ssd), F32),
        scratch_shapes=[pltpu.VMEM((q + SUBLANE, cch), F32),
                        pltpu.VMEM((SSD_GROUPS, SSD_STATE, d_ssd // SSD_GROUPS), F32)],
        compiler_params=_cparams("parallel", "arbitrary"),
        name="ssd",
    )(z, xbc, dtp, conv_w, conv_b.reshape(1, cch), pad(dt_bias, 0.0), pad(a_log, 0.0),
      jnp.repeat(d_skip, SSD_HEAD_DIM).reshape(1, d_ssd), norm_g.reshape(1, d_ssd))


def _lru_body(xl_ref, gl_ref, cw_ref, cb_ref, wa_ref, ba_ref, wi_ref, bi_ref, lam_ref,
              y_ref, xe_scr, h_scr, *, tl):
    step = pl.program_id(1)

    @pl.when(step == 0)
    def _():
        h_scr[...] = jnp.zeros(h_scr.shape, F32)

    xr = _causal_conv(xl_ref, cw_ref, cb_ref, xe_scr, step, tl)
    xrb = xr.astype(BF16)
    r = jax.nn.sigmoid(jnp.dot(xrb, wa_ref[...], preferred_element_type=F32) + ba_ref[...])
    i = jax.nn.sigmoid(jnp.dot(xrb, wi_ref[...], preferred_element_type=F32) + bi_ref[...])
    log_a = -LRU_C * r * _softplus(-lam_ref[...])
    a = jnp.exp(log_a)
    b = jnp.sqrt(-jnp.tanh(log_a) * (a * a + 1.0)) * (i * xr)
    rows = lax.broadcasted_iota(jnp.int32, a.shape, 0)
    s = 1
    while s < tl:
        keep = rows >= s
        a_sh = jnp.where(keep, pltpu.roll(a, s, 0), 1.0)
        b_sh = jnp.where(keep, pltpu.roll(b, s, 0), 0.0)
        b = a * b_sh + b
        a = a * a_sh
        s *= 2
    hs = a * h_scr[0:1, :] + b
    h_scr[...] = jnp.broadcast_to(hs[tl - 1:tl, :], h_scr.shape)
    y_ref[...] = hs * jax.nn.gelu(gl_ref[...])


def _block_diag(w):
    nb, bw, _ = w.shape
    out = jnp.zeros((nb * bw, nb * bw), w.dtype)
    for k in range(nb):
        out = out.at[k * bw:(k + 1) * bw, k * bw:(k + 1) * bw].set(w[k])
    return out


def _lru(xl, gl, conv_w, conv_b, wa, ba, wi, bi, lam, *, bsz, tl=256):
    t, dl = xl.shape
    nt = t // bsz // tl
    row = pl.BlockSpec((tl, dl), lambda b, i: (b * nt + i, 0))
    vec = _const_spec((1, dl))
    return pl.pallas_call(
        functools.partial(_lru_body, tl=tl),
        grid=(bsz, nt),
        in_specs=[row, row, _const_spec((CONV_W, dl)), vec, _const_spec((dl, dl)), vec,
                  _const_spec((dl, dl)), vec, vec],
        out_specs=row,
        out_shape=jax.ShapeDtypeStruct((t, dl), F32),
        scratch_shapes=[pltpu.VMEM((tl + SUBLANE, dl), F32), pltpu.VMEM((SUBLANE, dl), F32)],
        compiler_params=_cparams("parallel", "arbitrary"),
        name="rglru",
    )(xl, gl, conv_w, conv_b.reshape(1, dl), _block_diag(wa).astype(BF16), ba.reshape(1, dl),
      _block_diag(wi).astype(BF16), bi.reshape(1, dl), lam.reshape(1, dl))


def _rope(x, ct, st, axis):
    width = x.shape[axis]
    fm = lax.broadcasted_iota(jnp.int32, x.shape, axis) % ATT_HEAD_DIM
    half = ROPE_ROT // 2
    partner = jnp.where(fm < half, pltpu.roll(x, width - half, axis), pltpu.roll(x, half, axis))
    return x * ct + partner * st


def _dot_nt(a, b):
    return lax.dot_general(a, b, (((1,), (1,)), ((), ())), preferred_element_type=F32)


def _prep_body(cq_ref, ckv_ref, misc_ref, ct_ref, st_ref, ctt_ref, stt_ref, cqg_ref, ckvg_ref, wuqt_ref,
               wuk_ref, wuvt_ref, qg_ref, kg_ref, wqit_ref, kig_ref,
               k_ref, ki_ref, vt_ref, qt_ref, qit_ref, wc_ref, *, tm):
    ct = ct_ref[...]
    st = st_ref[...]
    d_att = wuk_ref.shape[1]
    gr = lax.broadcasted_iota(jnp.int32, (d_att, d_att), 0) // ATT_HEAD_DIM
    gc = lax.broadcasted_iota(jnp.int32, (d_att, d_att), 1) // ATT_HEAD_DIM
    head_mean = jnp.where(gr == gc, 1.0 / ATT_HEAD_DIM, 0.0).astype(F32)

    cqn = _rms(cq_ref[...], cqg_ref[...]).astype(BF16)
    ckvn = _rms(ckv_ref[...], ckvg_ref[...]).astype(BF16)

    qt = _dot_nt(wuqt_ref[...], cqn)
    ms = _dot_sel_lhs(head_mean, qt * qt)
    qt = qt * lax.rsqrt(ms + RMS_EPS) * qg_ref[...]
    qt_ref[...] = (_rope(qt, ctt_ref[...], stt_ref[...], 0) * ATT_HEAD_DIM ** -0.5).astype(BF16)
    vt_ref[0:d_att, :] = _dot_nt(wuvt_ref[...], ckvn).astype(BF16)
    vt_ref[d_att:, :] = jnp.ones((vt_ref.shape[0] - d_att, tm), BF16)

    k = jnp.dot(ckvn, wuk_ref[...], preferred_element_type=F32)
    ms = _dot_sel_rhs(k * k, head_mean)
    k_ref[...] = _rope(k * lax.rsqrt(ms + RMS_EPS) * kg_ref[...], ct, st, 1).astype(BF16)

    qit = _rope(_dot_nt(wqit_ref[...], cqn), ctt_ref[...], stt_ref[...], 0)
    misc = misc_ref[...]
    misc_t = misc.T
    wscale = IDX_HEADS ** -0.5 * IDX_DIM ** -0.5
    for j in range(tm // Q_BLOCK):
        cols = slice(j * Q_BLOCK, (j + 1) * Q_BLOCK)
        for h in range(IDX_HEADS):
            qit_ref[j, 0:IDX_DIM, h * Q_BLOCK:(h + 1) * Q_BLOCK] = (
                qit[h * IDX_DIM:(h + 1) * IDX_DIM, cols].astype(BF16))
        qit_ref[j, IDX_DIM:, :] = jnp.zeros((LANE - IDX_DIM, IDX_HEADS * Q_BLOCK), BF16)
        wrow = jnp.concatenate([misc_t[IDX_DIM + h:IDX_DIM + h + 1, cols] for h in range(IDX_HEADS)], axis=1)
        wc_ref[j] = jnp.broadcast_to(wrow * wscale, (SUBLANE, IDX_HEADS * Q_BLOCK))

    lane = lax.broadcasted_iota(jnp.int32, misc.shape, 1)
    km = jnp.where(lane < IDX_DIM, misc, 0.0)
    ms = jnp.sum(km * km, axis=-1, keepdims=True) * (1.0 / IDX_DIM)
    kin = km * lax.rsqrt(ms + RMS_EPS) * kig_ref[...]
    ki_ref[...] = _rope(kin, ct[:, :LANE], st[:, :LANE], 1).astype(BF16)


def _rope_lane_tables(length, width):
    half = ROPE_ROT // 2
    inv = ROPE_THETA ** (-jnp.arange(half, dtype=F32) * 2.0 / ROPE_ROT)
    ang = jnp.arange(length, dtype=F32)[:, None] * inv[None, :]
    cos, sin = jnp.cos(ang), jnp.sin(ang)
    ones = jnp.ones((length, ATT_HEAD_DIM - ROPE_ROT), F32)
    ct = jnp.concatenate([cos, cos, ones], axis=1)
    st = jnp.concatenate([-sin, sin, 0.0 * ones], axis=1)
    reps = width // ATT_HEAD_DIM
    return jnp.tile(ct, (1, reps)), jnp.tile(st, (1, reps))


def _dsa_prep(cq, ckv, misc, p, *, bsz):
    t, qr = cq.shape
    kvr = ckv.shape[1]
    tm = KEY_CHUNK
    length = t // bsz
    nt = length // tm
    d_att = ATT_HEADS * ATT_HEAD_DIM
    ct, st = _rope_lane_tables(length, d_att)
    wukv = p["w_ukv"].reshape(kvr, ATT_HEADS, 2, ATT_HEAD_DIM)
    wuk = wukv[:, :, 0, :].reshape(kvr, d_att).astype(BF16)
    wuvt = wukv[:, :, 1, :].reshape(kvr, d_att).T.astype(BF16)
    kig = jnp.concatenate([p["kidx_norm"], jnp.zeros((LANE - IDX_DIM,), F32)]).reshape(1, LANE)
    row = lambda c: pl.BlockSpec((tm, c), lambda b, i: (b * nt + i, 0))
    tab = pl.BlockSpec((tm, d_att), lambda b, i: (i, 0))
    tab_t = pl.BlockSpec((d_att, tm), lambda b, i: (0, i))
    nqb = tm // Q_BLOCK
    hq = IDX_HEADS * Q_BLOCK
    return pl.pallas_call(
        functools.partial(_prep_body, tm=tm),
        grid=(bsz, nt),
        in_specs=[row(qr), row(kvr), row(LANE), tab, tab, tab_t, tab_t, _const_spec((1, qr)),
                  _const_spec((1, kvr)), _const_spec((d_att, qr)), _const_spec((kvr, d_att)),
                  _const_spec((d_att, kvr)), _const_spec((d_att, 1)), _const_spec((1, d_att)),
                  _const_spec((IDX_HEADS * IDX_DIM, qr)), _const_spec((1, LANE))],
        out_specs=[row(d_att), row(LANE),
                   pl.BlockSpec((None, d_att + V_ONES_ROWS, tm), lambda b, i: (b * nt + i, 0, 0)),
                   pl.BlockSpec((None, d_att, tm), lambda b, i: (b * nt + i, 0, 0)),
                   pl.BlockSpec((nqb, LANE, hq), lambda b, i: (b * nt + i, 0, 0)),
                   pl.BlockSpec((nqb, SUBLANE, hq), lambda b, i: (b * nt + i, 0, 0))],
        out_shape=[jax.ShapeDtypeStruct((t, d_att), BF16),
                   jax.ShapeDtypeStruct((t, LANE), BF16),
                   jax.ShapeDtypeStruct((bsz * nt, d_att + V_ONES_ROWS, tm), BF16),
                   jax.ShapeDtypeStruct((bsz * nt, d_att, tm), BF16),
                   jax.ShapeDtypeStruct((t // Q_BLOCK, LANE, hq), BF16),
                   jax.ShapeDtypeStruct((t // Q_BLOCK, SUBLANE, hq), F32)],
        compiler_params=_cparams("parallel", "parallel"),
        name="dsa_prep",
    )(cq, ckv, misc, ct, st, ct.T, st.T, p["cq_norm"].reshape(1, qr), p["ckv_norm"].reshape(1, kvr),
      p["w_uq"].T.astype(BF16), wuk, wuvt, jnp.tile(p["q_norm"], ATT_HEADS).reshape(d_att, 1),
      jnp.tile(p["k_norm"], ATT_HEADS).reshape(1, d_att), p["w_qidx"].T.astype(BF16), kig)


INT_MIN = -2 ** 31
F32_TINY = 2.0 ** -126
F32_LOWEST = -3.4028234663852886e38
SEARCH_VALUE_STEPS = 16
SEARCH_CAP = 80


def _order_key_to_float(u):
    sk = u ^ jnp.int32(INT_MIN)
    fb = sk ^ ((sk >> 31) & jnp.int32(0x7FFFFFFF))
    return lax.bitcast_convert_type(fb, F32)


def _float_to_order_key(f):
    b = lax.bitcast_convert_type(f, jnp.int32)
    return (b ^ ((b >> 31) & jnp.int32(0x7FFFFFFF))) ^ jnp.int32(INT_MIN)


def _fold_rows(x, op):
    parts = [x[i * SUBLANE:(i + 1) * SUBLANE, :] for i in range(x.shape[0] // SUBLANE)]
    while len(parts) > 1:
        parts = [op(parts[i], parts[i + 1]) for i in range(0, len(parts), 2)]
    return parts[0]


def _dsa_body(qit_ref, wc_ref, qt_ref, ki_ref, k_ref, vt_ref, o_ref,
              sc_scr, lg_scr, qbd_scr, oacc_scr, *, n_sel):
    kc = KEY_CHUNK
    qb = pl.program_id(1)
    nk = qb // (kc // Q_BLOCK) + 1
    qpos = qb * Q_BLOCK + lax.broadcasted_iota(jnp.int32, (1, Q_BLOCK), 1)
    hq = IDX_HEADS * Q_BLOCK
    k_f = jnp.float32(n_sel)

    def chunk(c):
        return pl.ds(pl.multiple_of(c * kc, kc), kc)

    def over_chunks(body, carry):
        def run(first, count, cr):
            for j in range(count):
                cr = body(first + j, cr)
            return cr
        carry = lax.fori_loop(0, nk // 4, lambda i, cr: run(4 * i, 4, cr), carry)
        done = (nk // 4) * 4
        carry = lax.cond(nk % 4 >= 2, lambda cr: run(done, 2, cr), lambda cr: cr, carry)
        return lax.cond(nk % 2 == 1, lambda cr: run(nk - 1, 1, cr), lambda cr: cr, carry)

    qit = qit_ref[...]
    wc = wc_ref[0:1, :]

    def score_chunk(c, carry):
        amax, c_pos, c_nn = carry
        s4 = jnp.dot(ki_ref[chunk(c), :], qit, preferred_element_type=F32)
        s4 = jnp.maximum(s4, 0.0) * wc
        s = (s4[:, 0:Q_BLOCK] + s4[:, Q_BLOCK:2 * Q_BLOCK]) + (s4[:, 2 * Q_BLOCK:3 * Q_BLOCK] + s4[:, 3 * Q_BLOCK:])
        kpos = c * kc + lax.broadcasted_iota(jnp.int32, (kc, Q_BLOCK), 0)
        sm = jnp.where(kpos <= qpos, s, -jnp.inf)
        sc_scr[chunk(c), :] = sm
        return (jnp.maximum(amax, _fold_rows(jnp.abs(s), jnp.maximum)),
                c_pos + _fold_rows(jnp.where(sm >= F32_TINY, 1.0, 0.0), jnp.add),
                c_nn + _fold_rows(jnp.where(sm >= 0.0, 1.0, 0.0), jnp.add))

    zeros8 = jnp.zeros((SUBLANE, Q_BLOCK), F32)
    amax, c_pos, c_nn = over_chunks(score_chunk, (zeros8, zeros8, zeros8))
    amax = jnp.max(amax, axis=0, keepdims=True)
    c_pos = jnp.sum(c_pos, axis=0, keepdims=True)
    c_nn = jnp.sum(c_nn, axis=0, keepdims=True)

    def count_ge(*thrs):
        def body(c, accs):
            blk = sc_scr[chunk(c), :]
            return tuple(a + _fold_rows(jnp.where(blk >= t, 1.0, 0.0), jnp.add) for a, t in zip(accs, thrs))
        accs = over_chunks(body, tuple(jnp.zeros((SUBLANE, Q_BLOCK), F32) for _ in thrs))
        return tuple(jnp.sum(a, axis=0, keepdims=True) for a in accs)

    n_valid = (qpos + 1).astype(F32)
    take_all = n_valid <= k_f
    zero_tie = (c_pos < k_f) & (c_nn >= k_f)
    pos = c_pos >= k_f
    hi_top = _order_key_to_float(_float_to_order_key(amax) + 1)
    lo = jnp.where(take_all, F32_LOWEST, jnp.where(zero_tie, 0.0, jnp.where(pos, F32_TINY, -amax)))
    hi = jnp.where(take_all, jnp.inf, jnp.where(zero_tie, F32_TINY, jnp.where(pos, hi_top, 0.0)))
    cnt_lo = jnp.where(take_all | (~zero_tie & ~pos), n_valid, jnp.where(zero_tie, c_nn, c_pos))
    cnt_hi = jnp.where(take_all | (~zero_tie & pos), 0.0, jnp.where(zero_tie, c_pos, c_nn))
    settled = take_all | zero_tie

    def search_step(st, on_values):
        lk, hk, lo, hi, cnt_lo, cnt_hi = st
        midk = lk + ((hk - lk) >> 1)
        mid = _order_key_to_float(midk)
        if on_values:
            vmid = lo + (hi - lo) * 0.5
            inside = (vmid > lo) & (vmid < hi)
            mid = jnp.where(inside, vmid, mid)
            midk = jnp.where(inside, _float_to_order_key(vmid), midk)
        cnt, = count_ge(mid)
        ge = cnt >= k_f
        return (jnp.where(ge, midk, lk), jnp.where(ge, hk, midk), jnp.where(ge, mid, lo), jnp.where(ge, hi, mid),
                jnp.where(ge, cnt, cnt_lo), jnp.where(ge, cnt_hi, cnt))

    def search_cond(carry):
        it, (lk, hk, _, _, cnt_lo, _) = carry
        open_ = jnp.logical_not(settled | (cnt_lo == k_f) | (hk - lk <= 1))
        return (it < SEARCH_CAP) & (jnp.sum(jnp.where(open_, 1.0, 0.0)) > 0.0)

    def search_pair(carry):
        it, st = carry
        return it + 1, search_step(search_step(st, False), False)

    st = (_float_to_order_key(lo), _float_to_order_key(hi), lo, hi, cnt_lo, cnt_hi)
    st = lax.fori_loop(0, SEARCH_VALUE_STEPS, lambda _, s: search_step(s, True), st)
    _, (_, _, lo, hi, cnt_lo, cnt_hi) = lax.while_loop(search_cond, search_pair, (jnp.int32(0), st))
    need = k_f - cnt_hi

    sub = Q_BLOCK
    ri = lax.broadcasted_iota(jnp.int32, (sub, sub), 0)
    ci = lax.broadcasted_iota(jnp.int32, (sub, sub), 1)
    before = (ri > ci).astype(BF16)

    qbd_scr[...] = jnp.zeros(qbd_scr.shape, BF16)
    for h in range(ATT_HEADS):
        rows = slice(h * ATT_HEAD_DIM, (h + 1) * ATT_HEAD_DIM)
        qbd_scr[rows, h * Q_BLOCK:(h + 1) * Q_BLOCK] = qt_ref[rows, :]
    qbd = qbd_scr[...]

    def logit_chunk(c, carry):
        m, seen = carry
        for j in range(kc // sub):
            rows = pl.ds(pl.multiple_of(c * kc + j * sub, sub), sub)
            blk = sc_scr[rows, :]
            in_hi = blk >= hi
            tie = (blk >= lo) & jnp.logical_not(in_hi)
            tie_f = jnp.where(tie, 1.0, 0.0)
            rank = jnp.dot(before, tie_f.astype(BF16), preferred_element_type=F32) + seen
            sel = in_hi | (tie & (rank < need))
            seen = seen + jnp.sum(tie_f, axis=0, keepdims=True)
            b = jnp.where(sel, 0.0, -jnp.inf)
            lg = jnp.dot(k_ref[rows, :], qbd, preferred_element_type=F32) + jnp.concatenate([b] * ATT_HEADS, axis=1)
            lg_scr[rows, :] = lg
            m = jnp.maximum(m, jnp.max(lg, axis=0, keepdims=True))
        return m, seen

    m, _ = over_chunks(logit_chunk, (jnp.full((1, hq), -jnp.inf, F32), jnp.zeros((1, Q_BLOCK), F32)))
    oacc_scr[...] = jnp.zeros(oacc_scr.shape, F32)

    def value_chunk(c, carry):
        acc = oacc_scr[...]
        depth = 2 * Q_BLOCK
        for j in range(kc // depth):
            rows = pl.ds(pl.multiple_of(c * kc + j * depth, depth), depth)
            p = jnp.exp(lg_scr[rows, :] - m).astype(BF16)
            acc = acc + jnp.dot(vt_ref[c, :, j * depth:(j + 1) * depth], p, preferred_element_type=F32)
        oacc_scr[...] = acc
        return carry

    over_chunks(value_chunk, 0)
    d_att = ATT_HEADS * ATT_HEAD_DIM
    outs = []
    for h in range(ATT_HEADS):
        cols = slice(h * Q_BLOCK, (h + 1) * Q_BLOCK)
        ssum = oacc_scr[d_att:d_att + 1, cols]
        outs.append(oacc_scr[h * ATT_HEAD_DIM:(h + 1) * ATT_HEAD_DIM, cols] / ssum)
    o_ref[...] = jnp.concatenate(outs, axis=0).T


def _dsa(k, ki, vt, qt, qit, wc, *, bsz):
    t, d_att = k.shape
    length = t // bsz
    nb = length // Q_BLOCK
    kc = KEY_CHUNK
    nkc = length // kc
    hq = IDX_HEADS * Q_BLOCK
    per_q = kc // Q_BLOCK
    v_rows = d_att + V_ONES_ROWS
    vt = vt.reshape(bsz, nkc, v_rows, kc)
    return pl.pallas_call(
        functools.partial(_dsa_body, n_sel=min(TOP_K, length // 4)),
        grid=(bsz, nb),
        in_specs=[pl.BlockSpec((None, LANE, hq), lambda b, i: (b * nb + i, 0, 0)),
                  pl.BlockSpec((None, SUBLANE, hq), lambda b, i: (b * nb + i, 0, 0)),
                  pl.BlockSpec((None, d_att, Q_BLOCK), lambda b, i: (b * nkc + i // per_q, 0, i % per_q)),
                  pl.BlockSpec((length, LANE), lambda b, i: (b, 0)),
                  pl.BlockSpec((length, d_att), lambda b, i: (b, 0)),
                  pl.BlockSpec((None, nkc, v_rows, kc), lambda b, i: (b, 0, 0, 0))],
        out_specs=pl.BlockSpec((Q_BLOCK, d_att), lambda b, i: (b * nb + i, 0)),
        out_shape=jax.ShapeDtypeStruct((t, d_att), F32),
        scratch_shapes=[pltpu.VMEM((length, Q_BLOCK), F32), pltpu.VMEM((length, hq), F32),
                        pltpu.VMEM((d_att, hq), BF16), pltpu.VMEM((v_rows, hq), F32)],
        compiler_params=_cparams("parallel", "arbitrary"),
        name="dsa",
    )(qit, wc, qt, ki, k, vt)


def _pad_cols(w, width):
    return jnp.concatenate([w, jnp.zeros((w.shape[0], width - w.shape[1]), w.dtype)], axis=1)


def kernel(x, norm_ffn1, ffn1_w13, ffn1_w2, norm_mix, w_in, ssd_conv_w, ssd_conv_b, ssd_dt_bias, ssd_a_log, ssd_d, ssd_norm, cq_norm, ckv_norm, w_uq, w_ukv, q_norm, k_norm, w_qidx, kidx_norm, lru_conv_w, lru_conv_b, lru_wa, lru_ba, lru_wi, lru_bi, lru_lambda, w_out, norm_ffn2, ffn2_w13, ffn2_w2):
    bsz, length, d = x.shape
    depth = w_in.shape[0]
    d_ssd = ssd_norm.shape[1]
    conv_ch = ssd_conv_b.shape[1]
    heads = ssd_d.shape[1]
    q_rank, kv_rank = cq_norm.shape[1], ckv_norm.shape[1]
    d_lru = lru_lambda.shape[1]
    sizes = [d_ssd, conv_ch, heads, q_rank, kv_rank, IDX_DIM, IDX_HEADS, d_lru, d_lru]
    offs = [0]
    for s in sizes:
        offs.append(offs[-1] + s)
    col = lambda wl, k: wl[:, offs[k]:offs[k + 1]]
    widths = (d_ssd, conv_ch, LANE, q_rank, kv_rank, LANE, d_lru, d_lru)

    xt = x.reshape(bsz * length, d)
    for l in range(depth):
        wl = w_in[l]
        w_pad = jnp.concatenate(
            [col(wl, 0), col(wl, 1), _pad_cols(col(wl, 2), LANE), col(wl, 3), col(wl, 4),
             _pad_cols(jnp.concatenate([col(wl, 5), col(wl, 6)], axis=1), LANE), col(wl, 7), col(wl, 8)],
            axis=1).astype(BF16)
        xt, z, xbc, dtp, cq, ckv, misc, xl, gl = _ffn(xt, norm_ffn1[l], ffn1_w13[l], ffn1_w2[l],
                                                      proj=(norm_mix[l], w_pad, widths))
        y_ssd = _ssd(z, xbc, dtp, ssd_conv_w[l], ssd_conv_b[l], ssd_dt_bias[l], ssd_a_log[l], ssd_d[l],
                     ssd_norm[l], bsz=bsz)
        y_lru = _lru(xl, gl, lru_conv_w[l], lru_conv_b[l], lru_wa[l], lru_ba[l], lru_wi[l], lru_bi[l],
                     lru_lambda[l], bsz=bsz)
        prm = dict(cq_norm=cq_norm[l], ckv_norm=ckv_norm[l], w_uq=w_uq[l], w_ukv=w_ukv[l], q_norm=q_norm[l],
                   k_norm=k_norm[l], w_qidx=w_qidx[l], kidx_norm=kidx_norm[l])
        k, ki, vt, qt, qit, wc = _dsa_prep(cq, ckv, misc, prm, bsz=bsz)
        y_att = _dsa(k, ki, vt, qt, qit, wc, bsz=bsz)
        xt = _ffn(xt, norm_ffn2[l], ffn2_w13[l], ffn2_w2[l],
                  mixed=([y_ssd, y_att, y_lru], w_out[l].astype(BF16)))
    return xt.reshape(bsz, length, d)
```

```python
import functools

import jax
import jax.numpy as jnp
from jax import lax
from jax.experimental import pallas as pl
from jax.experimental.pallas import tpu as pltpu

F32 = jnp.float32
BF16 = jnp.bfloat16

RMS_EPS = 1e-6
LOG2_E = 1.4426950408889634
SSD_HEAD_DIM = 64
SSD_GROUPS = 2
SSD_STATE = 128
SSD_CHUNK = 128
ATT_HEADS = 4
ATT_HEAD_DIM = 64
IDX_HEADS = 4
IDX_DIM = 64
TOP_K = 256
Q_BLOCK = 128
ROPE_THETA = 500000.0
ROPE_ROT = 16
LRU_C = 8.0
CONV_W = 4
LANE = 128
SUBLANE = 8
KEY_CHUNK = 512
V_ONES_ROWS = 16
VMEM_LIMIT = 56 * 1024 * 1024


def _cparams(*sem):
    return pltpu.CompilerParams(dimension_semantics=sem, vmem_limit_bytes=VMEM_LIMIT)


def _const_spec(shape):
    nd = len(shape)
    return pl.BlockSpec(shape, lambda *_: (0,) * nd, pipeline_mode=pl.Buffered(1))


def _rms(x, g):
    ms = jnp.mean(x * x, axis=-1, keepdims=True)
    return x * lax.rsqrt(ms + RMS_EPS) * g


def _softplus(x):
    return jnp.maximum(x, 0.0) + jnp.log1p(jnp.exp(-jnp.abs(x)))


def _silu(x):
    return x * jax.nn.sigmoid(x)


def _split3(x):
    hi = x.astype(BF16)
    r = x - hi.astype(F32)
    mid = r.astype(BF16)
    return hi, mid, (r - mid.astype(F32)).astype(BF16)


def _dot_sel_rhs(x, sel):
    sel = sel.astype(BF16)
    return sum(jnp.dot(p, sel, preferred_element_type=F32) for p in _split3(x))


def _dot_sel_lhs(sel, x):
    sel = sel.astype(BF16)
    return sum(jnp.dot(sel, p, preferred_element_type=F32) for p in _split3(x))


def _ffn_body(*refs, d_ff, tf, n_mix, proj_widths):
    refs = list(refs)
    x = refs.pop(0)[...]
    if n_mix:
        y_refs = [refs.pop(0) for _ in range(n_mix)]
        wo_ref = refs.pop(0)
        off = 0
        for y_ref in y_refs:
            wd = y_ref.shape[1]
            x = x + jnp.dot(y_ref[...].astype(BF16), wo_ref[off:off + wd, :], preferred_element_type=F32)
            off += wd
    g_ref, w13_ref, w2_ref = refs.pop(0), refs.pop(0), refs.pop(0)
    if proj_widths:
        gp_ref, wi_ref = refs.pop(0), refs.pop(0)
    o_ref = refs.pop(0)

    h = _rms(x, g_ref[...]).astype(BF16)
    acc = jnp.zeros(x.shape, F32)
    for j in range(d_ff // tf):
        g = jnp.dot(h, w13_ref[:, j * tf:(j + 1) * tf], preferred_element_type=F32)
        u = jnp.dot(h, w13_ref[:, d_ff + j * tf:d_ff + (j + 1) * tf], preferred_element_type=F32)
        a = (_silu(g) * u).astype(BF16)
        acc = acc + jnp.dot(a, w2_ref[j * tf:(j + 1) * tf, :], preferred_element_type=F32)
    x = x + 0.5 * acc
    o_ref[...] = x

    if proj_widths:
        hp = _rms(x, gp_ref[...]).astype(BF16)
        off = 0
        for p_ref, wd in zip(refs, proj_widths):
            p_ref[...] = jnp.dot(hp, wi_ref[:, off:off + wd], preferred_element_type=F32)
            off += wd


def _ffn(x, g, w13, w2, layer, *, mixed=None, proj=None, tm=512, tf=256):
    t, d = x.shape
    d_ff = w2.shape[1]
    row = lambda c: pl.BlockSpec((tm, c), lambda i: (i, 0))
    layer_spec = lambda r, c: pl.BlockSpec((None, r, c), lambda i: (layer, 0, 0), pipeline_mode=pl.Buffered(1))
    args, in_specs = [x], [row(d)]
    if mixed is not None:
        ys, w_out = mixed
        args += [*ys, w_out]
        in_specs += [row(y.shape[1]) for y in ys] + [_const_spec(w_out.shape)]
    args += [g.reshape(1, d), w13, w2]
    in_specs += [_const_spec((1, d)), layer_spec(d, 2 * d_ff), layer_spec(d_ff, d)]
    out_specs, out_shape, widths = [row(d)], [jax.ShapeDtypeStruct((t, d), F32)], ()
    if proj is not None:
        gp, w_in, widths = proj
        args += [gp.reshape(1, d), w_in]
        in_specs += [_const_spec((1, d)), _const_spec(w_in.shape)]
        out_specs += [row(wd) for wd in widths]
        out_shape += [jax.ShapeDtypeStruct((t, wd), F32) for wd in widths]
    outs = pl.pallas_call(
        functools.partial(_ffn_body, d_ff=d_ff, tf=tf, n_mix=len(mixed[0]) if mixed is not None else 0,
                          proj_widths=tuple(widths)),
        grid=(t // tm,),
        in_specs=in_specs,
        out_specs=out_specs,
        out_shape=out_shape,
        compiler_params=_cparams("parallel"),
        name="ffn",
    )(*args)
    return outs if proj is not None else outs[0]


def _causal_conv(x_ref, cw_ref, cb_ref, tail_scr, step, rows):
    @pl.when(step == 0)
    def _():
        tail_scr[...] = jnp.zeros(tail_scr.shape, F32)

    x = x_ref[...]
    tail = tail_scr[...]
    row = lax.broadcasted_iota(jnp.int32, tail.shape, 0)
    y = cb_ref[...] + cw_ref[CONV_W - 1:CONV_W, :] * x
    for w in range(CONV_W - 1):
        back = CONV_W - 1 - w
        xr = pltpu.roll(x, back, 0)
        top = jnp.where(row < back, pltpu.roll(tail, back, 0), xr[0:SUBLANE, :])
        y = y + cw_ref[w:w + 1, :] * jnp.concatenate([top, xr[SUBLANE:, :]], axis=0)
    tail_scr[...] = x[rows - SUBLANE:rows, :]
    return y


def _ssd_body(z_ref, xbc_ref, dtp_ref, cw_ref, cb_ref, dtb_ref, alog_ref, dsk_ref, ng_ref,
              y_ref, tail_scr, st_scr, *, q, d_ssd):
    step = pl.program_id(1)
    n = SSD_STATE
    gw = d_ssd // SSD_GROUPS
    rpg = gw // SSD_HEAD_DIM

    @pl.when(step == 0)
    def _():
        st_scr[...] = jnp.zeros(st_scr.shape, F32)

    xbc = _silu(_causal_conv(xbc_ref, cw_ref, cb_ref, tail_scr, step, q))
    xs = xbc[:, :d_ssd]
    bm = xbc[:, d_ssd:d_ssd + SSD_GROUPS * n]
    cm = xbc[:, d_ssd + SSD_GROUPS * n:]

    dt = _softplus(dtp_ref[...] + dtb_ref[...])
    adt = dt * (-jnp.exp(alog_ref[...]))
    ri = lax.broadcasted_iota(jnp.int32, (q, q), 0)
    ci = lax.broadcasted_iota(jnp.int32, (q, q), 1)
    causal = ri >= ci
    acum = _dot_sel_lhs(causal.astype(F32), adt)
    er = lax.broadcasted_iota(jnp.int32, (LANE, d_ssd), 0)
    ec = lax.broadcasted_iota(jnp.int32, (LANE, d_ssd), 1)
    expand = (ec // SSD_HEAD_DIM == er).astype(F32)
    a_x = _dot_sel_rhs(acum, expand)
    dt_x = _dot_sel_rhs(dt, expand)
    acum_t = acum.T
    a_last = a_x[q - 1:q, :]
    exp_a = jnp.exp(a_x)
    xdt = xs * dt_x
    xsw = xdt * jnp.exp(a_last - a_x)
    cdec = jnp.exp(a_last)

    ys = []
    for g in range(SSD_GROUPS):
        bm_g = bm[:, g * n:(g + 1) * n]
        cm_g = cm[:, g * n:(g + 1) * n].astype(BF16)
        cb = lax.dot_general(cm_g, bm_g.astype(BF16), (((1,), (1,)), ((), ())), preferred_element_type=F32)
        yd = []
        for r in range(rpg):
            h = g * rpg + r
            seg = acum[:, h:h + 1] - acum_t[h:h + 1, :]
            decay = jnp.exp(jnp.where(causal, seg, -jnp.inf))
            m = (cb * decay).astype(BF16)
            xh = xdt[:, h * SSD_HEAD_DIM:(h + 1) * SSD_HEAD_DIM].astype(BF16)
            yd.append(jnp.dot(m, xh, preferred_element_type=F32))
        yd = jnp.concatenate(yd, axis=1)
        gs = slice(g * gw, (g + 1) * gw)
        prev = st_scr[g]
        yoff = jnp.dot(cm_g, prev.astype(BF16), preferred_element_type=F32) * exp_a[:, gs]
        st = jnp.dot(bm_g.T.astype(BF16), xsw[:, gs].astype(BF16), preferred_element_type=F32)
        st_scr[g] = prev * cdec[:, gs] + st
        ys.append(yd + yoff + dsk_ref[:, gs] * xs[:, gs])

    z = z_ref[...]
    for g in range(SSD_GROUPS):
        gs = slice(g * gw, (g + 1) * gw)
        y_ref[:, gs] = _rms(ys[g] * _silu(z[:, gs]), ng_ref[:, gs])


def _ssd(z, xbc, dtp, conv_w, conv_b, dt_bias, a_log, d_skip, norm_g, *, bsz):
    t, d_ssd = z.shape
    cch = xbc.shape[1]
    q = SSD_CHUNK
    nc = t // bsz // q
    heads = d_ssd // SSD_HEAD_DIM
    pad = lambda v, fill: jnp.concatenate([v, jnp.full((LANE - heads,), fill, F32)]).reshape(1, LANE)
    row = lambda c: pl.BlockSpec((q, c), lambda b, i: (b * nc + i, 0))
    return pl.pallas_call(
        functools.partial(_ssd_body, q=q, d_ssd=d_ssd),
        grid=(bsz, nc),
        in_specs=[row(d_ssd), row(cch), row(LANE), _const_spec((CONV_W, cch)), _const_spec((1, cch)),
                  _const_spec((1, LANE)), _const_spec((1, LANE)), _const_spec((1, d_ssd)),
                  _const_spec((1, d_ssd))],
        out_specs=row(d_ssd),
        out_shape=jax.ShapeDtypeStruct((t, d_ssd), F32),
        scratch_shapes=[pltpu.VMEM((SUBLANE, cch), F32),
                        pltpu.VMEM((SSD_GROUPS, SSD_STATE, d_ssd // SSD_GROUPS), F32)],
        compiler_params=_cparams("parallel", "arbitrary"),
        name="ssd",
    )(z, xbc, dtp, conv_w, conv_b.reshape(1, cch), pad(dt_bias, 0.0), pad(a_log, 0.0),
      jnp.repeat(d_skip, SSD_HEAD_DIM).reshape(1, d_ssd), norm_g.reshape(1, d_ssd))


def _lru_body(xl_ref, gl_ref, cw_ref, cb_ref, wa_ref, ba_ref, wi_ref, bi_ref, lam_ref,
              y_ref, tail_scr, h_scr, *, tl):
    step = pl.program_id(1)

    @pl.when(step == 0)
    def _():
        h_scr[...] = jnp.zeros(h_scr.shape, F32)

    xr = _causal_conv(xl_ref, cw_ref, cb_ref, tail_scr, step, tl)
    xrb = xr.astype(BF16)
    r = jax.nn.sigmoid(jnp.dot(xrb, wa_ref[...], preferred_element_type=F32) + ba_ref[...])
    i = jax.nn.sigmoid(jnp.dot(xrb, wi_ref[...], preferred_element_type=F32) + bi_ref[...])
    log_a = -LRU_C * r * _softplus(-lam_ref[...])
    a = jnp.exp(log_a)
    b = jnp.sqrt(-jnp.tanh(log_a) * (a * a + 1.0)) * (i * xr)
    rows = lax.broadcasted_iota(jnp.int32, a.shape, 0) % SUBLANE
    s = 1
    while s < SUBLANE:
        keep = rows >= s
        a_sh = jnp.where(keep, pltpu.roll(a, s, 0), 1.0)
        b_sh = jnp.where(keep, pltpu.roll(b, s, 0), 0.0)
        b = a * b_sh + b
        a = a * a_sh
        s *= 2
    h = h_scr[0:1, :]
    gate = jax.nn.gelu(gl_ref[...])
    for g in range(tl // SUBLANE):
        grp = slice(g * SUBLANE, (g + 1) * SUBLANE)
        hs = a[grp, :] * h + b[grp, :]
        y_ref[grp, :] = hs * gate[grp, :]
        h = hs[SUBLANE - 1:SUBLANE, :]
    h_scr[...] = jnp.broadcast_to(h, h_scr.shape)


def _block_diag(w):
    nb, bw, _ = w.shape
    out = jnp.zeros((nb * bw, nb * bw), w.dtype)
    for k in range(nb):
        out = out.at[k * bw:(k + 1) * bw, k * bw:(k + 1) * bw].set(w[k])
    return out


def _lru(xl, gl, conv_w, conv_b, wa, ba, wi, bi, lam, *, bsz, tl=256):
    t, dl = xl.shape
    nt = t // bsz // tl
    row = pl.BlockSpec((tl, dl), lambda b, i: (b * nt + i, 0))
    vec = _const_spec((1, dl))
    return pl.pallas_call(
        functools.partial(_lru_body, tl=tl),
        grid=(bsz, nt),
        in_specs=[row, row, _const_spec((CONV_W, dl)), vec, _const_spec((dl, dl)), vec,
                  _const_spec((dl, dl)), vec, vec],
        out_specs=row,
        out_shape=jax.ShapeDtypeStruct((t, dl), F32),
        scratch_shapes=[pltpu.VMEM((SUBLANE, dl), F32), pltpu.VMEM((SUBLANE, dl), F32)],
        compiler_params=_cparams("parallel", "arbitrary"),
        name="rglru",
    )(xl, gl, conv_w, conv_b.reshape(1, dl), _block_diag(wa).astype(BF16), ba.reshape(1, dl),
      _block_diag(wi).astype(BF16), bi.reshape(1, dl), lam.reshape(1, dl))


def _rope(x, ct, st, axis):
    width = x.shape[axis]
    fm = lax.broadcasted_iota(jnp.int32, x.shape, axis) % ATT_HEAD_DIM
    half = ROPE_ROT // 2
    partner = jnp.where(fm < half, pltpu.roll(x, width - half, axis), pltpu.roll(x, half, axis))
    return x * ct + partner * st


def _dot_nt(a, b):
    return lax.dot_general(a, b, (((1,), (1,)), ((), ())), preferred_element_type=F32)


def _prep_body(cq_ref, ckv_ref, misc_ref, ct_ref, st_ref, ctt_ref, stt_ref, cqg_ref, ckvg_ref, wuqt_ref,
               wuk_ref, wuvt_ref, qg_ref, kg_ref, wqit_ref, kig_ref,
               k_ref, ki_ref, vt_ref, qt_ref, qit_ref, wc_ref, *, tm):
    ct = ct_ref[...]
    st = st_ref[...]
    d_att = wuk_ref.shape[1]
    gr = lax.broadcasted_iota(jnp.int32, (d_att, d_att), 0) // ATT_HEAD_DIM
    gc = lax.broadcasted_iota(jnp.int32, (d_att, d_att), 1) // ATT_HEAD_DIM
    head_mean = jnp.where(gr == gc, 1.0 / ATT_HEAD_DIM, 0.0).astype(F32)

    cqn = _rms(cq_ref[...], cqg_ref[...]).astype(BF16)
    ckvn = _rms(ckv_ref[...], ckvg_ref[...]).astype(BF16)

    qt = _dot_nt(wuqt_ref[...], cqn)
    ms = _dot_sel_lhs(head_mean, qt * qt)
    qt = qt * lax.rsqrt(ms + RMS_EPS) * qg_ref[...]
    qt_ref[...] = (_rope(qt, ctt_ref[...], stt_ref[...], 0) * (ATT_HEAD_DIM ** -0.5 * LOG2_E)).astype(BF16)
    vt_ref[0:d_att, :] = _dot_nt(wuvt_ref[...], ckvn).astype(BF16)
    vt_ref[d_att:, :] = jnp.ones((vt_ref.shape[0] - d_att, tm), BF16)

    k = jnp.dot(ckvn, wuk_ref[...], preferred_element_type=F32)
    ms = _dot_sel_rhs(k * k, head_mean)
    k_ref[...] = _rope(k * lax.rsqrt(ms + RMS_EPS) * kg_ref[...], ct, st, 1).astype(BF16)

    qit = _rope(_dot_nt(wqit_ref[...], cqn), ctt_ref[...], stt_ref[...], 0)
    misc = misc_ref[...]
    misc_t = misc.T
    wscale = IDX_HEADS ** -0.5 * IDX_DIM ** -0.5
    for j in range(tm // Q_BLOCK):
        cols = slice(j * Q_BLOCK, (j + 1) * Q_BLOCK)
        for h in range(IDX_HEADS):
            qit_ref[j, 0:IDX_DIM, h * Q_BLOCK:(h + 1) * Q_BLOCK] = (
                qit[h * IDX_DIM:(h + 1) * IDX_DIM, cols].astype(BF16))
        qit_ref[j, IDX_DIM:, :] = jnp.zeros((LANE - IDX_DIM, IDX_HEADS * Q_BLOCK), BF16)
        wrow = jnp.concatenate([misc_t[IDX_DIM + h:IDX_DIM + h + 1, cols] for h in range(IDX_HEADS)], axis=1)
        wc_ref[j] = jnp.broadcast_to(wrow * wscale, (SUBLANE, IDX_HEADS * Q_BLOCK))

    lane = lax.broadcasted_iota(jnp.int32, misc.shape, 1)
    km = jnp.where(lane < IDX_DIM, misc, 0.0)
    ms = jnp.sum(km * km, axis=-1, keepdims=True) * (1.0 / IDX_DIM)
    kin = km * lax.rsqrt(ms + RMS_EPS) * kig_ref[...]
    ki_ref[...] = _rope(kin, ct[:, :LANE], st[:, :LANE], 1).astype(BF16)


def _rope_lane_tables(length, width):
    half = ROPE_ROT // 2
    inv = ROPE_THETA ** (-jnp.arange(half, dtype=F32) * 2.0 / ROPE_ROT)
    ang = jnp.arange(length, dtype=F32)[:, None] * inv[None, :]
    cos, sin = jnp.cos(ang), jnp.sin(ang)
    ones = jnp.ones((length, ATT_HEAD_DIM - ROPE_ROT), F32)
    ct = jnp.concatenate([cos, cos, ones], axis=1)
    st = jnp.concatenate([-sin, sin, 0.0 * ones], axis=1)
    reps = width // ATT_HEAD_DIM
    return jnp.tile(ct, (1, reps)), jnp.tile(st, (1, reps))


def _dsa_prep(cq, ckv, misc, p, *, bsz):
    t, qr = cq.shape
    kvr = ckv.shape[1]
    tm = KEY_CHUNK
    length = t // bsz
    nt = length // tm
    d_att = ATT_HEADS * ATT_HEAD_DIM
    ct, st = _rope_lane_tables(length, d_att)
    wukv = p["w_ukv"].reshape(kvr, ATT_HEADS, 2, ATT_HEAD_DIM)
    wuk = wukv[:, :, 0, :].reshape(kvr, d_att).astype(BF16)
    wuvt = wukv[:, :, 1, :].reshape(kvr, d_att).T.astype(BF16)
    kig = jnp.concatenate([p["kidx_norm"], jnp.zeros((LANE - IDX_DIM,), F32)]).reshape(1, LANE)
    row = lambda c: pl.BlockSpec((tm, c), lambda b, i: (b * nt + i, 0))
    tab = pl.BlockSpec((tm, d_att), lambda b, i: (i, 0))
    tab_t = pl.BlockSpec((d_att, tm), lambda b, i: (0, i))
    nqb = tm // Q_BLOCK
    hq = IDX_HEADS * Q_BLOCK
    return pl.pallas_call(
        functools.partial(_prep_body, tm=tm),
        grid=(bsz, nt),
        in_specs=[row(qr), row(kvr), row(LANE), tab, tab, tab_t, tab_t, _const_spec((1, qr)),
                  _const_spec((1, kvr)), _const_spec((d_att, qr)), _const_spec((kvr, d_att)),
                  _const_spec((d_att, kvr)), _const_spec((d_att, 1)), _const_spec((1, d_att)),
                  _const_spec((IDX_HEADS * IDX_DIM, qr)), _const_spec((1, LANE))],
        out_specs=[row(d_att), row(LANE),
                   pl.BlockSpec((None, d_att + V_ONES_ROWS, tm), lambda b, i: (b * nt + i, 0, 0)),
                   pl.BlockSpec((None, d_att, tm), lambda b, i: (b * nt + i, 0, 0)),
                   pl.BlockSpec((nqb, LANE, hq), lambda b, i: (b * nt + i, 0, 0)),
                   pl.BlockSpec((nqb, SUBLANE, hq), lambda b, i: (b * nt + i, 0, 0))],
        out_shape=[jax.ShapeDtypeStruct((t, d_att), BF16),
                   jax.ShapeDtypeStruct((t, LANE), BF16),
                   jax.ShapeDtypeStruct((bsz * nt, d_att + V_ONES_ROWS, tm), BF16),
                   jax.ShapeDtypeStruct((bsz * nt, d_att, tm), BF16),
                   jax.ShapeDtypeStruct((t // Q_BLOCK, LANE, hq), BF16),
                   jax.ShapeDtypeStruct((t // Q_BLOCK, SUBLANE, hq), F32)],
        compiler_params=_cparams("parallel", "parallel"),
        name="dsa_prep",
    )(cq, ckv, misc, ct, st, ct.T, st.T, p["cq_norm"].reshape(1, qr), p["ckv_norm"].reshape(1, kvr),
      p["w_uq"].T.astype(BF16), wuk, wuvt, jnp.tile(p["q_norm"], ATT_HEADS).reshape(d_att, 1),
      jnp.tile(p["k_norm"], ATT_HEADS).reshape(1, d_att), p["w_qidx"].T.astype(BF16), kig)


INT_MIN = -2 ** 31
F32_TINY = 2.0 ** -126
F32_LOWEST = -3.4028234663852886e38
SEARCH_VALUE_STEPS = 16
SEARCH_CAP = 80


def _order_key_to_float(u):
    sk = u ^ jnp.int32(INT_MIN)
    fb = sk ^ ((sk >> 31) & jnp.int32(0x7FFFFFFF))
    return lax.bitcast_convert_type(fb, F32)


def _float_to_order_key(f):
    b = lax.bitcast_convert_type(f, jnp.int32)
    return (b ^ ((b >> 31) & jnp.int32(0x7FFFFFFF))) ^ jnp.int32(INT_MIN)


def _fold_rows(x, op):
    parts = [x[i * SUBLANE:(i + 1) * SUBLANE, :] for i in range(x.shape[0] // SUBLANE)]
    while len(parts) > 1:
        parts = [op(parts[i], parts[i + 1]) for i in range(0, len(parts), 2)]
    return parts[0]


def _dsa_body(qit_ref, wc_ref, qt_ref, ki_ref, k_ref, vt_ref, o_ref,
              sc_scr, lg_scr, qbd_scr, oacc_scr, *, n_sel):
    kc = KEY_CHUNK
    qb = pl.program_id(1)
    nk = qb // (kc // Q_BLOCK) + 1
    qpos = qb * Q_BLOCK + lax.broadcasted_iota(jnp.int32, (1, Q_BLOCK), 1)
    hq = IDX_HEADS * Q_BLOCK
    k_f = jnp.float32(n_sel)

    def chunk(c):
        return pl.ds(pl.multiple_of(c * kc, kc), kc)

    def over_chunks(body, carry):
        def run(first, count, cr):
            for j in range(count):
                cr = body(first + j, cr)
            return cr
        carry = lax.fori_loop(0, nk // 4, lambda i, cr: run(4 * i, 4, cr), carry)
        done = (nk // 4) * 4
        carry = lax.cond(nk % 4 >= 2, lambda cr: run(done, 2, cr), lambda cr: cr, carry)
        return lax.cond(nk % 2 == 1, lambda cr: run(nk - 1, 1, cr), lambda cr: cr, carry)

    qit = qit_ref[...]
    wc = wc_ref[0:1, :]

    def score_chunk(c, carry):
        amax, c_pos, c_nn = carry
        s4 = jnp.dot(ki_ref[chunk(c), :], qit, preferred_element_type=F32)
        s4 = jnp.maximum(s4, 0.0) * wc
        s = (s4[:, 0:Q_BLOCK] + s4[:, Q_BLOCK:2 * Q_BLOCK]) + (s4[:, 2 * Q_BLOCK:3 * Q_BLOCK] + s4[:, 3 * Q_BLOCK:])
        kpos = c * kc + lax.broadcasted_iota(jnp.int32, (kc, Q_BLOCK), 0)
        sm = jnp.where(kpos <= qpos, s, -jnp.inf)
        sc_scr[chunk(c), :] = sm
        return (jnp.maximum(amax, _fold_rows(jnp.abs(s), jnp.maximum)),
                c_pos + _fold_rows(jnp.where(sm >= F32_TINY, 1.0, 0.0), jnp.add),
                c_nn + _fold_rows(jnp.where(sm >= 0.0, 1.0, 0.0), jnp.add))

    zeros8 = jnp.zeros((SUBLANE, Q_BLOCK), F32)
    amax, c_pos, c_nn = over_chunks(score_chunk, (zeros8, zeros8, zeros8))
    amax = jnp.max(amax, axis=0, keepdims=True)
    c_pos = jnp.sum(c_pos, axis=0, keepdims=True)
    c_nn = jnp.sum(c_nn, axis=0, keepdims=True)

    def count_ge(*thrs):
        def body(c, accs):
            blk = sc_scr[chunk(c), :]
            return tuple(a + _fold_rows(jnp.where(blk >= t, 1.0, 0.0), jnp.add) for a, t in zip(accs, thrs))
        accs = over_chunks(body, tuple(jnp.zeros((SUBLANE, Q_BLOCK), F32) for _ in thrs))
        return tuple(jnp.sum(a, axis=0, keepdims=True) for a in accs)

    n_valid = (qpos + 1).astype(F32)
    take_all = n_valid <= k_f
    zero_tie = (c_pos < k_f) & (c_nn >= k_f)
    pos = c_pos >= k_f
    hi_top = _order_key_to_float(_float_to_order_key(amax) + 1)
    lo = jnp.where(take_all, F32_LOWEST, jnp.where(zero_tie, 0.0, jnp.where(pos, F32_TINY, -amax)))
    hi = jnp.where(take_all, jnp.inf, jnp.where(zero_tie, F32_TINY, jnp.where(pos, hi_top, 0.0)))
    cnt_lo = jnp.where(take_all | (~zero_tie & ~pos), n_valid, jnp.where(zero_tie, c_nn, c_pos))
    cnt_hi = jnp.where(take_all | (~zero_tie & pos), 0.0, jnp.where(zero_tie, c_pos, c_nn))
    settled = take_all | zero_tie

    def search_step(st, on_values):
        lk, hk, lo, hi, cnt_lo, cnt_hi = st
        midk = lk + ((hk - lk) >> 1)
        mid = _order_key_to_float(midk)
        if on_values:
            vmid = lo + (hi - lo) * 0.5
            inside = (vmid > lo) & (vmid < hi)
            mid = jnp.where(inside, vmid, mid)
            midk = jnp.where(inside, _float_to_order_key(vmid), midk)
        cnt, = count_ge(mid)
        ge = cnt >= k_f
        return (jnp.where(ge, midk, lk), jnp.where(ge, hk, midk), jnp.where(ge, mid, lo), jnp.where(ge, hi, mid),
                jnp.where(ge, cnt, cnt_lo), jnp.where(ge, cnt_hi, cnt))

    def search_cond(carry):
        it, (lk, hk, _, _, cnt_lo, _) = carry
        open_ = jnp.logical_not(settled | (cnt_lo == k_f) | (hk - lk <= 1))
        return (it < SEARCH_CAP) & (jnp.sum(jnp.where(open_, 1.0, 0.0)) > 0.0)

    def search_pair(carry):
        it, st = carry
        return it + 1, search_step(search_step(st, False), False)

    st = (_float_to_order_key(lo), _float_to_order_key(hi), lo, hi, cnt_lo, cnt_hi)
    st = lax.fori_loop(0, SEARCH_VALUE_STEPS, lambda _, s: search_step(s, True), st)
    _, (_, _, lo, hi, cnt_lo, cnt_hi) = lax.while_loop(search_cond, search_pair, (jnp.int32(0), st))
    need = k_f - cnt_hi

    sub = Q_BLOCK
    ri = lax.broadcasted_iota(jnp.int32, (sub, sub), 0)
    ci = lax.broadcasted_iota(jnp.int32, (sub, sub), 1)
    before = (ri > ci).astype(BF16)

    qbd_scr[...] = jnp.zeros(qbd_scr.shape, BF16)
    for h in range(ATT_HEADS):
        rows = slice(h * ATT_HEAD_DIM, (h + 1) * ATT_HEAD_DIM)
        qbd_scr[rows, h * Q_BLOCK:(h + 1) * Q_BLOCK] = qt_ref[rows, :]
    qbd = qbd_scr[...]

    def logit_chunk(c, carry):
        m, seen = carry
        for j in range(kc // sub):
            rows = pl.ds(pl.multiple_of(c * kc + j * sub, sub), sub)
            blk = sc_scr[rows, :]
            in_hi = blk >= hi
            tie = (blk >= lo) & jnp.logical_not(in_hi)
            tie_f = jnp.where(tie, 1.0, 0.0)
            rank = jnp.dot(before, tie_f.astype(BF16), preferred_element_type=F32) + seen
            sel = in_hi | (tie & (rank < need))
            seen = seen + jnp.sum(tie_f, axis=0, keepdims=True)
            b = jnp.where(sel, 0.0, -jnp.inf)
            lg = jnp.dot(k_ref[rows, :], qbd, preferred_element_type=F32) + jnp.concatenate([b] * ATT_HEADS, axis=1)
            lg_scr[rows, :] = lg
            m = jnp.maximum(m, jnp.max(lg, axis=0, keepdims=True))
        return m, seen

    m, _ = over_chunks(logit_chunk, (jnp.full((1, hq), -jnp.inf, F32), jnp.zeros((1, Q_BLOCK), F32)))
    oacc_scr[...] = jnp.zeros(oacc_scr.shape, F32)

    def value_chunk(c, carry):
        acc = oacc_scr[...]
        depth = 2 * Q_BLOCK
        for j in range(kc // depth):
            rows = pl.ds(pl.multiple_of(c * kc + j * depth, depth), depth)
            p = jnp.exp2(lg_scr[rows, :] - m).astype(BF16)
            acc = acc + jnp.dot(vt_ref[c, :, j * depth:(j + 1) * depth], p, preferred_element_type=F32)
        oacc_scr[...] = acc
        return carry

    over_chunks(value_chunk, 0)
    d_att = ATT_HEADS * ATT_HEAD_DIM
    outs = []
    for h in range(ATT_HEADS):
        cols = slice(h * Q_BLOCK, (h + 1) * Q_BLOCK)
        ssum = oacc_scr[d_att:d_att + 1, cols]
        outs.append(oacc_scr[h * ATT_HEAD_DIM:(h + 1) * ATT_HEAD_DIM, cols] / ssum)
    o_ref[...] = jnp.concatenate(outs, axis=0).T


def _dsa(k, ki, vt, qt, qit, wc, *, bsz):
    t, d_att = k.shape
    length = t // bsz
    nb = length // Q_BLOCK
    kc = KEY_CHUNK
    nkc = length // kc
    hq = IDX_HEADS * Q_BLOCK
    per_q = kc // Q_BLOCK
    v_rows = d_att + V_ONES_ROWS
    vt = vt.reshape(bsz, nkc, v_rows, kc)
    return pl.pallas_call(
        functools.partial(_dsa_body, n_sel=min(TOP_K, length // 4)),
        grid=(bsz, nb),
        in_specs=[pl.BlockSpec((None, LANE, hq), lambda b, i: (b * nb + i, 0, 0)),
                  pl.BlockSpec((None, SUBLANE, hq), lambda b, i: (b * nb + i, 0, 0)),
                  pl.BlockSpec((None, d_att, Q_BLOCK), lambda b, i: (b * nkc + i // per_q, 0, i % per_q)),
                  pl.BlockSpec((length, LANE), lambda b, i: (b, 0)),
                  pl.BlockSpec((length, d_att), lambda b, i: (b, 0)),
                  pl.BlockSpec((None, nkc, v_rows, kc), lambda b, i: (b, 0, 0, 0))],
        out_specs=pl.BlockSpec((Q_BLOCK, d_att), lambda b, i: (b * nb + i, 0)),
        out_shape=jax.ShapeDtypeStruct((t, d_att), F32),
        scratch_shapes=[pltpu.VMEM((length, Q_BLOCK), F32), pltpu.VMEM((length, hq), F32),
                        pltpu.VMEM((d_att, hq), BF16), pltpu.VMEM((v_rows, hq), F32)],
        compiler_params=_cparams("parallel", "arbitrary"),
        name="dsa",
    )(qit, wc, qt, ki, k, vt)


def _pad_cols(w, width):
    return jnp.concatenate([w, jnp.zeros((w.shape[0], width - w.shape[1]), w.dtype)], axis=1)


def kernel(x, norm_ffn1, ffn1_w13, ffn1_w2, norm_mix, w_in, ssd_conv_w, ssd_conv_b, ssd_dt_bias, ssd_a_log, ssd_d, ssd_norm, cq_norm, ckv_norm, w_uq, w_ukv, q_norm, k_norm, w_qidx, kidx_norm, lru_conv_w, lru_conv_b, lru_wa, lru_ba, lru_wi, lru_bi, lru_lambda, w_out, norm_ffn2, ffn2_w13, ffn2_w2):
    bsz, length, d = x.shape
    depth = w_in.shape[0]
    d_ssd = ssd_norm.shape[1]
    conv_ch = ssd_conv_b.shape[1]
    heads = ssd_d.shape[1]
    q_rank, kv_rank = cq_norm.shape[1], ckv_norm.shape[1]
    d_lru = lru_lambda.shape[1]
    sizes = [d_ssd, conv_ch, heads, q_rank, kv_rank, IDX_DIM, IDX_HEADS, d_lru, d_lru]
    offs = [0]
    for s in sizes:
        offs.append(offs[-1] + s)
    col = lambda wl, k: wl[:, offs[k]:offs[k + 1]]
    widths = (d_ssd, conv_ch, LANE, q_rank, kv_rank, LANE, d_lru, d_lru)

    xt = x.reshape(bsz * length, d)
    w13_1, w2_1, w13_2, w2_2 = (w.astype(BF16) for w in (ffn1_w13, ffn1_w2, ffn2_w13, ffn2_w2))
    for l in range(depth):
        wl = w_in[l]
        w_pad = jnp.concatenate(
            [col(wl, 0), col(wl, 1), _pad_cols(col(wl, 2), LANE), col(wl, 3), col(wl, 4),
             _pad_cols(jnp.concatenate([col(wl, 5), col(wl, 6)], axis=1), LANE), col(wl, 7), col(wl, 8)],
            axis=1).astype(BF16)
        xt, z, xbc, dtp, cq, ckv, misc, xl, gl = _ffn(xt, norm_ffn1[l], w13_1, w2_1, l,
                                                      proj=(norm_mix[l], w_pad, widths))
        y_ssd = _ssd(z, xbc, dtp, ssd_conv_w[l], ssd_conv_b[l], ssd_dt_bias[l], ssd_a_log[l], ssd_d[l],
                     ssd_norm[l], bsz=bsz)
        y_lru = _lru(xl, gl, lru_conv_w[l], lru_conv_b[l], lru_wa[l], lru_ba[l], lru_wi[l], lru_bi[l],
                     lru_lambda[l], bsz=bsz)
        prm = dict(cq_norm=cq_norm[l], ckv_norm=ckv_norm[l], w_uq=w_uq[l], w_ukv=w_ukv[l], q_norm=q_norm[l],
                   k_norm=k_norm[l], w_qidx=w_qidx[l], kidx_norm=kidx_norm[l])
        k, ki, vt, qt, qit, wc = _dsa_prep(cq, ckv, misc, prm, bsz=bsz)
        y_att = _dsa(k, ki, vt, qt, qit, wc, bsz=bsz)
        xt = _ffn(xt, norm_ffn2[l], w13_2, w2_2, l, mixed=([y_ssd, y_att, y_lru], w_out[l].astype(BF16)))
    return xt.reshape(bsz, length, d)
```

```python
import functools

import jax
import jax.numpy as jnp
from jax import lax
from jax.experimental import pallas as pl
from jax.experimental.pallas import tpu as pltpu

F32 = jnp.float32
BF16 = jnp.bfloat16

RMS_EPS = 1e-6
LOG2_E = 1.4426950408889634
SSD_HEAD_DIM = 64
SSD_GROUPS = 2
SSD_STATE = 128
SSD_CHUNK = 128
ATT_HEADS = 4
ATT_HEAD_DIM = 64
IDX_HEADS = 4
IDX_DIM = 64
TOP_K = 256
Q_BLOCK = 128
ROPE_THETA = 500000.0
ROPE_ROT = 16
LRU_C = 8.0
CONV_W = 4
LANE = 128
SUBLANE = 8
KEY_CHUNK = 512
V_ONES_ROWS = 16
VMEM_LIMIT = 56 * 1024 * 1024


def _cparams(*sem):
    return pltpu.CompilerParams(dimension_semantics=sem, vmem_limit_bytes=VMEM_LIMIT)


def _const_spec(shape):
    nd = len(shape)
    return pl.BlockSpec(shape, lambda *_: (0,) * nd, pipeline_mode=pl.Buffered(1))


def _rms(x, g):
    ms = jnp.mean(x * x, axis=-1, keepdims=True)
    return x * lax.rsqrt(ms + RMS_EPS) * g


def _softplus(x):
    return jnp.maximum(x, 0.0) + jnp.log1p(jnp.exp(-jnp.abs(x)))


def _silu(x):
    return x * jax.nn.sigmoid(x)


def _split3(x):
    hi = x.astype(BF16)
    r = x - hi.astype(F32)
    mid = r.astype(BF16)
    return hi, mid, (r - mid.astype(F32)).astype(BF16)


def _dot_sel_rhs(x, sel):
    sel = sel.astype(BF16)
    return sum(jnp.dot(p, sel, preferred_element_type=F32) for p in _split3(x))


def _dot_sel_lhs(sel, x):
    sel = sel.astype(BF16)
    return sum(jnp.dot(sel, p, preferred_element_type=F32) for p in _split3(x))


def _ffn_body(*refs, d_ff, tf, n_mix, proj_widths):
    refs = list(refs)
    x = refs.pop(0)[...]
    if n_mix:
        y_refs = [refs.pop(0) for _ in range(n_mix)]
        wo_ref = refs.pop(0)
        off = 0
        for y_ref in y_refs:
            wd = y_ref.shape[1]
            x = x + jnp.dot(y_ref[...].astype(BF16), wo_ref[off:off + wd, :], preferred_element_type=F32)
            off += wd
    g_ref, w13_ref, w2_ref = refs.pop(0), refs.pop(0), refs.pop(0)
    if proj_widths:
        gp_ref, wi_ref = refs.pop(0), refs.pop(0)
    o_ref = refs.pop(0)

    h = _rms(x, g_ref[...]).astype(BF16)
    acc = jnp.zeros(x.shape, F32)
    for j in range(d_ff // tf):
        g = jnp.dot(h, w13_ref[:, j * tf:(j + 1) * tf], preferred_element_type=F32)
        u = jnp.dot(h, w13_ref[:, d_ff + j * tf:d_ff + (j + 1) * tf], preferred_element_type=F32)
        a = (_silu(g) * u).astype(BF16)
        acc = acc + jnp.dot(a, w2_ref[j * tf:(j + 1) * tf, :], preferred_element_type=F32)
    x = x + 0.5 * acc
    o_ref[...] = x

    if proj_widths:
        hp = _rms(x, gp_ref[...]).astype(BF16)
        off = 0
        for p_ref, wd in zip(refs, proj_widths):
            p_ref[...] = _dot_nt(hp, wi_ref[off:off + wd, :])
            off += wd


def _ffn(x, g, w13, w2, layer, *, mixed=None, proj=None, tm=512, tf=256):
    t, d = x.shape
    d_ff = w2.shape[1]
    row = lambda c: pl.BlockSpec((tm, c), lambda i: (i, 0))
    layer_spec = lambda r, c: pl.BlockSpec((None, r, c), lambda i: (layer, 0, 0), pipeline_mode=pl.Buffered(1))
    args, in_specs = [x], [row(d)]
    if mixed is not None:
        ys, w_out = mixed
        args += [*ys, w_out]
        in_specs += [row(y.shape[1]) for y in ys] + [_const_spec(w_out.shape)]
    args += [g.reshape(1, d), w13, w2]
    in_specs += [_const_spec((1, d)), layer_spec(d, 2 * d_ff), layer_spec(d_ff, d)]
    out_specs, out_shape, widths = [row(d)], [jax.ShapeDtypeStruct((t, d), F32)], ()
    if proj is not None:
        gp, w_in, widths = proj
        args += [gp.reshape(1, d), w_in]
        in_specs += [_const_spec((1, d)), _const_spec(w_in.shape)]
        out_specs += [row(wd) for wd in widths]
        out_shape += [jax.ShapeDtypeStruct((t, wd), F32) for wd in widths]
    outs = pl.pallas_call(
        functools.partial(_ffn_body, d_ff=d_ff, tf=tf, n_mix=len(mixed[0]) if mixed is not None else 0,
                          proj_widths=tuple(widths)),
        grid=(t // tm,),
        in_specs=in_specs,
        out_specs=out_specs,
        out_shape=out_shape,
        compiler_params=_cparams("parallel"),
        name="ffn",
    )(*args)
    return outs if proj is not None else outs[0]


def _causal_conv(x_ref, cw_ref, cb_ref, tail_scr, step, rows):
    @pl.when(step == 0)
    def _():
        tail_scr[...] = jnp.zeros(tail_scr.shape, F32)

    x = x_ref[...]
    tail = tail_scr[...]
    row = lax.broadcasted_iota(jnp.int32, tail.shape, 0)
    y = cb_ref[...] + cw_ref[CONV_W - 1:CONV_W, :] * x
    for w in range(CONV_W - 1):
        back = CONV_W - 1 - w
        xr = pltpu.roll(x, back, 0)
        top = jnp.where(row < back, pltpu.roll(tail, back, 0), xr[0:SUBLANE, :])
        y = y + cw_ref[w:w + 1, :] * jnp.concatenate([top, xr[SUBLANE:, :]], axis=0)
    tail_scr[...] = x[rows - SUBLANE:rows, :]
    return y


def _ssd_body(z_ref, xbc_ref, dtp_ref, cw_ref, cb_ref, dtb_ref, alog_ref, dsk_ref, ng_ref,
              y_ref, tail_scr, st_scr, *, q, d_ssd):
    step = pl.program_id(1)
    n = SSD_STATE
    gw = d_ssd // SSD_GROUPS
    rpg = gw // SSD_HEAD_DIM

    @pl.when(step == 0)
    def _():
        st_scr[...] = jnp.zeros(st_scr.shape, F32)

    xbc = _silu(_causal_conv(xbc_ref, cw_ref, cb_ref, tail_scr, step, q))
    xs = xbc[:, :d_ssd]
    bm = xbc[:, d_ssd:d_ssd + SSD_GROUPS * n]
    cm = xbc[:, d_ssd + SSD_GROUPS * n:]

    dt = _softplus(dtp_ref[...] + dtb_ref[...])
    adt = dt * (-jnp.exp(alog_ref[...]))
    ri = lax.broadcasted_iota(jnp.int32, (q, q), 0)
    ci = lax.broadcasted_iota(jnp.int32, (q, q), 1)
    causal = ri >= ci
    acum = _dot_sel_lhs(causal.astype(F32), adt)
    er = lax.broadcasted_iota(jnp.int32, (LANE, d_ssd), 0)
    ec = lax.broadcasted_iota(jnp.int32, (LANE, d_ssd), 1)
    expand = (ec // SSD_HEAD_DIM == er).astype(F32)
    a_x = _dot_sel_rhs(acum, expand)
    dt_x = _dot_sel_rhs(dt, expand)
    acum_t = acum.T
    a_last = a_x[q - 1:q, :]
    exp_a = jnp.exp(a_x)
    xdt = xs * dt_x
    xsw = xdt * jnp.exp(a_last - a_x)
    cdec = jnp.exp(a_last)

    ys = []
    for g in range(SSD_GROUPS):
        bm_g = bm[:, g * n:(g + 1) * n]
        cm_g = cm[:, g * n:(g + 1) * n].astype(BF16)
        cb = lax.dot_general(cm_g, bm_g.astype(BF16), (((1,), (1,)), ((), ())), preferred_element_type=F32)
        yd = []
        for r in range(rpg):
            h = g * rpg + r
            seg = acum[:, h:h + 1] - acum_t[h:h + 1, :]
            decay = jnp.exp(jnp.where(causal, seg, -jnp.inf))
            m = (cb * decay).astype(BF16)
            xh = xdt[:, h * SSD_HEAD_DIM:(h + 1) * SSD_HEAD_DIM].astype(BF16)
            yd.append(jnp.dot(m, xh, preferred_element_type=F32))
        yd = jnp.concatenate(yd, axis=1)
        gs = slice(g * gw, (g + 1) * gw)
        prev = st_scr[g]
        yoff = jnp.dot(cm_g, prev.astype(BF16), preferred_element_type=F32) * exp_a[:, gs]
        st = jnp.dot(bm_g.T.astype(BF16), xsw[:, gs].astype(BF16), preferred_element_type=F32)
        st_scr[g] = prev * cdec[:, gs] + st
        ys.append(yd + yoff + dsk_ref[:, gs] * xs[:, gs])

    z = z_ref[...]
    for g in range(SSD_GROUPS):
        gs = slice(g * gw, (g + 1) * gw)
        y_ref[:, gs] = _rms(ys[g] * _silu(z[:, gs]), ng_ref[:, gs])


def _ssd(z, xbc, dtp, conv_w, conv_b, dt_bias, a_log, d_skip, norm_g, *, bsz):
    t, d_ssd = z.shape
    cch = xbc.shape[1]
    q = SSD_CHUNK
    nc = t // bsz // q
    heads = d_ssd // SSD_HEAD_DIM
    pad = lambda v, fill: jnp.concatenate([v, jnp.full((LANE - heads,), fill, F32)]).reshape(1, LANE)
    row = lambda c: pl.BlockSpec((q, c), lambda b, i: (b * nc + i, 0))
    return pl.pallas_call(
        functools.partial(_ssd_body, q=q, d_ssd=d_ssd),
        grid=(bsz, nc),
        in_specs=[row(d_ssd), row(cch), row(LANE), _const_spec((CONV_W, cch)), _const_spec((1, cch)),
                  _const_spec((1, LANE)), _const_spec((1, LANE)), _const_spec((1, d_ssd)),
                  _const_spec((1, d_ssd))],
        out_specs=row(d_ssd),
        out_shape=jax.ShapeDtypeStruct((t, d_ssd), F32),
        scratch_shapes=[pltpu.VMEM((SUBLANE, cch), F32),
                        pltpu.VMEM((SSD_GROUPS, SSD_STATE, d_ssd // SSD_GROUPS), F32)],
        compiler_params=_cparams("parallel", "arbitrary"),
        name="ssd",
    )(z, xbc, dtp, conv_w, conv_b.reshape(1, cch), pad(dt_bias, 0.0), pad(a_log, 0.0),
      jnp.repeat(d_skip, SSD_HEAD_DIM).reshape(1, d_ssd), norm_g.reshape(1, d_ssd))


def _lru_body(xl_ref, gl_ref, cw_ref, cb_ref, wa_ref, ba_ref, wi_ref, bi_ref, lam_ref,
              y_ref, tail_scr, h_scr, *, tl):
    step = pl.program_id(1)

    @pl.when(step == 0)
    def _():
        h_scr[...] = jnp.zeros(h_scr.shape, F32)

    xr = _causal_conv(xl_ref, cw_ref, cb_ref, tail_scr, step, tl)
    xrb = xr.astype(BF16)
    r = jax.nn.sigmoid(jnp.dot(xrb, wa_ref[...], preferred_element_type=F32) + ba_ref[...])
    i = jax.nn.sigmoid(jnp.dot(xrb, wi_ref[...], preferred_element_type=F32) + bi_ref[...])
    log_a = -LRU_C * r * _softplus(-lam_ref[...])
    a = jnp.exp(log_a)
    b = jnp.sqrt(-jnp.tanh(log_a) * (a * a + 1.0)) * (i * xr)
    rows = lax.broadcasted_iota(jnp.int32, a.shape, 0) % SUBLANE
    s = 1
    while s < SUBLANE:
        keep = rows >= s
        a_sh = jnp.where(keep, pltpu.roll(a, s, 0), 1.0)
        b_sh = jnp.where(keep, pltpu.roll(b, s, 0), 0.0)
        b = a * b_sh + b
        a = a * a_sh
        s *= 2
    h = h_scr[0:1, :]
    gate = jax.nn.gelu(gl_ref[...])
    for g in range(tl // SUBLANE):
        grp = slice(g * SUBLANE, (g + 1) * SUBLANE)
        hs = a[grp, :] * h + b[grp, :]
        y_ref[grp, :] = hs * gate[grp, :]
        h = hs[SUBLANE - 1:SUBLANE, :]
    h_scr[...] = jnp.broadcast_to(h, h_scr.shape)


def _block_diag(w):
    nb, bw, _ = w.shape
    out = jnp.zeros((nb * bw, nb * bw), w.dtype)
    for k in range(nb):
        out = out.at[k * bw:(k + 1) * bw, k * bw:(k + 1) * bw].set(w[k])
    return out


def _lru(xl, gl, conv_w, conv_b, wa, ba, wi, bi, lam, *, bsz, tl=256):
    t, dl = xl.shape
    nt = t // bsz // tl
    row = pl.BlockSpec((tl, dl), lambda b, i: (b * nt + i, 0))
    vec = _const_spec((1, dl))
    return pl.pallas_call(
        functools.partial(_lru_body, tl=tl),
        grid=(bsz, nt),
        in_specs=[row, row, _const_spec((CONV_W, dl)), vec, _const_spec((dl, dl)), vec,
                  _const_spec((dl, dl)), vec, vec],
        out_specs=row,
        out_shape=jax.ShapeDtypeStruct((t, dl), F32),
        scratch_shapes=[pltpu.VMEM((SUBLANE, dl), F32), pltpu.VMEM((SUBLANE, dl), F32)],
        compiler_params=_cparams("parallel", "arbitrary"),
        name="rglru",
    )(xl, gl, conv_w, conv_b.reshape(1, dl), _block_diag(wa).astype(BF16), ba.reshape(1, dl),
      _block_diag(wi).astype(BF16), bi.reshape(1, dl), lam.reshape(1, dl))


def _rope(x, ct, st, axis):
    width = x.shape[axis]
    fm = lax.broadcasted_iota(jnp.int32, x.shape, axis) % ATT_HEAD_DIM
    half = ROPE_ROT // 2
    partner = jnp.where(fm < half, pltpu.roll(x, width - half, axis), pltpu.roll(x, half, axis))
    return x * ct + partner * st


def _dot_nt(a, b):
    return lax.dot_general(a, b, (((1,), (1,)), ((), ())), preferred_element_type=F32)


def _prep_body(cq_ref, ckv_ref, misc_ref, ct_ref, st_ref, ctt_ref, stt_ref, cqg_ref, ckvg_ref, wuqt_ref,
               wuk_ref, wuvt_ref, qg_ref, kg_ref, wqit_ref, kig_ref,
               k_ref, ki_ref, vt_ref, qt_ref, qit_ref, wc_ref, *, tm):
    ct = ct_ref[...]
    st = st_ref[...]
    d_att = wuk_ref.shape[1]
    gr = lax.broadcasted_iota(jnp.int32, (d_att, d_att), 0) // ATT_HEAD_DIM
    gc = lax.broadcasted_iota(jnp.int32, (d_att, d_att), 1) // ATT_HEAD_DIM
    head_mean = jnp.where(gr == gc, 1.0 / ATT_HEAD_DIM, 0.0).astype(F32)

    cqn = _rms(cq_ref[...], cqg_ref[...]).astype(BF16)
    ckvn = _rms(ckv_ref[...], ckvg_ref[...]).astype(BF16)

    qt = _dot_nt(wuqt_ref[...], cqn)
    ms = _dot_sel_lhs(head_mean, qt * qt)
    qt = qt * lax.rsqrt(ms + RMS_EPS) * qg_ref[...]
    qt_ref[...] = (_rope(qt, ctt_ref[...], stt_ref[...], 0) * (ATT_HEAD_DIM ** -0.5 * LOG2_E)).astype(BF16)
    vt_ref[0:d_att, :] = _dot_nt(wuvt_ref[...], ckvn).astype(BF16)
    vt_ref[d_att:, :] = jnp.ones((vt_ref.shape[0] - d_att, tm), BF16)

    k = jnp.dot(ckvn, wuk_ref[...], preferred_element_type=F32)
    ms = _dot_sel_rhs(k * k, head_mean)
    k_ref[...] = _rope(k * lax.rsqrt(ms + RMS_EPS) * kg_ref[...], ct, st, 1).astype(BF16)

    qit = _rope(_dot_nt(wqit_ref[...], cqn), ctt_ref[...], stt_ref[...], 0)
    misc = misc_ref[...]
    misc_t = misc.T
    wscale = IDX_HEADS ** -0.5 * IDX_DIM ** -0.5
    for j in range(tm // Q_BLOCK):
        cols = slice(j * Q_BLOCK, (j + 1) * Q_BLOCK)
        for h in range(IDX_HEADS):
            qit_ref[j, 0:IDX_DIM, h * Q_BLOCK:(h + 1) * Q_BLOCK] = (
                qit[h * IDX_DIM:(h + 1) * IDX_DIM, cols].astype(BF16))
        qit_ref[j, IDX_DIM:, :] = jnp.zeros((LANE - IDX_DIM, IDX_HEADS * Q_BLOCK), BF16)
        wrow = jnp.concatenate([misc_t[IDX_DIM + h:IDX_DIM + h + 1, cols] for h in range(IDX_HEADS)], axis=1)
        wc_ref[j] = jnp.broadcast_to(wrow * wscale, (SUBLANE, IDX_HEADS * Q_BLOCK))

    lane = lax.broadcasted_iota(jnp.int32, misc.shape, 1)
    km = jnp.where(lane < IDX_DIM, misc, 0.0)
    ms = jnp.sum(km * km, axis=-1, keepdims=True) * (1.0 / IDX_DIM)
    kin = km * lax.rsqrt(ms + RMS_EPS) * kig_ref[...]
    ki_ref[...] = _rope(kin, ct[:, :LANE], st[:, :LANE], 1).astype(BF16)


def _rope_lane_tables(length, width):
    half = ROPE_ROT // 2
    inv = ROPE_THETA ** (-jnp.arange(half, dtype=F32) * 2.0 / ROPE_ROT)
    ang = jnp.arange(length, dtype=F32)[:, None] * inv[None, :]
    cos, sin = jnp.cos(ang), jnp.sin(ang)
    ones = jnp.ones((length, ATT_HEAD_DIM - ROPE_ROT), F32)
    ct = jnp.concatenate([cos, cos, ones], axis=1)
    st = jnp.concatenate([-sin, sin, 0.0 * ones], axis=1)
    reps = width // ATT_HEAD_DIM
    return jnp.tile(ct, (1, reps)), jnp.tile(st, (1, reps))


def _dsa_prep(cq, ckv, misc, p, *, bsz):
    t, qr = cq.shape
    kvr = ckv.shape[1]
    tm = KEY_CHUNK
    length = t // bsz
    nt = length // tm
    d_att = ATT_HEADS * ATT_HEAD_DIM
    ct, st = _rope_lane_tables(length, d_att)
    wukv = p["w_ukv"].reshape(kvr, ATT_HEADS, 2, ATT_HEAD_DIM)
    wuk = wukv[:, :, 0, :].reshape(kvr, d_att).astype(BF16)
    wuvt = wukv[:, :, 1, :].reshape(kvr, d_att).T.astype(BF16)
    kig = jnp.concatenate([p["kidx_norm"], jnp.zeros((LANE - IDX_DIM,), F32)]).reshape(1, LANE)
    row = lambda c: pl.BlockSpec((tm, c), lambda b, i: (b * nt + i, 0))
    tab = pl.BlockSpec((tm, d_att), lambda b, i: (i, 0))
    tab_t = pl.BlockSpec((d_att, tm), lambda b, i: (0, i))
    nqb = tm // Q_BLOCK
    hq = IDX_HEADS * Q_BLOCK
    return pl.pallas_call(
        functools.partial(_prep_body, tm=tm),
        grid=(bsz, nt),
        in_specs=[row(qr), row(kvr), row(LANE), tab, tab, tab_t, tab_t, _const_spec((1, qr)),
                  _const_spec((1, kvr)), _const_spec((d_att, qr)), _const_spec((kvr, d_att)),
                  _const_spec((d_att, kvr)), _const_spec((d_att, 1)), _const_spec((1, d_att)),
                  _const_spec((IDX_HEADS * IDX_DIM, qr)), _const_spec((1, LANE))],
        out_specs=[row(d_att), row(LANE),
                   pl.BlockSpec((None, d_att + V_ONES_ROWS, tm), lambda b, i: (b * nt + i, 0, 0)),
                   pl.BlockSpec((None, d_att, tm), lambda b, i: (b * nt + i, 0, 0)),
                   pl.BlockSpec((nqb, LANE, hq), lambda b, i: (b * nt + i, 0, 0)),
                   pl.BlockSpec((nqb, SUBLANE, hq), lambda b, i: (b * nt + i, 0, 0))],
        out_shape=[jax.ShapeDtypeStruct((t, d_att), BF16),
                   jax.ShapeDtypeStruct((t, LANE), BF16),
                   jax.ShapeDtypeStruct((bsz * nt, d_att + V_ONES_ROWS, tm), BF16),
                   jax.ShapeDtypeStruct((bsz * nt, d_att, tm), BF16),
                   jax.ShapeDtypeStruct((t // Q_BLOCK, LANE, hq), BF16),
                   jax.ShapeDtypeStruct((t // Q_BLOCK, SUBLANE, hq), F32)],
        compiler_params=_cparams("parallel", "parallel"),
        name="dsa_prep",
    )(cq, ckv, misc, ct, st, ct.T, st.T, p["cq_norm"].reshape(1, qr), p["ckv_norm"].reshape(1, kvr),
      p["w_uq"].T.astype(BF16), wuk, wuvt, jnp.tile(p["q_norm"], ATT_HEADS).reshape(d_att, 1),
      jnp.tile(p["k_norm"], ATT_HEADS).reshape(1, d_att), p["w_qidx"].T.astype(BF16), kig)


INT_MIN = -2 ** 31
F32_TINY = 2.0 ** -126
F32_LOWEST = -3.4028234663852886e38
SEARCH_VALUE_STEPS = 16
SEARCH_CAP = 80


def _order_key_to_float(u):
    sk = u ^ jnp.int32(INT_MIN)
    fb = sk ^ ((sk >> 31) & jnp.int32(0x7FFFFFFF))
    return lax.bitcast_convert_type(fb, F32)


def _float_to_order_key(f):
    b = lax.bitcast_convert_type(f, jnp.int32)
    return (b ^ ((b >> 31) & jnp.int32(0x7FFFFFFF))) ^ jnp.int32(INT_MIN)


def _fold_rows(x, op):
    parts = [x[i * SUBLANE:(i + 1) * SUBLANE, :] for i in range(x.shape[0] // SUBLANE)]
    while len(parts) > 1:
        parts = [op(parts[i], parts[i + 1]) for i in range(0, len(parts), 2)]
    return parts[0]


def _dsa_body(qit_ref, wc_ref, qt_ref, ki_ref, k_ref, vt_ref, o_ref,
              sc_scr, lg_scr, qbd_scr, oacc_scr, *, n_sel):
    kc = KEY_CHUNK
    qb = pl.program_id(1)
    nk = qb // (kc // Q_BLOCK) + 1
    qpos = qb * Q_BLOCK + lax.broadcasted_iota(jnp.int32, (1, Q_BLOCK), 1)
    hq = IDX_HEADS * Q_BLOCK
    k_f = jnp.float32(n_sel)

    def chunk(c):
        return pl.ds(pl.multiple_of(c * kc, kc), kc)

    def over_chunks(body, carry):
        def run(first, count, cr):
            for j in range(count):
                cr = body(first + j, cr)
            return cr
        carry = lax.fori_loop(0, nk // 4, lambda i, cr: run(4 * i, 4, cr), carry)
        done = (nk // 4) * 4
        carry = lax.cond(nk % 4 >= 2, lambda cr: run(done, 2, cr), lambda cr: cr, carry)
        return lax.cond(nk % 2 == 1, lambda cr: run(nk - 1, 1, cr), lambda cr: cr, carry)

    qit = qit_ref[...]
    wc = wc_ref[0:1, :]

    def score_chunk(c, carry):
        amax, c_pos, c_nn = carry
        s4 = jnp.dot(ki_ref[chunk(c), :], qit, preferred_element_type=F32)
        s4 = jnp.maximum(s4, 0.0) * wc
        s = (s4[:, 0:Q_BLOCK] + s4[:, Q_BLOCK:2 * Q_BLOCK]) + (s4[:, 2 * Q_BLOCK:3 * Q_BLOCK] + s4[:, 3 * Q_BLOCK:])
        kpos = c * kc + lax.broadcasted_iota(jnp.int32, (kc, Q_BLOCK), 0)
        sm = jnp.where(kpos <= qpos, s, -jnp.inf)
        sc_scr[chunk(c), :] = sm
        return (jnp.maximum(amax, _fold_rows(jnp.abs(s), jnp.maximum)),
                c_pos + _fold_rows(jnp.where(sm >= F32_TINY, 1.0, 0.0), jnp.add),
                c_nn + _fold_rows(jnp.where(sm >= 0.0, 1.0, 0.0), jnp.add))

    zeros8 = jnp.zeros((SUBLANE, Q_BLOCK), F32)
    amax, c_pos, c_nn = over_chunks(score_chunk, (zeros8, zeros8, zeros8))
    amax = jnp.max(amax, axis=0, keepdims=True)
    c_pos = jnp.sum(c_pos, axis=0, keepdims=True)
    c_nn = jnp.sum(c_nn, axis=0, keepdims=True)

    def count_ge(*thrs):
        def body(c, accs):
            blk = sc_scr[chunk(c), :]
            return tuple(a + _fold_rows(jnp.where(blk >= t, 1.0, 0.0), jnp.add) for a, t in zip(accs, thrs))
        accs = over_chunks(body, tuple(jnp.zeros((SUBLANE, Q_BLOCK), F32) for _ in thrs))
        return tuple(jnp.sum(a, axis=0, keepdims=True) for a in accs)

    n_valid = (qpos + 1).astype(F32)
    take_all = n_valid <= k_f
    zero_tie = (c_pos < k_f) & (c_nn >= k_f)
    pos = c_pos >= k_f
    hi_top = _order_key_to_float(_float_to_order_key(amax) + 1)
    lo = jnp.where(take_all, F32_LOWEST, jnp.where(zero_tie, 0.0, jnp.where(pos, F32_TINY, -amax)))
    hi = jnp.where(take_all, jnp.inf, jnp.where(zero_tie, F32_TINY, jnp.where(pos, hi_top, 0.0)))
    cnt_lo = jnp.where(take_all | (~zero_tie & ~pos), n_valid, jnp.where(zero_tie, c_nn, c_pos))
    cnt_hi = jnp.where(take_all | (~zero_tie & pos), 0.0, jnp.where(zero_tie, c_pos, c_nn))
    settled = take_all | zero_tie

    def search_step(st, on_values):
        lk, hk, lo, hi, cnt_lo, cnt_hi = st
        midk = lk + ((hk - lk) >> 1)
        mid = _order_key_to_float(midk)
        if on_values:
            vmid = lo + (hi - lo) * 0.5
            inside = (vmid > lo) & (vmid < hi)
            mid = jnp.where(inside, vmid, mid)
            midk = jnp.where(inside, _float_to_order_key(vmid), midk)
        cnt, = count_ge(mid)
        ge = cnt >= k_f
        return (jnp.where(ge, midk, lk), jnp.where(ge, hk, midk), jnp.where(ge, mid, lo), jnp.where(ge, hi, mid),
                jnp.where(ge, cnt, cnt_lo), jnp.where(ge, cnt_hi, cnt))

    def search_cond(carry):
        it, (lk, hk, _, _, cnt_lo, _) = carry
        open_ = jnp.logical_not(settled | (cnt_lo == k_f) | (hk - lk <= 1))
        return (it < SEARCH_CAP) & (jnp.sum(jnp.where(open_, 1.0, 0.0)) > 0.0)

    def search_pair(carry):
        it, st = carry
        return it + 1, search_step(search_step(st, False), False)

    st = (_float_to_order_key(lo), _float_to_order_key(hi), lo, hi, cnt_lo, cnt_hi)
    st = lax.fori_loop(0, SEARCH_VALUE_STEPS, lambda _, s: search_step(s, True), st)
    _, (_, _, lo, hi, cnt_lo, cnt_hi) = lax.while_loop(search_cond, search_pair, (jnp.int32(0), st))
    need = k_f - cnt_hi

    sub = Q_BLOCK
    ri = lax.broadcasted_iota(jnp.int32, (sub, sub), 0)
    ci = lax.broadcasted_iota(jnp.int32, (sub, sub), 1)
    before = (ri > ci).astype(BF16)

    qbd_scr[...] = jnp.zeros(qbd_scr.shape, BF16)
    for h in range(ATT_HEADS):
        rows = slice(h * ATT_HEAD_DIM, (h + 1) * ATT_HEAD_DIM)
        qbd_scr[rows, h * Q_BLOCK:(h + 1) * Q_BLOCK] = qt_ref[rows, :]
    qbd = qbd_scr[...]

    def logit_chunk(c, carry):
        m, seen = carry
        for j in range(kc // sub):
            rows = pl.ds(pl.multiple_of(c * kc + j * sub, sub), sub)
            blk = sc_scr[rows, :]
            in_hi = blk >= hi
            tie = (blk >= lo) & jnp.logical_not(in_hi)
            tie_f = jnp.where(tie, 1.0, 0.0)
            rank = jnp.dot(before, tie_f.astype(BF16), preferred_element_type=F32) + seen
            sel = in_hi | (tie & (rank < need))
            seen = seen + jnp.sum(tie_f, axis=0, keepdims=True)
            b = jnp.where(sel, 0.0, -jnp.inf)
            lg = jnp.dot(k_ref[rows, :], qbd, preferred_element_type=F32) + jnp.concatenate([b] * ATT_HEADS, axis=1)
            lg_scr[rows, :] = lg
            m = jnp.maximum(m, jnp.max(lg, axis=0, keepdims=True))
        return m, seen

    m, _ = over_chunks(logit_chunk, (jnp.full((1, hq), -jnp.inf, F32), jnp.zeros((1, Q_BLOCK), F32)))
    oacc_scr[...] = jnp.zeros(oacc_scr.shape, F32)

    def value_chunk(c, carry):
        acc = oacc_scr[...]
        depth = 2 * Q_BLOCK
        for j in range(kc // depth):
            rows = pl.ds(pl.multiple_of(c * kc + j * depth, depth), depth)
            p = jnp.exp2(lg_scr[rows, :] - m).astype(BF16)
            acc = acc + jnp.dot(vt_ref[c, :, j * depth:(j + 1) * depth], p, preferred_element_type=F32)
        oacc_scr[...] = acc
        return carry

    over_chunks(value_chunk, 0)
    d_att = ATT_HEADS * ATT_HEAD_DIM
    outs = []
    for h in range(ATT_HEADS):
        cols = slice(h * Q_BLOCK, (h + 1) * Q_BLOCK)
        ssum = oacc_scr[d_att:d_att + 1, cols]
        outs.append(oacc_scr[h * ATT_HEAD_DIM:(h + 1) * ATT_HEAD_DIM, cols] / ssum)
    o_ref[...] = jnp.concatenate(outs, axis=0).T


def _dsa(k, ki, vt, qt, qit, wc, *, bsz):
    t, d_att = k.shape
    length = t // bsz
    nb = length // Q_BLOCK
    kc = KEY_CHUNK
    nkc = length // kc
    hq = IDX_HEADS * Q_BLOCK
    per_q = kc // Q_BLOCK
    v_rows = d_att + V_ONES_ROWS
    vt = vt.reshape(bsz, nkc, v_rows, kc)
    return pl.pallas_call(
        functools.partial(_dsa_body, n_sel=min(TOP_K, length // 4)),
        grid=(bsz, nb),
        in_specs=[pl.BlockSpec((None, LANE, hq), lambda b, i: (b * nb + i, 0, 0)),
                  pl.BlockSpec((None, SUBLANE, hq), lambda b, i: (b * nb + i, 0, 0)),
                  pl.BlockSpec((None, d_att, Q_BLOCK), lambda b, i: (b * nkc + i // per_q, 0, i % per_q)),
                  pl.BlockSpec((length, LANE), lambda b, i: (b, 0)),
                  pl.BlockSpec((length, d_att), lambda b, i: (b, 0)),
                  pl.BlockSpec((None, nkc, v_rows, kc), lambda b, i: (b, 0, 0, 0))],
        out_specs=pl.BlockSpec((Q_BLOCK, d_att), lambda b, i: (b * nb + i, 0)),
        out_shape=jax.ShapeDtypeStruct((t, d_att), F32),
        scratch_shapes=[pltpu.VMEM((length, Q_BLOCK), F32), pltpu.VMEM((length, hq), F32),
                        pltpu.VMEM((d_att, hq), BF16), pltpu.VMEM((v_rows, hq), F32)],
        compiler_params=_cparams("parallel", "arbitrary"),
        name="dsa",
    )(qit, wc, qt, ki, k, vt)


def _pad_cols(w, width):
    return jnp.concatenate([w, jnp.zeros((w.shape[0], width - w.shape[1]), w.dtype)], axis=1)


def kernel(x, norm_ffn1, ffn1_w13, ffn1_w2, norm_mix, w_in, ssd_conv_w, ssd_conv_b, ssd_dt_bias, ssd_a_log, ssd_d, ssd_norm, cq_norm, ckv_norm, w_uq, w_ukv, q_norm, k_norm, w_qidx, kidx_norm, lru_conv_w, lru_conv_b, lru_wa, lru_ba, lru_wi, lru_bi, lru_lambda, w_out, norm_ffn2, ffn2_w13, ffn2_w2):
    bsz, length, d = x.shape
    depth = w_in.shape[0]
    d_ssd = ssd_norm.shape[1]
    conv_ch = ssd_conv_b.shape[1]
    heads = ssd_d.shape[1]
    q_rank, kv_rank = cq_norm.shape[1], ckv_norm.shape[1]
    d_lru = lru_lambda.shape[1]
    sizes = [d_ssd, conv_ch, heads, q_rank, kv_rank, IDX_DIM, IDX_HEADS, d_lru, d_lru]
    offs = [0]
    for s in sizes:
        offs.append(offs[-1] + s)
    widths = (d_ssd, conv_ch, LANE, q_rank, kv_rank, LANE, d_lru, d_lru)
    zero_rows = lambda n: jnp.zeros((n, d), BF16)

    xt = x.reshape(bsz * length, d)
    w13_1, w2_1, w13_2, w2_2 = (w.astype(BF16) for w in (ffn1_w13, ffn1_w2, ffn2_w13, ffn2_w2))
    for l in range(depth):
        wt = w_in[l].T.astype(BF16)
        w_pad = jnp.concatenate(
            [wt[offs[0]:offs[3]], zero_rows(LANE - heads), wt[offs[3]:offs[7]],
             zero_rows(LANE - IDX_DIM - IDX_HEADS), wt[offs[7]:offs[9]]], axis=0)
        xt, z, xbc, dtp, cq, ckv, misc, xl, gl = _ffn(xt, norm_ffn1[l], w13_1, w2_1, l,
                                                      proj=(norm_mix[l], w_pad, widths))
        y_ssd = _ssd(z, xbc, dtp, ssd_conv_w[l], ssd_conv_b[l], ssd_dt_bias[l], ssd_a_log[l], ssd_d[l],
                     ssd_norm[l], bsz=bsz)
        y_lru = _lru(xl, gl, lru_conv_w[l], lru_conv_b[l], lru_wa[l], lru_ba[l], lru_wi[l], lru_bi[l],
                     lru_lambda[l], bsz=bsz)
        prm = dict(cq_norm=cq_norm[l], ckv_norm=ckv_norm[l], w_uq=w_uq[l], w_ukv=w_ukv[l], q_norm=q_norm[l],
                   k_norm=k_norm[l], w_qidx=w_qidx[l], kidx_norm=kidx_norm[l])
        k, ki, vt, qt, qit, wc = _dsa_prep(cq, ckv, misc, prm, bsz=bsz)
        y_att = _dsa(k, ki, vt, qt, qit, wc, bsz=bsz)
        xt = _ffn(xt, norm_ffn2[l], w13_2, w2_2, l, mixed=([y_ssd, y_att, y_lru], w_out[l].astype(BF16)))
    return xt.reshape(bsz, length, d)
```

```python
import functools

import jax
import jax.numpy as jnp
from jax import lax
from jax.experimental import pallas as pl
from jax.experimental.pallas import tpu as pltpu

F32 = jnp.float32
BF16 = jnp.bfloat16

RMS_EPS = 1e-6
LOG2_E = 1.4426950408889634
SSD_HEAD_DIM = 64
SSD_GROUPS = 2
SSD_STATE = 128
SSD_CHUNK = 128
ATT_HEADS = 4
ATT_HEAD_DIM = 64
IDX_HEADS = 4
IDX_DIM = 64
TOP_K = 256
Q_BLOCK = 128
ROPE_THETA = 500000.0
ROPE_ROT = 16
LRU_C = 8.0
CONV_W = 4
LANE = 128
SUBLANE = 8
KEY_CHUNK = 512
V_ONES_ROWS = 16
VMEM_LIMIT = 56 * 1024 * 1024


def _cparams(*sem):
    return pltpu.CompilerParams(dimension_semantics=sem, vmem_limit_bytes=VMEM_LIMIT)


def _const_spec(shape):
    nd = len(shape)
    return pl.BlockSpec(shape, lambda *_: (0,) * nd, pipeline_mode=pl.Buffered(1))


def _rms(x, g):
    ms = jnp.mean(x * x, axis=-1, keepdims=True)
    return x * lax.rsqrt(ms + RMS_EPS) * g


def _softplus(x):
    return jnp.maximum(x, 0.0) + jnp.log1p(jnp.exp(-jnp.abs(x)))


def _silu(x):
    return x * jax.nn.sigmoid(x)


def _split3(x):
    hi = x.astype(BF16)
    r = x - hi.astype(F32)
    mid = r.astype(BF16)
    return hi, mid, (r - mid.astype(F32)).astype(BF16)


def _dot_sel_rhs(x, sel):
    sel = sel.astype(BF16)
    return sum(jnp.dot(p, sel, preferred_element_type=F32) for p in _split3(x))


def _dot_sel_lhs(sel, x):
    sel = sel.astype(BF16)
    return sum(jnp.dot(sel, p, preferred_element_type=F32) for p in _split3(x))


def _ffn_body(*refs, d_ff, tf, n_mix, proj_widths):
    refs = list(refs)
    x = refs.pop(0)[...]
    if n_mix:
        y_refs = [refs.pop(0) for _ in range(n_mix)]
        wo_ref = refs.pop(0)
        off = 0
        for y_ref in y_refs:
            wd = y_ref.shape[1]
            x = x + jnp.dot(y_ref[...].astype(BF16), wo_ref[off:off + wd, :], preferred_element_type=F32)
            off += wd
    g_ref, w13_ref, w2_ref = refs.pop(0), refs.pop(0), refs.pop(0)
    if proj_widths:
        gp_ref, wi_ref = refs.pop(0), refs.pop(0)
    o_ref = refs.pop(0)

    h = _rms(x, g_ref[...]).astype(BF16)
    acc = jnp.zeros(x.shape, F32)
    for j in range(d_ff // tf):
        g = jnp.dot(h, w13_ref[:, j * tf:(j + 1) * tf], preferred_element_type=F32)
        u = jnp.dot(h, w13_ref[:, d_ff + j * tf:d_ff + (j + 1) * tf], preferred_element_type=F32)
        a = (_silu(g) * u).astype(BF16)
        acc = acc + jnp.dot(a, w2_ref[j * tf:(j + 1) * tf, :], preferred_element_type=F32)
    x = x + 0.5 * acc
    o_ref[...] = x

    if proj_widths:
        hp = _rms(x, gp_ref[...]).astype(BF16)
        off = 0
        for p_ref, wd in zip(refs, proj_widths):
            p_ref[...] = _dot_nt(hp, wi_ref[off:off + wd, :])
            off += wd


def _ffn(x, g, w13, w2, layer, *, mixed=None, proj=None, tm=512, tf=256):
    t, d = x.shape
    d_ff = w2.shape[1]
    row = lambda c: pl.BlockSpec((tm, c), lambda i: (i, 0))
    layer_spec = lambda r, c: pl.BlockSpec((None, r, c), lambda i: (layer, 0, 0), pipeline_mode=pl.Buffered(1))
    args, in_specs = [x], [row(d)]
    if mixed is not None:
        ys, w_out = mixed
        args += [*ys, w_out]
        in_specs += [row(y.shape[1]) for y in ys] + [_const_spec(w_out.shape)]
    args += [g.reshape(1, d), w13, w2]
    in_specs += [_const_spec((1, d)), layer_spec(d, 2 * d_ff), layer_spec(d_ff, d)]
    out_specs, out_shape, widths = [row(d)], [jax.ShapeDtypeStruct((t, d), F32)], ()
    if proj is not None:
        gp, w_in, widths = proj
        args += [gp.reshape(1, d), w_in]
        in_specs += [_const_spec((1, d)), _const_spec(w_in.shape)]
        out_specs += [row(wd) for wd in widths]
        out_shape += [jax.ShapeDtypeStruct((t, wd), F32) for wd in widths]
    outs = pl.pallas_call(
        functools.partial(_ffn_body, d_ff=d_ff, tf=tf, n_mix=len(mixed[0]) if mixed is not None else 0,
                          proj_widths=tuple(widths)),
        grid=(t // tm,),
        in_specs=in_specs,
        out_specs=out_specs,
        out_shape=out_shape,
        compiler_params=_cparams("parallel"),
        name="ffn",
    )(*args)
    return outs if proj is not None else outs[0]


def _causal_conv(x_ref, cw_ref, cb_ref, tail_scr, step, rows):
    @pl.when(step == 0)
    def _():
        tail_scr[...] = jnp.zeros(tail_scr.shape, F32)

    x = x_ref[...]
    tail = tail_scr[...]
    row = lax.broadcasted_iota(jnp.int32, tail.shape, 0)
    y = cb_ref[...] + cw_ref[CONV_W - 1:CONV_W, :] * x
    for w in range(CONV_W - 1):
        back = CONV_W - 1 - w
        xr = pltpu.roll(x, back, 0)
        top = jnp.where(row < back, pltpu.roll(tail, back, 0), xr[0:SUBLANE, :])
        y = y + cw_ref[w:w + 1, :] * jnp.concatenate([top, xr[SUBLANE:, :]], axis=0)
    tail_scr[...] = x[rows - SUBLANE:rows, :]
    return y


def _ssd_body(z_ref, xbc_ref, dtp_ref, cw_ref, cb_ref, dtb_ref, alog_ref, dsk_ref, ng_ref,
              y_ref, tail_scr, st_scr, *, q, d_ssd):
    step = pl.program_id(1)
    n = SSD_STATE
    gw = d_ssd // SSD_GROUPS
    rpg = gw // SSD_HEAD_DIM

    @pl.when(step == 0)
    def _():
        st_scr[...] = jnp.zeros(st_scr.shape, F32)

    xbc = _silu(_causal_conv(xbc_ref, cw_ref, cb_ref, tail_scr, step, q))
    xs = xbc[:, :d_ssd]
    bm = xbc[:, d_ssd:d_ssd + SSD_GROUPS * n]
    cm = xbc[:, d_ssd + SSD_GROUPS * n:]

    dt = _softplus(dtp_ref[...] + dtb_ref[...])
    adt = dt * (-jnp.exp(alog_ref[...]))
    ri = lax.broadcasted_iota(jnp.int32, (q, q), 0)
    ci = lax.broadcasted_iota(jnp.int32, (q, q), 1)
    causal = ri >= ci
    acum = _dot_sel_lhs(causal.astype(F32), adt)
    er = lax.broadcasted_iota(jnp.int32, (LANE, d_ssd), 0)
    ec = lax.broadcasted_iota(jnp.int32, (LANE, d_ssd), 1)
    expand = (ec // SSD_HEAD_DIM == er).astype(F32)
    a_x = _dot_sel_rhs(acum, expand)
    dt_x = _dot_sel_rhs(dt, expand)
    acum_t = acum.T
    a_last = a_x[q - 1:q, :]
    exp_a = jnp.exp(a_x)
    xdt = xs * dt_x
    xsw = xdt * jnp.exp(a_last - a_x)
    cdec = jnp.exp(a_last)

    ys = []
    for g in range(SSD_GROUPS):
        bm_g = bm[:, g * n:(g + 1) * n]
        cm_g = cm[:, g * n:(g + 1) * n].astype(BF16)
        cb = lax.dot_general(cm_g, bm_g.astype(BF16), (((1,), (1,)), ((), ())), preferred_element_type=F32)
        yd = []
        for r in range(rpg):
            h = g * rpg + r
            seg = acum[:, h:h + 1] - acum_t[h:h + 1, :]
            decay = jnp.exp(jnp.where(causal, seg, -jnp.inf))
            m = (cb * decay).astype(BF16)
            xh = xdt[:, h * SSD_HEAD_DIM:(h + 1) * SSD_HEAD_DIM].astype(BF16)
            yd.append(jnp.dot(m, xh, preferred_element_type=F32))
        yd = jnp.concatenate(yd, axis=1)
        gs = slice(g * gw, (g + 1) * gw)
        prev = st_scr[g]
        yoff = jnp.dot(cm_g, prev.astype(BF16), preferred_element_type=F32) * exp_a[:, gs]
        st = jnp.dot(bm_g.T.astype(BF16), xsw[:, gs].astype(BF16), preferred_element_type=F32)
        st_scr[g] = prev * cdec[:, gs] + st
        ys.append(yd + yoff + dsk_ref[:, gs] * xs[:, gs])

    z = z_ref[...]
    for g in range(SSD_GROUPS):
        gs = slice(g * gw, (g + 1) * gw)
        y_ref[:, gs] = _rms(ys[g] * _silu(z[:, gs]), ng_ref[:, gs])


def _ssd(z, xbc, dtp, conv_w, conv_b, dt_bias, a_log, d_skip, norm_g, *, bsz):
    t, d_ssd = z.shape
    cch = xbc.shape[1]
    q = SSD_CHUNK
    nc = t // bsz // q
    heads = d_ssd // SSD_HEAD_DIM
    pad = lambda v, fill: jnp.concatenate([v, jnp.full((LANE - heads,), fill, F32)]).reshape(1, LANE)
    row = lambda c: pl.BlockSpec((q, c), lambda b, i: (b * nc + i, 0))
    return pl.pallas_call(
        functools.partial(_ssd_body, q=q, d_ssd=d_ssd),
        grid=(bsz, nc),
        in_specs=[row(d_ssd), row(cch), row(LANE), _const_spec((CONV_W, cch)), _const_spec((1, cch)),
                  _const_spec((1, LANE)), _const_spec((1, LANE)), _const_spec((1, d_ssd)),
                  _const_spec((1, d_ssd))],
        out_specs=row(d_ssd),
        out_shape=jax.ShapeDtypeStruct((t, d_ssd), F32),
        scratch_shapes=[pltpu.VMEM((SUBLANE, cch), F32),
                        pltpu.VMEM((SSD_GROUPS, SSD_STATE, d_ssd // SSD_GROUPS), F32)],
        compiler_params=_cparams("parallel", "arbitrary"),
        name="ssd",
    )(z, xbc, dtp, conv_w, conv_b.reshape(1, cch), pad(dt_bias, 0.0), pad(a_log, 0.0),
      jnp.repeat(d_skip, SSD_HEAD_DIM).reshape(1, d_ssd), norm_g.reshape(1, d_ssd))


def _lru_body(xl_ref, gl_ref, cw_ref, cb_ref, wa_ref, ba_ref, wi_ref, bi_ref, lam_ref,
              y_ref, tail_scr, h_scr, *, tl):
    step = pl.program_id(1)

    @pl.when(step == 0)
    def _():
        h_scr[...] = jnp.zeros(h_scr.shape, F32)

    xr = _causal_conv(xl_ref, cw_ref, cb_ref, tail_scr, step, tl)
    xrb = xr.astype(BF16)
    r = jax.nn.sigmoid(jnp.dot(xrb, wa_ref[...], preferred_element_type=F32) + ba_ref[...])
    i = jax.nn.sigmoid(jnp.dot(xrb, wi_ref[...], preferred_element_type=F32) + bi_ref[...])
    log_a = -LRU_C * r * _softplus(-lam_ref[...])
    a = jnp.exp(log_a)
    b = jnp.sqrt(-jnp.tanh(log_a) * (a * a + 1.0)) * (i * xr)
    rows = lax.broadcasted_iota(jnp.int32, a.shape, 0) % SUBLANE
    s = 1
    while s < SUBLANE:
        keep = rows >= s
        a_sh = jnp.where(keep, pltpu.roll(a, s, 0), 1.0)
        b_sh = jnp.where(keep, pltpu.roll(b, s, 0), 0.0)
        b = a * b_sh + b
        a = a * a_sh
        s *= 2
    h = h_scr[0:1, :]
    gate = jax.nn.gelu(gl_ref[...])
    for g in range(tl // SUBLANE):
        grp = slice(g * SUBLANE, (g + 1) * SUBLANE)
        hs = a[grp, :] * h + b[grp, :]
        y_ref[grp, :] = hs * gate[grp, :]
        h = hs[SUBLANE - 1:SUBLANE, :]
    h_scr[...] = jnp.broadcast_to(h, h_scr.shape)


def _block_diag(w):
    nb, bw, _ = w.shape
    out = jnp.zeros((nb * bw, nb * bw), w.dtype)
    for k in range(nb):
        out = out.at[k * bw:(k + 1) * bw, k * bw:(k + 1) * bw].set(w[k])
    return out


def _lru(xl, gl, conv_w, conv_b, wa, ba, wi, bi, lam, *, bsz, tl=256):
    t, dl = xl.shape
    nt = t // bsz // tl
    row = pl.BlockSpec((tl, dl), lambda b, i: (b * nt + i, 0))
    vec = _const_spec((1, dl))
    return pl.pallas_call(
        functools.partial(_lru_body, tl=tl),
        grid=(bsz, nt),
        in_specs=[row, row, _const_spec((CONV_W, dl)), vec, _const_spec((dl, dl)), vec,
                  _const_spec((dl, dl)), vec, vec],
        out_specs=row,
        out_shape=jax.ShapeDtypeStruct((t, dl), F32),
        scratch_shapes=[pltpu.VMEM((SUBLANE, dl), F32), pltpu.VMEM((SUBLANE, dl), F32)],
        compiler_params=_cparams("parallel", "arbitrary"),
        name="rglru",
    )(xl, gl, conv_w, conv_b.reshape(1, dl), _block_diag(wa).astype(BF16), ba.reshape(1, dl),
      _block_diag(wi).astype(BF16), bi.reshape(1, dl), lam.reshape(1, dl))


def _rope(x, ct, st, axis):
    width = x.shape[axis]
    fm = lax.broadcasted_iota(jnp.int32, x.shape, axis) % ATT_HEAD_DIM
    half = ROPE_ROT // 2
    partner = jnp.where(fm < half, pltpu.roll(x, width - half, axis), pltpu.roll(x, half, axis))
    return x * ct + partner * st


def _dot_nt(a, b):
    return lax.dot_general(a, b, (((1,), (1,)), ((), ())), preferred_element_type=F32)


def _prep_body(cq_ref, ckv_ref, misc_ref, ct_ref, st_ref, ctt_ref, stt_ref, cqg_ref, ckvg_ref, wuqt_ref,
               wuk_ref, wuvt_ref, qg_ref, kg_ref, wqit_ref, kig_ref,
               k_ref, ki_ref, vt_ref, qt_ref, qit_ref, wc_ref, *, tm):
    ct = ct_ref[...]
    st = st_ref[...]
    d_att = wuk_ref.shape[1]
    gr = lax.broadcasted_iota(jnp.int32, (d_att, d_att), 0) // ATT_HEAD_DIM
    gc = lax.broadcasted_iota(jnp.int32, (d_att, d_att), 1) // ATT_HEAD_DIM
    head_mean = jnp.where(gr == gc, 1.0 / ATT_HEAD_DIM, 0.0).astype(F32)

    cqn = _rms(cq_ref[...], cqg_ref[...]).astype(BF16)
    ckvn = _rms(ckv_ref[...], ckvg_ref[...]).astype(BF16)

    qt = _dot_nt(wuqt_ref[...], cqn)
    ms = _dot_sel_lhs(head_mean, qt * qt)
    qt = qt * lax.rsqrt(ms + RMS_EPS) * qg_ref[...]
    qt_ref[...] = (_rope(qt, ctt_ref[...], stt_ref[...], 0) * (ATT_HEAD_DIM ** -0.5 * LOG2_E)).astype(BF16)
    vt_ref[0:d_att, :] = _dot_nt(wuvt_ref[...], ckvn).astype(BF16)
    vt_ref[d_att:, :] = jnp.ones((vt_ref.shape[0] - d_att, tm), BF16)

    k = jnp.dot(ckvn, wuk_ref[...], preferred_element_type=F32)
    ms = _dot_sel_rhs(k * k, head_mean)
    k_ref[...] = _rope(k * lax.rsqrt(ms + RMS_EPS) * kg_ref[...], ct, st, 1).astype(BF16)

    qit = _rope(_dot_nt(wqit_ref[...], cqn), ctt_ref[...], stt_ref[...], 0)
    misc = misc_ref[...]
    misc_t = misc.T
    wscale = IDX_HEADS ** -0.5 * IDX_DIM ** -0.5
    for j in range(tm // Q_BLOCK):
        cols = slice(j * Q_BLOCK, (j + 1) * Q_BLOCK)
        for h in range(IDX_HEADS):
            qit_ref[j, 0:IDX_DIM, h * Q_BLOCK:(h + 1) * Q_BLOCK] = (
                qit[h * IDX_DIM:(h + 1) * IDX_DIM, cols].astype(BF16))
        qit_ref[j, IDX_DIM:, :] = jnp.zeros((LANE - IDX_DIM, IDX_HEADS * Q_BLOCK), BF16)
        wrow = jnp.concatenate([misc_t[IDX_DIM + h:IDX_DIM + h + 1, cols] for h in range(IDX_HEADS)], axis=1)
        wc_ref[j] = jnp.broadcast_to(wrow * wscale, (SUBLANE, IDX_HEADS * Q_BLOCK))

    lane = lax.broadcasted_iota(jnp.int32, misc.shape, 1)
    km = jnp.where(lane < IDX_DIM, misc, 0.0)
    ms = jnp.sum(km * km, axis=-1, keepdims=True) * (1.0 / IDX_DIM)
    kin = km * lax.rsqrt(ms + RMS_EPS) * kig_ref[...]
    ki_ref[...] = _rope(kin, ct[:, :LANE], st[:, :LANE], 1).astype(BF16)


def _rope_lane_tables(length, width):
    half = ROPE_ROT // 2
    inv = ROPE_THETA ** (-jnp.arange(half, dtype=F32) * 2.0 / ROPE_ROT)
    ang = jnp.arange(length, dtype=F32)[:, None] * inv[None, :]
    cos, sin = jnp.cos(ang), jnp.sin(ang)
    ones = jnp.ones((length, ATT_HEAD_DIM - ROPE_ROT), F32)
    ct = jnp.concatenate([cos, cos, ones], axis=1)
    st = jnp.concatenate([-sin, sin, 0.0 * ones], axis=1)
    reps = width // ATT_HEAD_DIM
    return jnp.tile(ct, (1, reps)), jnp.tile(st, (1, reps))


def _dsa_prep(cq, ckv, misc, p, *, bsz):
    t, qr = cq.shape
    kvr = ckv.shape[1]
    tm = KEY_CHUNK
    length = t // bsz
    nt = length // tm
    d_att = ATT_HEADS * ATT_HEAD_DIM
    ct, st = _rope_lane_tables(length, d_att)
    wukv = p["w_ukv"].reshape(kvr, ATT_HEADS, 2, ATT_HEAD_DIM)
    wuk = wukv[:, :, 0, :].reshape(kvr, d_att).astype(BF16)
    wuvt = wukv[:, :, 1, :].reshape(kvr, d_att).T.astype(BF16)
    kig = jnp.concatenate([p["kidx_norm"], jnp.zeros((LANE - IDX_DIM,), F32)]).reshape(1, LANE)
    row = lambda c: pl.BlockSpec((tm, c), lambda b, i: (b * nt + i, 0))
    tab = pl.BlockSpec((tm, d_att), lambda b, i: (i, 0))
    tab_t = pl.BlockSpec((d_att, tm), lambda b, i: (0, i))
    nqb = tm // Q_BLOCK
    hq = IDX_HEADS * Q_BLOCK
    return pl.pallas_call(
        functools.partial(_prep_body, tm=tm),
        grid=(bsz, nt),
        in_specs=[row(qr), row(kvr), row(LANE), tab, tab, tab_t, tab_t, _const_spec((1, qr)),
                  _const_spec((1, kvr)), _const_spec((d_att, qr)), _const_spec((kvr, d_att)),
                  _const_spec((d_att, kvr)), _const_spec((d_att, 1)), _const_spec((1, d_att)),
                  _const_spec((IDX_HEADS * IDX_DIM, qr)), _const_spec((1, LANE))],
        out_specs=[row(d_att), row(LANE),
                   pl.BlockSpec((None, d_att + V_ONES_ROWS, tm), lambda b, i: (b * nt + i, 0, 0)),
                   pl.BlockSpec((None, d_att, tm), lambda b, i: (b * nt + i, 0, 0)),
                   pl.BlockSpec((nqb, LANE, hq), lambda b, i: (b * nt + i, 0, 0)),
                   pl.BlockSpec((nqb, SUBLANE, hq), lambda b, i: (b * nt + i, 0, 0))],
        out_shape=[jax.ShapeDtypeStruct((t, d_att), BF16),
                   jax.ShapeDtypeStruct((t, LANE), BF16),
                   jax.ShapeDtypeStruct((bsz * nt, d_att + V_ONES_ROWS, tm), BF16),
                   jax.ShapeDtypeStruct((bsz * nt, d_att, tm), BF16),
                   jax.ShapeDtypeStruct((t // Q_BLOCK, LANE, hq), BF16),
                   jax.ShapeDtypeStruct((t // Q_BLOCK, SUBLANE, hq), F32)],
        compiler_params=_cparams("parallel", "parallel"),
        name="dsa_prep",
    )(cq, ckv, misc, ct, st, ct.T, st.T, p["cq_norm"].reshape(1, qr), p["ckv_norm"].reshape(1, kvr),
      p["w_uq"].T.astype(BF16), wuk, wuvt, jnp.tile(p["q_norm"], ATT_HEADS).reshape(d_att, 1),
      jnp.tile(p["k_norm"], ATT_HEADS).reshape(1, d_att), p["w_qidx"].T.astype(BF16), kig)


INT_MIN = -2 ** 31
F32_TINY = 2.0 ** -126
F32_LOWEST = -3.4028234663852886e38
SEARCH_VALUE_STEPS = 16
SEARCH_CAP = 80


def _order_key_to_float(u):
    sk = u ^ jnp.int32(INT_MIN)
    fb = sk ^ ((sk >> 31) & jnp.int32(0x7FFFFFFF))
    return lax.bitcast_convert_type(fb, F32)


def _float_to_order_key(f):
    b = lax.bitcast_convert_type(f, jnp.int32)
    return (b ^ ((b >> 31) & jnp.int32(0x7FFFFFFF))) ^ jnp.int32(INT_MIN)


def _fold_rows(x, op):
    parts = [x[i * SUBLANE:(i + 1) * SUBLANE, :] for i in range(x.shape[0] // SUBLANE)]
    while len(parts) > 1:
        parts = [op(parts[i], parts[i + 1]) for i in range(0, len(parts), 2)]
    return parts[0]


def _dsa_body(qit_ref, wc_ref, qt_ref, ki_ref, k_ref, vt_ref, o_ref,
              sc_scr, lg_scr, qbd_scr, oacc_scr, *, n_sel, nk):
    kc = KEY_CHUNK
    qb = (nk - 1) * (kc // Q_BLOCK) + pl.program_id(1)
    qpos = qb * Q_BLOCK + lax.broadcasted_iota(jnp.int32, (1, Q_BLOCK), 1)
    hq = IDX_HEADS * Q_BLOCK
    k_f = jnp.float32(n_sel)

    def chunk(c):
        return pl.ds(c * kc, kc)

    def over_chunks(body, carry):
        for c in range(nk):
            carry = body(c, carry)
        return carry

    qit = qit_ref[...]
    wc = wc_ref[0:1, :]

    def score_chunk(c, carry):
        amax, c_pos, c_nn = carry
        s4 = jnp.dot(ki_ref[chunk(c), :], qit, preferred_element_type=F32)
        s4 = jnp.maximum(s4, 0.0) * wc
        s = (s4[:, 0:Q_BLOCK] + s4[:, Q_BLOCK:2 * Q_BLOCK]) + (s4[:, 2 * Q_BLOCK:3 * Q_BLOCK] + s4[:, 3 * Q_BLOCK:])
        kpos = c * kc + lax.broadcasted_iota(jnp.int32, (kc, Q_BLOCK), 0)
        sm = jnp.where(kpos <= qpos, s, -jnp.inf)
        sc_scr[chunk(c), :] = sm
        return (jnp.maximum(amax, _fold_rows(jnp.abs(s), jnp.maximum)),
                c_pos + _fold_rows(jnp.where(sm >= F32_TINY, 1.0, 0.0), jnp.add),
                c_nn + _fold_rows(jnp.where(sm >= 0.0, 1.0, 0.0), jnp.add))

    zeros8 = jnp.zeros((SUBLANE, Q_BLOCK), F32)
    amax, c_pos, c_nn = over_chunks(score_chunk, (zeros8, zeros8, zeros8))
    amax = jnp.max(amax, axis=0, keepdims=True)
    c_pos = jnp.sum(c_pos, axis=0, keepdims=True)
    c_nn = jnp.sum(c_nn, axis=0, keepdims=True)

    def count_ge(*thrs):
        def body(c, accs):
            blk = sc_scr[chunk(c), :]
            return tuple(a + _fold_rows(jnp.where(blk >= t, 1.0, 0.0), jnp.add) for a, t in zip(accs, thrs))
        accs = over_chunks(body, tuple(jnp.zeros((SUBLANE, Q_BLOCK), F32) for _ in thrs))
        return tuple(jnp.sum(a, axis=0, keepdims=True) for a in accs)

    n_valid = (qpos + 1).astype(F32)
    take_all = n_valid <= k_f
    zero_tie = (c_pos < k_f) & (c_nn >= k_f)
    pos = c_pos >= k_f
    hi_top = _order_key_to_float(_float_to_order_key(amax) + 1)
    lo = jnp.where(take_all, F32_LOWEST, jnp.where(zero_tie, 0.0, jnp.where(pos, F32_TINY, -amax)))
    hi = jnp.where(take_all, jnp.inf, jnp.where(zero_tie, F32_TINY, jnp.where(pos, hi_top, 0.0)))
    cnt_lo = jnp.where(take_all | (~zero_tie & ~pos), n_valid, jnp.where(zero_tie, c_nn, c_pos))
    cnt_hi = jnp.where(take_all | (~zero_tie & pos), 0.0, jnp.where(zero_tie, c_pos, c_nn))
    settled = take_all | zero_tie

    def search_step(st, on_values):
        lk, hk, lo, hi, cnt_lo, cnt_hi = st
        midk = lk + ((hk - lk) >> 1)
        mid = _order_key_to_float(midk)
        if on_values:
            vmid = lo + (hi - lo) * 0.5
            inside = (vmid > lo) & (vmid < hi)
            mid = jnp.where(inside, vmid, mid)
            midk = jnp.where(inside, _float_to_order_key(vmid), midk)
        cnt, = count_ge(mid)
        ge = cnt >= k_f
        return (jnp.where(ge, midk, lk), jnp.where(ge, hk, midk), jnp.where(ge, mid, lo), jnp.where(ge, hi, mid),
                jnp.where(ge, cnt, cnt_lo), jnp.where(ge, cnt_hi, cnt))

    def search_cond(carry):
        it, (lk, hk, _, _, cnt_lo, _) = carry
        open_ = jnp.logical_not(settled | (cnt_lo == k_f) | (hk - lk <= 1))
        return (it < SEARCH_CAP) & (jnp.sum(jnp.where(open_, 1.0, 0.0)) > 0.0)

    def search_pair(carry):
        it, st = carry
        return it + 1, search_step(search_step(st, False), False)

    st = (_float_to_order_key(lo), _float_to_order_key(hi), lo, hi, cnt_lo, cnt_hi)
    st = lax.fori_loop(0, SEARCH_VALUE_STEPS, lambda _, s: search_step(s, True), st)
    _, (_, _, lo, hi, cnt_lo, cnt_hi) = lax.while_loop(search_cond, search_pair, (jnp.int32(0), st))
    need = k_f - cnt_hi

    sub = Q_BLOCK
    ri = lax.broadcasted_iota(jnp.int32, (sub, sub), 0)
    ci = lax.broadcasted_iota(jnp.int32, (sub, sub), 1)
    before = (ri > ci).astype(BF16)

    qbd_scr[...] = jnp.zeros(qbd_scr.shape, BF16)
    for h in range(ATT_HEADS):
        rows = slice(h * ATT_HEAD_DIM, (h + 1) * ATT_HEAD_DIM)
        qbd_scr[rows, h * Q_BLOCK:(h + 1) * Q_BLOCK] = qt_ref[rows, :]
    qbd = qbd_scr[...]

    def logit_chunk(c, carry):
        m, seen = carry
        for j in range(kc // sub):
            rows = pl.ds(c * kc + j * sub, sub)
            blk = sc_scr[rows, :]
            in_hi = blk >= hi
            tie = (blk >= lo) & jnp.logical_not(in_hi)
            tie_f = jnp.where(tie, 1.0, 0.0)
            rank = jnp.dot(before, tie_f.astype(BF16), preferred_element_type=F32) + seen
            sel = in_hi | (tie & (rank < need))
            seen = seen + jnp.sum(tie_f, axis=0, keepdims=True)
            b = jnp.where(sel, 0.0, -jnp.inf)
            lg = jnp.dot(k_ref[rows, :], qbd, preferred_element_type=F32) + jnp.concatenate([b] * ATT_HEADS, axis=1)
            lg_scr[rows, :] = lg
            m = jnp.maximum(m, jnp.max(lg, axis=0, keepdims=True))
        return m, seen

    m, _ = over_chunks(logit_chunk, (jnp.full((1, hq), -jnp.inf, F32), jnp.zeros((1, Q_BLOCK), F32)))
    oacc_scr[...] = jnp.zeros(oacc_scr.shape, F32)

    def value_chunk(c, carry):
        acc = oacc_scr[...]
        depth = 2 * Q_BLOCK
        for j in range(kc // depth):
            rows = pl.ds(c * kc + j * depth, depth)
            p = jnp.exp2(lg_scr[rows, :] - m).astype(BF16)
            acc = acc + jnp.dot(vt_ref[c, :, j * depth:(j + 1) * depth], p, preferred_element_type=F32)
        oacc_scr[...] = acc
        return carry

    over_chunks(value_chunk, 0)
    d_att = ATT_HEADS * ATT_HEAD_DIM
    outs = []
    for h in range(ATT_HEADS):
        cols = slice(h * Q_BLOCK, (h + 1) * Q_BLOCK)
        ssum = oacc_scr[d_att:d_att + 1, cols]
        outs.append(oacc_scr[h * ATT_HEAD_DIM:(h + 1) * ATT_HEAD_DIM, cols] / ssum)
    o_ref[...] = jnp.concatenate(outs, axis=0).T


def _dsa(k, ki, vt, qt, qit, wc, *, bsz):
    t, d_att = k.shape
    length = t // bsz
    nb = length // Q_BLOCK
    kc = KEY_CHUNK
    nkc = length // kc
    hq = IDX_HEADS * Q_BLOCK
    per_q = kc // Q_BLOCK
    v_rows = d_att + V_ONES_ROWS
    vt = vt.reshape(bsz, nkc, v_rows, kc)
    ki = ki.reshape(bsz, length, LANE)
    k = k.reshape(bsz, length, d_att)
    outs = []
    for g in range(nkc):
        nk = g + 1
        qblk = lambda b, i, g=g: b * nb + g * per_q + i
        outs.append(pl.pallas_call(
            functools.partial(_dsa_body, n_sel=min(TOP_K, length // 4), nk=nk),
            grid=(bsz, per_q),
            in_specs=[pl.BlockSpec((None, LANE, hq), lambda b, i, f=qblk: (f(b, i), 0, 0)),
                      pl.BlockSpec((None, SUBLANE, hq), lambda b, i, f=qblk: (f(b, i), 0, 0)),
                      pl.BlockSpec((None, d_att, Q_BLOCK), lambda b, i, g=g: (b * nkc + g, 0, i)),
                      pl.BlockSpec((None, nk * kc, LANE), lambda b, i: (b, 0, 0)),
                      pl.BlockSpec((None, nk * kc, d_att), lambda b, i: (b, 0, 0)),
                      pl.BlockSpec((None, nk, v_rows, kc), lambda b, i: (b, 0, 0, 0))],
            out_specs=pl.BlockSpec((None, Q_BLOCK, d_att), lambda b, i: (b, i, 0)),
            out_shape=jax.ShapeDtypeStruct((bsz, kc, d_att), F32),
            scratch_shapes=[pltpu.VMEM((nk * kc, Q_BLOCK), F32), pltpu.VMEM((nk * kc, hq), F32),
                            pltpu.VMEM((d_att, hq), BF16), pltpu.VMEM((v_rows, hq), F32)],
            compiler_params=_cparams("parallel", "arbitrary"),
            name=f"dsa{nk}",
        )(qit, wc, qt, ki, k, vt))
    return jnp.stack(outs, axis=1).reshape(t, d_att)


def _pad_cols(w, width):
    return jnp.concatenate([w, jnp.zeros((w.shape[0], width - w.shape[1]), w.dtype)], axis=1)


def kernel(x, norm_ffn1, ffn1_w13, ffn1_w2, norm_mix, w_in, ssd_conv_w, ssd_conv_b, ssd_dt_bias, ssd_a_log, ssd_d, ssd_norm, cq_norm, ckv_norm, w_uq, w_ukv, q_norm, k_norm, w_qidx, kidx_norm, lru_conv_w, lru_conv_b, lru_wa, lru_ba, lru_wi, lru_bi, lru_lambda, w_out, norm_ffn2, ffn2_w13, ffn2_w2):
    bsz, length, d = x.shape
    depth = w_in.shape[0]
    d_ssd = ssd_norm.shape[1]
    conv_ch = ssd_conv_b.shape[1]
    heads = ssd_d.shape[1]
    q_rank, kv_rank = cq_norm.shape[1], ckv_norm.shape[1]
    d_lru = lru_lambda.shape[1]
    sizes = [d_ssd, conv_ch, heads, q_rank, kv_rank, IDX_DIM, IDX_HEADS, d_lru, d_lru]
    offs = [0]
    for s in sizes:
        offs.append(offs[-1] + s)
    widths = (d_ssd, conv_ch, LANE, q_rank, kv_rank, LANE, d_lru, d_lru)
    zero_rows = lambda n: jnp.zeros((n, d), BF16)

    xt = x.reshape(bsz * length, d)
    w13_1, w2_1, w13_2, w2_2 = (w.astype(BF16) for w in (ffn1_w13, ffn1_w2, ffn2_w13, ffn2_w2))
    for l in range(depth):
        wt = w_in[l].T.astype(BF16)
        w_pad = jnp.concatenate(
            [wt[offs[0]:offs[3]], zero_rows(LANE - heads), wt[offs[3]:offs[7]],
             zero_rows(LANE - IDX_DIM - IDX_HEADS), wt[offs[7]:offs[9]]], axis=0)
        xt, z, xbc, dtp, cq, ckv, misc, xl, gl = _ffn(xt, norm_ffn1[l], w13_1, w2_1, l,
                                                      proj=(norm_mix[l], w_pad, widths))
        y_ssd = _ssd(z, xbc, dtp, ssd_conv_w[l], ssd_conv_b[l], ssd_dt_bias[l], ssd_a_log[l], ssd_d[l],
                     ssd_norm[l], bsz=bsz)
        y_lru = _lru(xl, gl, lru_conv_w[l], lru_conv_b[l], lru_wa[l], lru_ba[l], lru_wi[l], lru_bi[l],
                     lru_lambda[l], bsz=bsz)
        prm = dict(cq_norm=cq_norm[l], ckv_norm=ckv_norm[l], w_uq=w_uq[l], w_ukv=w_ukv[l], q_norm=q_norm[l],
                   k_norm=k_norm[l], w_qidx=w_qidx[l], kidx_norm=kidx_norm[l])
        k, ki, vt, qt, qit, wc = _dsa_prep(cq, ckv, misc, prm, bsz=bsz)
        y_att = _dsa(k, ki, vt, qt, qit, wc, bsz=bsz)
        xt = _ffn(xt, norm_ffn2[l], w13_2, w2_2, l, mixed=([y_ssd, y_att, y_lru], w_out[l].astype(BF16)))
    return xt.reshape(bsz, length, d)
```

```python
import functools

import jax
import jax.numpy as jnp
from jax import lax
from jax.experimental import pallas as pl
from jax.experimental.pallas import tpu as pltpu

F32 = jnp.float32
BF16 = jnp.bfloat16

RMS_EPS = 1e-6
LOG2_E = 1.4426950408889634
SSD_HEAD_DIM = 64
SSD_GROUPS = 2
SSD_STATE = 128
SSD_CHUNK = 128
ATT_HEADS = 4
ATT_HEAD_DIM = 64
IDX_HEADS = 4
IDX_DIM = 64
TOP_K = 256
Q_BLOCK = 128
ROPE_THETA = 500000.0
ROPE_ROT = 16
LRU_C = 8.0
CONV_W = 4
LANE = 128
SUBLANE = 8
KEY_CHUNK = 512
V_ONES_ROWS = 16
V_PAIR_ROWS = 2 * ATT_HEAD_DIM + V_ONES_ROWS
V_ROWS = (ATT_HEADS // 2) * V_PAIR_ROWS
VMEM_LIMIT = 56 * 1024 * 1024


def _cparams(*sem):
    return pltpu.CompilerParams(dimension_semantics=sem, vmem_limit_bytes=VMEM_LIMIT)


def _const_spec(shape):
    nd = len(shape)
    return pl.BlockSpec(shape, lambda *_: (0,) * nd, pipeline_mode=pl.Buffered(1))


def _rms(x, g):
    ms = jnp.mean(x * x, axis=-1, keepdims=True)
    return x * lax.rsqrt(ms + RMS_EPS) * g


def _softplus(x):
    return jnp.maximum(x, 0.0) + jnp.log1p(jnp.exp(-jnp.abs(x)))


def _silu(x):
    return x * jax.nn.sigmoid(x)


def _split3(x):
    hi = x.astype(BF16)
    r = x - hi.astype(F32)
    mid = r.astype(BF16)
    return hi, mid, (r - mid.astype(F32)).astype(BF16)


def _dot_sel_rhs(x, sel):
    sel = sel.astype(BF16)
    return sum(jnp.dot(p, sel, preferred_element_type=F32) for p in _split3(x))


def _dot_sel_lhs(sel, x):
    sel = sel.astype(BF16)
    return sum(jnp.dot(sel, p, preferred_element_type=F32) for p in _split3(x))


def _ffn_body(*refs, d_ff, tf, n_mix, proj_widths):
    refs = list(refs)
    x = refs.pop(0)[...]
    if n_mix:
        y_refs = [refs.pop(0) for _ in range(n_mix)]
        wo_ref = refs.pop(0)
        off = 0
        for y_ref in y_refs:
            wd = y_ref.shape[1]
            x = x + jnp.dot(y_ref[...].astype(BF16), wo_ref[off:off + wd, :], preferred_element_type=F32)
            off += wd
    g_ref, w13_ref, w2_ref = refs.pop(0), refs.pop(0), refs.pop(0)
    if proj_widths:
        gp_ref, wi_ref = refs.pop(0), refs.pop(0)
    o_ref = refs.pop(0)

    h = _rms(x, g_ref[...]).astype(BF16)
    acc = jnp.zeros(x.shape, F32)
    for j in range(d_ff // tf):
        g = jnp.dot(h, w13_ref[:, j * tf:(j + 1) * tf], preferred_element_type=F32)
        u = jnp.dot(h, w13_ref[:, d_ff + j * tf:d_ff + (j + 1) * tf], preferred_element_type=F32)
        a = (_silu(g) * u).astype(BF16)
        acc = acc + jnp.dot(a, w2_ref[j * tf:(j + 1) * tf, :], preferred_element_type=F32)
    x = x + 0.5 * acc
    o_ref[...] = x

    if proj_widths:
        hp = _rms(x, gp_ref[...]).astype(BF16)
        off = 0
        for p_ref, wd in zip(refs, proj_widths):
            p_ref[...] = _dot_nt(hp, wi_ref[off:off + wd, :])
            off += wd


def _ffn(x, g, w13, w2, layer, *, mixed=None, proj=None, tm=512, tf=256):
    t, d = x.shape
    d_ff = w2.shape[1]
    row = lambda c: pl.BlockSpec((tm, c), lambda i: (i, 0))
    layer_spec = lambda r, c: pl.BlockSpec((None, r, c), lambda i: (layer, 0, 0), pipeline_mode=pl.Buffered(1))
    args, in_specs = [x], [row(d)]
    if mixed is not None:
        ys, w_out = mixed
        args += [*ys, w_out]
        in_specs += [row(y.shape[1]) for y in ys] + [_const_spec(w_out.shape)]
    args += [g.reshape(1, d), w13, w2]
    in_specs += [_const_spec((1, d)), layer_spec(d, 2 * d_ff), layer_spec(d_ff, d)]
    out_specs, out_shape, widths = [row(d)], [jax.ShapeDtypeStruct((t, d), F32)], ()
    if proj is not None:
        gp, w_in, widths = proj
        args += [gp.reshape(1, d), w_in]
        in_specs += [_const_spec((1, d)), _const_spec(w_in.shape)]
        out_specs += [row(wd) for wd in widths]
        out_shape += [jax.ShapeDtypeStruct((t, wd), F32) for wd in widths]
    outs = pl.pallas_call(
        functools.partial(_ffn_body, d_ff=d_ff, tf=tf, n_mix=len(mixed[0]) if mixed is not None else 0,
                          proj_widths=tuple(widths)),
        grid=(t // tm,),
        in_specs=in_specs,
        out_specs=out_specs,
        out_shape=out_shape,
        compiler_params=_cparams("parallel"),
        name="ffn",
    )(*args)
    return outs if proj is not None else outs[0]


def _causal_conv(x_ref, cw_ref, cb_ref, tail_scr, step, rows):
    @pl.when(step == 0)
    def _():
        tail_scr[...] = jnp.zeros(tail_scr.shape, F32)

    x = x_ref[...]
    tail = tail_scr[...]
    row = lax.broadcasted_iota(jnp.int32, tail.shape, 0)
    y = cb_ref[...] + cw_ref[CONV_W - 1:CONV_W, :] * x
    for w in range(CONV_W - 1):
        back = CONV_W - 1 - w
        xr = pltpu.roll(x, back, 0)
        top = jnp.where(row < back, pltpu.roll(tail, back, 0), xr[0:SUBLANE, :])
        y = y + cw_ref[w:w + 1, :] * jnp.concatenate([top, xr[SUBLANE:, :]], axis=0)
    tail_scr[...] = x[rows - SUBLANE:rows, :]
    return y


def _ssd_body(z_ref, xbc_ref, dtp_ref, cw_ref, cb_ref, dtb_ref, alog_ref, dsk_ref, ng_ref,
              y_ref, tail_scr, st_scr, *, q, d_ssd):
    step = pl.program_id(1)
    n = SSD_STATE
    gw = d_ssd // SSD_GROUPS
    rpg = gw // SSD_HEAD_DIM

    @pl.when(step == 0)
    def _():
        st_scr[...] = jnp.zeros(st_scr.shape, F32)

    xbc = _silu(_causal_conv(xbc_ref, cw_ref, cb_ref, tail_scr, step, q))
    xs = xbc[:, :d_ssd]
    bm = xbc[:, d_ssd:d_ssd + SSD_GROUPS * n]
    cm = xbc[:, d_ssd + SSD_GROUPS * n:]

    dt = _softplus(dtp_ref[...] + dtb_ref[...])
    adt = dt * (-jnp.exp(alog_ref[...]))
    ri = lax.broadcasted_iota(jnp.int32, (q, q), 0)
    ci = lax.broadcasted_iota(jnp.int32, (q, q), 1)
    causal = ri >= ci
    acum = _dot_sel_lhs(causal.astype(F32), adt)
    er = lax.broadcasted_iota(jnp.int32, (LANE, d_ssd), 0)
    ec = lax.broadcasted_iota(jnp.int32, (LANE, d_ssd), 1)
    expand = (ec // SSD_HEAD_DIM == er).astype(F32)
    a_x = _dot_sel_rhs(acum, expand)
    dt_x = _dot_sel_rhs(dt, expand)
    acum_t = acum.T
    a_last = a_x[q - 1:q, :]
    exp_a = jnp.exp(a_x)
    xdt = xs * dt_x
    xsw = xdt * jnp.exp(a_last - a_x)
    cdec = jnp.exp(a_last)

    ys = []
    for g in range(SSD_GROUPS):
        bm_g = bm[:, g * n:(g + 1) * n]
        cm_g = cm[:, g * n:(g + 1) * n].astype(BF16)
        cb = lax.dot_general(cm_g, bm_g.astype(BF16), (((1,), (1,)), ((), ())), preferred_element_type=F32)
        yd = []
        for r in range(rpg):
            h = g * rpg + r
            seg = acum[:, h:h + 1] - acum_t[h:h + 1, :]
            decay = jnp.exp(jnp.where(causal, seg, -jnp.inf))
            m = (cb * decay).astype(BF16)
            xh = xdt[:, h * SSD_HEAD_DIM:(h + 1) * SSD_HEAD_DIM].astype(BF16)
            yd.append(jnp.dot(m, xh, preferred_element_type=F32))
        yd = jnp.concatenate(yd, axis=1)
        gs = slice(g * gw, (g + 1) * gw)
        prev = st_scr[g]
        yoff = jnp.dot(cm_g, prev.astype(BF16), preferred_element_type=F32) * exp_a[:, gs]
        st = jnp.dot(bm_g.T.astype(BF16), xsw[:, gs].astype(BF16), preferred_element_type=F32)
        st_scr[g] = prev * cdec[:, gs] + st
        ys.append(yd + yoff + dsk_ref[:, gs] * xs[:, gs])

    z = z_ref[...]
    for g in range(SSD_GROUPS):
        gs = slice(g * gw, (g + 1) * gw)
        y_ref[:, gs] = _rms(ys[g] * _silu(z[:, gs]), ng_ref[:, gs])


def _ssd(z, xbc, dtp, conv_w, conv_b, dt_bias, a_log, d_skip, norm_g, *, bsz):
    t, d_ssd = z.shape
    cch = xbc.shape[1]
    q = SSD_CHUNK
    nc = t // bsz // q
    heads = d_ssd // SSD_HEAD_DIM
    pad = lambda v, fill: jnp.concatenate([v, jnp.full((LANE - heads,), fill, F32)]).reshape(1, LANE)
    row = lambda c: pl.BlockSpec((q, c), lambda b, i: (b * nc + i, 0))
    return pl.pallas_call(
        functools.partial(_ssd_body, q=q, d_ssd=d_ssd),
        grid=(bsz, nc),
        in_specs=[row(d_ssd), row(cch), row(LANE), _const_spec((CONV_W, cch)), _const_spec((1, cch)),
                  _const_spec((1, LANE)), _const_spec((1, LANE)), _const_spec((1, d_ssd)),
                  _const_spec((1, d_ssd))],
        out_specs=row(d_ssd),
        out_shape=jax.ShapeDtypeStruct((t, d_ssd), F32),
        scratch_shapes=[pltpu.VMEM((SUBLANE, cch), F32),
                        pltpu.VMEM((SSD_GROUPS, SSD_STATE, d_ssd // SSD_GROUPS), F32)],
        compiler_params=_cparams("parallel", "arbitrary"),
        name="ssd",
    )(z, xbc, dtp, conv_w, conv_b.reshape(1, cch), pad(dt_bias, 0.0), pad(a_log, 0.0),
      jnp.repeat(d_skip, SSD_HEAD_DIM).reshape(1, d_ssd), norm_g.reshape(1, d_ssd))


def _lru_body(xl_ref, gl_ref, cw_ref, cb_ref, wa_ref, ba_ref, wi_ref, bi_ref, lam_ref,
              y_ref, tail_scr, h_scr, *, tl):
    step = pl.program_id(1)

    @pl.when(step == 0)
    def _():
        h_scr[...] = jnp.zeros(h_scr.shape, F32)

    xr = _causal_conv(xl_ref, cw_ref, cb_ref, tail_scr, step, tl)
    xrb = xr.astype(BF16)
    r = jax.nn.sigmoid(jnp.dot(xrb, wa_ref[...], preferred_element_type=F32) + ba_ref[...])
    i = jax.nn.sigmoid(jnp.dot(xrb, wi_ref[...], preferred_element_type=F32) + bi_ref[...])
    log_a = -LRU_C * r * _softplus(-lam_ref[...])
    a = jnp.exp(log_a)
    b = jnp.sqrt(-jnp.tanh(log_a) * (a * a + 1.0)) * (i * xr)
    rows = lax.broadcasted_iota(jnp.int32, a.shape, 0) % SUBLANE
    s = 1
    while s < SUBLANE:
        keep = rows >= s
        a_sh = jnp.where(keep, pltpu.roll(a, s, 0), 1.0)
        b_sh = jnp.where(keep, pltpu.roll(b, s, 0), 0.0)
        b = a * b_sh + b
        a = a * a_sh
        s *= 2
    h = h_scr[0:1, :]
    gate = jax.nn.gelu(gl_ref[...])
    for g in range(tl // SUBLANE):
        grp = slice(g * SUBLANE, (g + 1) * SUBLANE)
        hs = a[grp, :] * h + b[grp, :]
        y_ref[grp, :] = hs * gate[grp, :]
        h = hs[SUBLANE - 1:SUBLANE, :]
    h_scr[...] = jnp.broadcast_to(h, h_scr.shape)


def _block_diag(w):
    nb, bw, _ = w.shape
    out = jnp.zeros((nb * bw, nb * bw), w.dtype)
    for k in range(nb):
        out = out.at[k * bw:(k + 1) * bw, k * bw:(k + 1) * bw].set(w[k])
    return out


def _lru(xl, gl, conv_w, conv_b, wa, ba, wi, bi, lam, *, bsz, tl=256):
    t, dl = xl.shape
    nt = t // bsz // tl
    row = pl.BlockSpec((tl, dl), lambda b, i: (b * nt + i, 0))
    vec = _const_spec((1, dl))
    return pl.pallas_call(
        functools.partial(_lru_body, tl=tl),
        grid=(bsz, nt),
        in_specs=[row, row, _const_spec((CONV_W, dl)), vec, _const_spec((dl, dl)), vec,
                  _const_spec((dl, dl)), vec, vec],
        out_specs=row,
        out_shape=jax.ShapeDtypeStruct((t, dl), F32),
        scratch_shapes=[pltpu.VMEM((SUBLANE, dl), F32), pltpu.VMEM((SUBLANE, dl), F32)],
        compiler_params=_cparams("parallel", "arbitrary"),
        name="rglru",
    )(xl, gl, conv_w, conv_b.reshape(1, dl), _block_diag(wa).astype(BF16), ba.reshape(1, dl),
      _block_diag(wi).astype(BF16), bi.reshape(1, dl), lam.reshape(1, dl))


def _rope(x, ct, st, axis):
    width = x.shape[axis]
    fm = lax.broadcasted_iota(jnp.int32, x.shape, axis) % ATT_HEAD_DIM
    half = ROPE_ROT // 2
    partner = jnp.where(fm < half, pltpu.roll(x, width - half, axis), pltpu.roll(x, half, axis))
    return x * ct + partner * st


def _dot_nt(a, b):
    return lax.dot_general(a, b, (((1,), (1,)), ((), ())), preferred_element_type=F32)


def _prep_body(cq_ref, ckv_ref, misc_ref, ct_ref, st_ref, ctt_ref, stt_ref, cqg_ref, ckvg_ref, wuqt_ref,
               wuk_ref, wuvt_ref, qg_ref, kg_ref, wqit_ref, kig_ref,
               k_ref, ki_ref, vt_ref, qt_ref, qit_ref, wc_ref, *, tm):
    ct = ct_ref[...]
    st = st_ref[...]
    d_att = wuk_ref.shape[1]
    gr = lax.broadcasted_iota(jnp.int32, (d_att, d_att), 0) // ATT_HEAD_DIM
    gc = lax.broadcasted_iota(jnp.int32, (d_att, d_att), 1) // ATT_HEAD_DIM
    head_mean = jnp.where(gr == gc, 1.0 / ATT_HEAD_DIM, 0.0).astype(F32)

    cqn = _rms(cq_ref[...], cqg_ref[...]).astype(BF16)
    ckvn = _rms(ckv_ref[...], ckvg_ref[...]).astype(BF16)

    qt = _dot_nt(wuqt_ref[...], cqn)
    ms = _dot_sel_lhs(head_mean, qt * qt)
    qt = qt * lax.rsqrt(ms + RMS_EPS) * qg_ref[...]
    qt_ref[...] = (_rope(qt, ctt_ref[...], stt_ref[...], 0) * (ATT_HEAD_DIM ** -0.5 * LOG2_E)).astype(BF16)
    vt = _dot_nt(wuvt_ref[...], ckvn).astype(BF16)
    for pr in range(ATT_HEADS // 2):
        base = pr * V_PAIR_ROWS
        vt_ref[base:base + 2 * ATT_HEAD_DIM, :] = vt[pr * 2 * ATT_HEAD_DIM:(pr + 1) * 2 * ATT_HEAD_DIM, :]
        vt_ref[base + 2 * ATT_HEAD_DIM:base + V_PAIR_ROWS, :] = jnp.ones((V_ONES_ROWS, tm), BF16)

    k = jnp.dot(ckvn, wuk_ref[...], preferred_element_type=F32)
    ms = _dot_sel_rhs(k * k, head_mean)
    k_ref[...] = _rope(k * lax.rsqrt(ms + RMS_EPS) * kg_ref[...], ct, st, 1).astype(BF16)

    qit = _rope(_dot_nt(wqit_ref[...], cqn), ctt_ref[...], stt_ref[...], 0)
    misc = misc_ref[...]
    misc_t = misc.T
    wscale = IDX_HEADS ** -0.5 * IDX_DIM ** -0.5
    for j in range(tm // Q_BLOCK):
        cols = slice(j * Q_BLOCK, (j + 1) * Q_BLOCK)
        for h in range(IDX_HEADS):
            qit_ref[j, 0:IDX_DIM, h * Q_BLOCK:(h + 1) * Q_BLOCK] = (
                qit[h * IDX_DIM:(h + 1) * IDX_DIM, cols].astype(BF16))
        qit_ref[j, IDX_DIM:, :] = jnp.zeros((LANE - IDX_DIM, IDX_HEADS * Q_BLOCK), BF16)
        wrow = jnp.concatenate([misc_t[IDX_DIM + h:IDX_DIM + h + 1, cols] for h in range(IDX_HEADS)], axis=1)
        wc_ref[j] = jnp.broadcast_to(wrow * wscale, (SUBLANE, IDX_HEADS * Q_BLOCK))

    lane = lax.broadcasted_iota(jnp.int32, misc.shape, 1)
    km = jnp.where(lane < IDX_DIM, misc, 0.0)
    ms = jnp.sum(km * km, axis=-1, keepdims=True) * (1.0 / IDX_DIM)
    kin = km * lax.rsqrt(ms + RMS_EPS) * kig_ref[...]
    ki_ref[...] = _rope(kin, ct[:, :LANE], st[:, :LANE], 1).astype(BF16)


def _rope_lane_tables(length, width):
    half = ROPE_ROT // 2
    inv = ROPE_THETA ** (-jnp.arange(half, dtype=F32) * 2.0 / ROPE_ROT)
    ang = jnp.arange(length, dtype=F32)[:, None] * inv[None, :]
    cos, sin = jnp.cos(ang), jnp.sin(ang)
    ones = jnp.ones((length, ATT_HEAD_DIM - ROPE_ROT), F32)
    ct = jnp.concatenate([cos, cos, ones], axis=1)
    st = jnp.concatenate([-sin, sin, 0.0 * ones], axis=1)
    reps = width // ATT_HEAD_DIM
    return jnp.tile(ct, (1, reps)), jnp.tile(st, (1, reps))


def _dsa_prep(cq, ckv, misc, p, *, bsz):
    t, qr = cq.shape
    kvr = ckv.shape[1]
    tm = KEY_CHUNK
    length = t // bsz
    nt = length // tm
    d_att = ATT_HEADS * ATT_HEAD_DIM
    ct, st = _rope_lane_tables(length, d_att)
    wukv = p["w_ukv"].reshape(kvr, ATT_HEADS, 2, ATT_HEAD_DIM)
    wuk = wukv[:, :, 0, :].reshape(kvr, d_att).astype(BF16)
    wuvt = wukv[:, :, 1, :].reshape(kvr, d_att).T.astype(BF16)
    kig = jnp.concatenate([p["kidx_norm"], jnp.zeros((LANE - IDX_DIM,), F32)]).reshape(1, LANE)
    row = lambda c: pl.BlockSpec((tm, c), lambda b, i: (b * nt + i, 0))
    tab = pl.BlockSpec((tm, d_att), lambda b, i: (i, 0))
    tab_t = pl.BlockSpec((d_att, tm), lambda b, i: (0, i))
    nqb = tm // Q_BLOCK
    hq = IDX_HEADS * Q_BLOCK
    return pl.pallas_call(
        functools.partial(_prep_body, tm=tm),
        grid=(bsz, nt),
        in_specs=[row(qr), row(kvr), row(LANE), tab, tab, tab_t, tab_t, _const_spec((1, qr)),
                  _const_spec((1, kvr)), _const_spec((d_att, qr)), _const_spec((kvr, d_att)),
                  _const_spec((d_att, kvr)), _const_spec((d_att, 1)), _const_spec((1, d_att)),
                  _const_spec((IDX_HEADS * IDX_DIM, qr)), _const_spec((1, LANE))],
        out_specs=[row(d_att), row(LANE),
                   pl.BlockSpec((None, V_ROWS, tm), lambda b, i: (b * nt + i, 0, 0)),
                   pl.BlockSpec((None, d_att, tm), lambda b, i: (b * nt + i, 0, 0)),
                   pl.BlockSpec((nqb, LANE, hq), lambda b, i: (b * nt + i, 0, 0)),
                   pl.BlockSpec((nqb, SUBLANE, hq), lambda b, i: (b * nt + i, 0, 0))],
        out_shape=[jax.ShapeDtypeStruct((t, d_att), BF16),
                   jax.ShapeDtypeStruct((t, LANE), BF16),
                   jax.ShapeDtypeStruct((bsz * nt, V_ROWS, tm), BF16),
                   jax.ShapeDtypeStruct((bsz * nt, d_att, tm), BF16),
                   jax.ShapeDtypeStruct((t // Q_BLOCK, LANE, hq), BF16),
                   jax.ShapeDtypeStruct((t // Q_BLOCK, SUBLANE, hq), F32)],
        compiler_params=_cparams("parallel", "parallel"),
        name="dsa_prep",
    )(cq, ckv, misc, ct, st, ct.T, st.T, p["cq_norm"].reshape(1, qr), p["ckv_norm"].reshape(1, kvr),
      p["w_uq"].T.astype(BF16), wuk, wuvt, jnp.tile(p["q_norm"], ATT_HEADS).reshape(d_att, 1),
      jnp.tile(p["k_norm"], ATT_HEADS).reshape(1, d_att), p["w_qidx"].T.astype(BF16), kig)


INT_MIN = -2 ** 31
F32_TINY = 2.0 ** -126
F32_LOWEST = -3.4028234663852886e38
SEARCH_VALUE_STEPS = 18
SEARCH_CAP = 80


def _order_key_to_float(u):
    sk = u ^ jnp.int32(INT_MIN)
    fb = sk ^ ((sk >> 31) & jnp.int32(0x7FFFFFFF))
    return lax.bitcast_convert_type(fb, F32)


def _float_to_order_key(f):
    b = lax.bitcast_convert_type(f, jnp.int32)
    return (b ^ ((b >> 31) & jnp.int32(0x7FFFFFFF))) ^ jnp.int32(INT_MIN)


def _fold_rows(x, op):
    parts = [x[i * SUBLANE:(i + 1) * SUBLANE, :] for i in range(x.shape[0] // SUBLANE)]
    while len(parts) > 1:
        parts = [op(parts[i], parts[i + 1]) for i in range(0, len(parts), 2)]
    return parts[0]


def _dsa_body(qit_ref, wc_ref, qt_ref, ki_ref, k_ref, vt_ref, o_ref,
              sc_scr, lg_scr, qbd_scr, oacc_scr, *, n_sel, nk):
    kc = KEY_CHUNK
    qb = (nk - 1) * (kc // Q_BLOCK) + pl.program_id(1)
    qpos = qb * Q_BLOCK + lax.broadcasted_iota(jnp.int32, (1, Q_BLOCK), 1)
    hq = IDX_HEADS * Q_BLOCK
    k_f = jnp.float32(n_sel)

    def chunk(c):
        return pl.ds(c * kc, kc)

    def over_chunks(body, carry):
        for c in range(nk):
            carry = body(c, carry)
        return carry

    qit = qit_ref[...]
    wc = wc_ref[0:1, :]

    def score_chunk(c, carry):
        amax, c_pos, c_nn = carry
        s4 = jnp.dot(ki_ref[chunk(c), :], qit, preferred_element_type=F32)
        s4 = jnp.maximum(s4, 0.0) * wc
        s = (s4[:, 0:Q_BLOCK] + s4[:, Q_BLOCK:2 * Q_BLOCK]) + (s4[:, 2 * Q_BLOCK:3 * Q_BLOCK] + s4[:, 3 * Q_BLOCK:])
        kpos = c * kc + lax.broadcasted_iota(jnp.int32, (kc, Q_BLOCK), 0)
        sm = jnp.where(kpos <= qpos, s, -jnp.inf)
        sc_scr[chunk(c), :] = sm
        return (jnp.maximum(amax, _fold_rows(jnp.abs(s), jnp.maximum)),
                c_pos + _fold_rows(jnp.where(sm >= F32_TINY, 1.0, 0.0), jnp.add),
                c_nn + _fold_rows(jnp.where(sm >= 0.0, 1.0, 0.0), jnp.add))

    zeros8 = jnp.zeros((SUBLANE, Q_BLOCK), F32)
    amax, c_pos, c_nn = over_chunks(score_chunk, (zeros8, zeros8, zeros8))
    amax = jnp.max(amax, axis=0, keepdims=True)
    c_pos = jnp.sum(c_pos, axis=0, keepdims=True)
    c_nn = jnp.sum(c_nn, axis=0, keepdims=True)

    def count_ge(*thrs):
        def body(c, accs):
            blk = sc_scr[chunk(c), :]
            return tuple(a + _fold_rows(jnp.where(blk >= t, 1.0, 0.0), jnp.add) for a, t in zip(accs, thrs))
        accs = over_chunks(body, tuple(jnp.zeros((SUBLANE, Q_BLOCK), F32) for _ in thrs))
        return tuple(jnp.sum(a, axis=0, keepdims=True) for a in accs)

    n_valid = (qpos + 1).astype(F32)
    take_all = n_valid <= k_f
    zero_tie = (c_pos < k_f) & (c_nn >= k_f)
    pos = c_pos >= k_f
    hi_top = _order_key_to_float(_float_to_order_key(amax) + 1)
    lo = jnp.where(take_all, F32_LOWEST, jnp.where(zero_tie, 0.0, jnp.where(pos, F32_TINY, -amax)))
    hi = jnp.where(take_all, jnp.inf, jnp.where(zero_tie, F32_TINY, jnp.where(pos, hi_top, 0.0)))
    cnt_lo = jnp.where(take_all | (~zero_tie & ~pos), n_valid, jnp.where(zero_tie, c_nn, c_pos))
    cnt_hi = jnp.where(take_all | (~zero_tie & pos), 0.0, jnp.where(zero_tie, c_pos, c_nn))
    settled = take_all | zero_tie

    def search_step(st, on_values):
        lk, hk, lo, hi, cnt_lo, cnt_hi = st
        midk = lk + ((hk - lk) >> 1)
        mid = _order_key_to_float(midk)
        if on_values:
            vmid = lo + (hi - lo) * 0.5
            inside = (vmid > lo) & (vmid < hi)
            mid = jnp.where(inside, vmid, mid)
            midk = jnp.where(inside, _float_to_order_key(vmid), midk)
        cnt, = count_ge(mid)
        ge = cnt >= k_f
        return (jnp.where(ge, midk, lk), jnp.where(ge, hk, midk), jnp.where(ge, mid, lo), jnp.where(ge, hi, mid),
                jnp.where(ge, cnt, cnt_lo), jnp.where(ge, cnt_hi, cnt))

    def search_cond(carry):
        it, (lk, hk, _, _, cnt_lo, _) = carry
        open_ = jnp.logical_not(settled | (cnt_lo == k_f) | (hk - lk <= 1))
        return (it < SEARCH_CAP) & (jnp.sum(jnp.where(open_, 1.0, 0.0)) > 0.0)

    def search_pair(carry):
        it, st = carry
        return it + 1, search_step(search_step(st, False), False)

    st = (_float_to_order_key(lo), _float_to_order_key(hi), lo, hi, cnt_lo, cnt_hi)
    st = lax.fori_loop(0, SEARCH_VALUE_STEPS, lambda _, s: search_step(s, True), st)
    _, (_, _, lo, hi, cnt_lo, cnt_hi) = lax.while_loop(search_cond, search_pair, (jnp.int32(0), st))
    need = k_f - cnt_hi

    sub = Q_BLOCK
    ri = lax.broadcasted_iota(jnp.int32, (sub, sub), 0)
    ci = lax.broadcasted_iota(jnp.int32, (sub, sub), 1)
    before = (ri > ci).astype(BF16)

    qbd_scr[...] = jnp.zeros(qbd_scr.shape, BF16)
    for h in range(ATT_HEADS):
        rows = slice(h * ATT_HEAD_DIM, (h + 1) * ATT_HEAD_DIM)
        qbd_scr[rows, h * Q_BLOCK:(h + 1) * Q_BLOCK] = qt_ref[rows, :]
    qbd = qbd_scr[...]

    def logit_chunk(c, carry):
        m, seen = carry
        for j in range(kc // sub):
            rows = pl.ds(c * kc + j * sub, sub)
            blk = sc_scr[rows, :]
            in_hi = blk >= hi
            tie = (blk >= lo) & jnp.logical_not(in_hi)
            tie_f = jnp.where(tie, 1.0, 0.0)
            rank = jnp.dot(before, tie_f.astype(BF16), preferred_element_type=F32) + seen
            sel = in_hi | (tie & (rank < need))
            seen = seen + jnp.sum(tie_f, axis=0, keepdims=True)
            b = jnp.where(sel, 0.0, -jnp.inf)
            lg = jnp.dot(k_ref[rows, :], qbd, preferred_element_type=F32) + jnp.concatenate([b] * ATT_HEADS, axis=1)
            lg_scr[rows, :] = lg
            m = jnp.maximum(m, jnp.max(lg, axis=0, keepdims=True))
        return m, seen

    m, _ = over_chunks(logit_chunk, (jnp.full((1, hq), -jnp.inf, F32), jnp.zeros((1, Q_BLOCK), F32)))
    oacc_scr[...] = jnp.zeros(oacc_scr.shape, F32)

    pair_cols = 2 * Q_BLOCK

    def value_chunk(c, carry):
        depth = 2 * Q_BLOCK
        for pr in range(ATT_HEADS // 2):
            rws = slice(pr * V_PAIR_ROWS, (pr + 1) * V_PAIR_ROWS)
            cls = slice(pr * pair_cols, (pr + 1) * pair_cols)
            acc = oacc_scr[rws, :]
            for j in range(kc // depth):
                rows = pl.ds(c * kc + j * depth, depth)
                p = jnp.exp2(lg_scr[rows, cls] - m[:, cls]).astype(BF16)
                acc = acc + jnp.dot(vt_ref[c, rws, j * depth:(j + 1) * depth], p, preferred_element_type=F32)
            oacc_scr[rws, :] = acc
        return carry

    over_chunks(value_chunk, 0)
    outs = []
    for h in range(ATT_HEADS):
        base = (h // 2) * V_PAIR_ROWS
        cols = slice((h % 2) * Q_BLOCK, (h % 2 + 1) * Q_BLOCK)
        ones_row = base + 2 * ATT_HEAD_DIM
        ssum = oacc_scr[ones_row:ones_row + 1, cols]
        outs.append(oacc_scr[base + (h % 2) * ATT_HEAD_DIM:base + (h % 2 + 1) * ATT_HEAD_DIM, cols] / ssum)
    o_ref[...] = jnp.concatenate(outs, axis=0).T


def _dsa(k, ki, vt, qt, qit, wc, *, bsz):
    t, d_att = k.shape
    length = t // bsz
    nb = length // Q_BLOCK
    kc = KEY_CHUNK
    nkc = length // kc
    hq = IDX_HEADS * Q_BLOCK
    per_q = kc // Q_BLOCK
    vt = vt.reshape(bsz, nkc, V_ROWS, kc)
    ki = ki.reshape(bsz, length, LANE)
    k = k.reshape(bsz, length, d_att)
    outs = []
    for g in range(nkc):
        nk = g + 1
        qblk = lambda b, i, g=g: b * nb + g * per_q + i
        outs.append(pl.pallas_call(
            functools.partial(_dsa_body, n_sel=min(TOP_K, length // 4), nk=nk),
            grid=(bsz, per_q),
            in_specs=[pl.BlockSpec((None, LANE, hq), lambda b, i, f=qblk: (f(b, i), 0, 0)),
                      pl.BlockSpec((None, SUBLANE, hq), lambda b, i, f=qblk: (f(b, i), 0, 0)),
                      pl.BlockSpec((None, d_att, Q_BLOCK), lambda b, i, g=g: (b * nkc + g, 0, i)),
                      pl.BlockSpec((None, nk * kc, LANE), lambda b, i: (b, 0, 0)),
                      pl.BlockSpec((None, nk * kc, d_att), lambda b, i: (b, 0, 0)),
                      pl.BlockSpec((None, nk, V_ROWS, kc), lambda b, i: (b, 0, 0, 0))],
            out_specs=pl.BlockSpec((None, Q_BLOCK, d_att), lambda b, i: (b, i, 0)),
            out_shape=jax.ShapeDtypeStruct((bsz, kc, d_att), F32),
            scratch_shapes=[pltpu.VMEM((nk * kc, Q_BLOCK), F32), pltpu.VMEM((nk * kc, hq), F32),
                            pltpu.VMEM((d_att, hq), BF16), pltpu.VMEM((V_ROWS, 2 * Q_BLOCK), F32)],
            compiler_params=_cparams("parallel", "arbitrary"),
            name=f"dsa{nk}",
        )(qit, wc, qt, ki, k, vt))
    return jnp.stack(outs, axis=1).reshape(t, d_att)


def _pad_cols(w, width):
    return jnp.concatenate([w, jnp.zeros((w.shape[0], width - w.shape[1]), w.dtype)], axis=1)


def kernel(x, norm_ffn1, ffn1_w13, ffn1_w2, norm_mix, w_in, ssd_conv_w, ssd_conv_b, ssd_dt_bias, ssd_a_log, ssd_d, ssd_norm, cq_norm, ckv_norm, w_uq, w_ukv, q_norm, k_norm, w_qidx, kidx_norm, lru_conv_w, lru_conv_b, lru_wa, lru_ba, lru_wi, lru_bi, lru_lambda, w_out, norm_ffn2, ffn2_w13, ffn2_w2):
    bsz, length, d = x.shape
    depth = w_in.shape[0]
    d_ssd = ssd_norm.shape[1]
    conv_ch = ssd_conv_b.shape[1]
    heads = ssd_d.shape[1]
    q_rank, kv_rank = cq_norm.shape[1], ckv_norm.shape[1]
    d_lru = lru_lambda.shape[1]
    sizes = [d_ssd, conv_ch, heads, q_rank, kv_rank, IDX_DIM, IDX_HEADS, d_lru, d_lru]
    offs = [0]
    for s in sizes:
        offs.append(offs[-1] + s)
    widths = (d_ssd, conv_ch, LANE, q_rank, kv_rank, LANE, d_lru, d_lru)
    zero_rows = lambda n: jnp.zeros((n, d), BF16)

    xt = x.reshape(bsz * length, d)
    w13_1, w2_1, w13_2, w2_2 = (w.astype(BF16) for w in (ffn1_w13, ffn1_w2, ffn2_w13, ffn2_w2))
    for l in range(depth):
        wt = w_in[l].T.astype(BF16)
        w_pad = jnp.concatenate(
            [wt[offs[0]:offs[3]], zero_rows(LANE - heads), wt[offs[3]:offs[7]],
             zero_rows(LANE - IDX_DIM - IDX_HEADS), wt[offs[7]:offs[9]]], axis=0)
        xt, z, xbc, dtp, cq, ckv, misc, xl, gl = _ffn(xt, norm_ffn1[l], w13_1, w2_1, l,
                                                      proj=(norm_mix[l], w_pad, widths))
        y_ssd = _ssd(z, xbc, dtp, ssd_conv_w[l], ssd_conv_b[l], ssd_dt_bias[l], ssd_a_log[l], ssd_d[l],
                     ssd_norm[l], bsz=bsz)
        y_lru = _lru(xl, gl, lru_conv_w[l], lru_conv_b[l], lru_wa[l], lru_ba[l], lru_wi[l], lru_bi[l],
                     lru_lambda[l], bsz=bsz)
        prm = dict(cq_norm=cq_norm[l], ckv_norm=ckv_norm[l], w_uq=w_uq[l], w_ukv=w_ukv[l], q_norm=q_norm[l],
                   k_norm=k_norm[l], w_qidx=w_qidx[l], kidx_norm=kidx_norm[l])
        k, ki, vt, qt, qit, wc = _dsa_prep(cq, ckv, misc, prm, bsz=bsz)
        y_att = _dsa(k, ki, vt, qt, qit, wc, bsz=bsz)
        xt = _ffn(xt, norm_ffn2[l], w13_2, w2_2, l, mixed=([y_ssd, y_att, y_lru], w_out[l].astype(BF16)))
    return xt.reshape(bsz, length, d)
```

```python
import functools

import jax
import jax.numpy as jnp
from jax import lax
from jax.experimental import pallas as pl
from jax.experimental.pallas import tpu as pltpu

F32 = jnp.float32
BF16 = jnp.bfloat16

RMS_EPS = 1e-6
LOG2_E = 1.4426950408889634
SSD_HEAD_DIM = 64
SSD_GROUPS = 2
SSD_STATE = 128
SSD_CHUNK = 128
ATT_HEADS = 4
ATT_HEAD_DIM = 64
IDX_HEADS = 4
IDX_DIM = 64
TOP_K = 256
Q_BLOCK = 128
ROPE_THETA = 500000.0
ROPE_ROT = 16
LRU_C = 8.0
CONV_W = 4
LANE = 128
SUBLANE = 8
KEY_CHUNK = 512
V_ONES_ROWS = 16
V_PAIR_ROWS = 2 * ATT_HEAD_DIM + V_ONES_ROWS
V_ROWS = (ATT_HEADS // 2) * V_PAIR_ROWS
VMEM_LIMIT = 56 * 1024 * 1024


def _cparams(*sem):
    return pltpu.CompilerParams(dimension_semantics=sem, vmem_limit_bytes=VMEM_LIMIT)


def _const_spec(shape):
    nd = len(shape)
    return pl.BlockSpec(shape, lambda *_: (0,) * nd, pipeline_mode=pl.Buffered(1))


def _rms(x, g):
    ms = jnp.mean(x * x, axis=-1, keepdims=True)
    return x * lax.rsqrt(ms + RMS_EPS) * g


def _softplus(x):
    return jnp.maximum(x, 0.0) + jnp.log1p(jnp.exp(-jnp.abs(x)))


def _silu(x):
    return x * jax.nn.sigmoid(x)


def _split3(x):
    hi = x.astype(BF16)
    r = x - hi.astype(F32)
    mid = r.astype(BF16)
    return hi, mid, (r - mid.astype(F32)).astype(BF16)


def _dot_sel_rhs(x, sel):
    sel = sel.astype(BF16)
    return sum(jnp.dot(p, sel, preferred_element_type=F32) for p in _split3(x))


def _dot_sel_lhs(sel, x):
    sel = sel.astype(BF16)
    return sum(jnp.dot(sel, p, preferred_element_type=F32) for p in _split3(x))


def _ffn_body(*refs, d_ff, tf, n_mix, proj_widths):
    refs = list(refs)
    x = refs.pop(0)[...]
    if n_mix:
        y_refs = [refs.pop(0) for _ in range(n_mix)]
        wo_ref = refs.pop(0)
        off = 0
        for y_ref in y_refs:
            wd = y_ref.shape[1]
            x = x + jnp.dot(y_ref[...].astype(BF16), wo_ref[off:off + wd, :], preferred_element_type=F32)
            off += wd
    g_ref, w13_ref, w2_ref = refs.pop(0), refs.pop(0), refs.pop(0)
    if proj_widths:
        gp_ref, wi_ref = refs.pop(0), refs.pop(0)
    o_ref = refs.pop(0)

    h = _rms(x, g_ref[...]).astype(BF16)
    acc = jnp.zeros(x.shape, F32)
    for j in range(d_ff // tf):
        g = jnp.dot(h, w13_ref[:, j * tf:(j + 1) * tf], preferred_element_type=F32)
        u = jnp.dot(h, w13_ref[:, d_ff + j * tf:d_ff + (j + 1) * tf], preferred_element_type=F32)
        a = (_silu(g) * u).astype(BF16)
        acc = acc + jnp.dot(a, w2_ref[j * tf:(j + 1) * tf, :], preferred_element_type=F32)
    x = x + 0.5 * acc
    o_ref[...] = x

    if proj_widths:
        hp = _rms(x, gp_ref[...]).astype(BF16)
        off = 0
        for p_ref, wd in zip(refs, proj_widths):
            p_ref[...] = _dot_nt(hp, wi_ref[off:off + wd, :])
            off += wd


def _ffn(x, g, w13, w2, layer, *, mixed=None, proj=None, tm=512, tf=256):
    t, d = x.shape
    d_ff = w2.shape[1]
    row = lambda c: pl.BlockSpec((tm, c), lambda i: (i, 0))
    layer_spec = lambda r, c: pl.BlockSpec((None, r, c), lambda i: (layer, 0, 0), pipeline_mode=pl.Buffered(1))
    args, in_specs = [x], [row(d)]
    if mixed is not None:
        ys, w_out = mixed
        args += [*ys, w_out]
        in_specs += [row(y.shape[1]) for y in ys] + [_const_spec(w_out.shape)]
    args += [g.reshape(1, d), w13, w2]
    in_specs += [_const_spec((1, d)), layer_spec(d, 2 * d_ff), layer_spec(d_ff, d)]
    out_specs, out_shape, widths = [row(d)], [jax.ShapeDtypeStruct((t, d), F32)], ()
    if proj is not None:
        gp, w_in, widths = proj
        args += [gp.reshape(1, d), w_in]
        in_specs += [_const_spec((1, d)), _const_spec(w_in.shape)]
        out_specs += [row(wd) for wd in widths]
        out_shape += [jax.ShapeDtypeStruct((t, wd), F32) for wd in widths]
    outs = pl.pallas_call(
        functools.partial(_ffn_body, d_ff=d_ff, tf=tf, n_mix=len(mixed[0]) if mixed is not None else 0,
                          proj_widths=tuple(widths)),
        grid=(t // tm,),
        in_specs=in_specs,
        out_specs=out_specs,
        out_shape=out_shape,
        compiler_params=_cparams("parallel"),
        name="ffn",
    )(*args)
    return outs if proj is not None else outs[0]


def _causal_conv(x_ref, cw_ref, cb_ref, tail_scr, step, rows):
    @pl.when(step == 0)
    def _():
        tail_scr[...] = jnp.zeros(tail_scr.shape, F32)

    x = x_ref[...]
    tail = tail_scr[...]
    row = lax.broadcasted_iota(jnp.int32, tail.shape, 0)
    y = cb_ref[...] + cw_ref[CONV_W - 1:CONV_W, :] * x
    for w in range(CONV_W - 1):
        back = CONV_W - 1 - w
        xr = pltpu.roll(x, back, 0)
        top = jnp.where(row < back, pltpu.roll(tail, back, 0), xr[0:SUBLANE, :])
        y = y + cw_ref[w:w + 1, :] * jnp.concatenate([top, xr[SUBLANE:, :]], axis=0)
    tail_scr[...] = x[rows - SUBLANE:rows, :]
    return y


def _ssd_body(z_ref, xbc_ref, dtp_ref, cw_ref, cb_ref, dtb_ref, alog_ref, dsk_ref, ng_ref,
              y_ref, tail_scr, st_scr, *, q, d_ssd):
    step = pl.program_id(1)
    n = SSD_STATE
    gw = d_ssd // SSD_GROUPS
    rpg = gw // SSD_HEAD_DIM

    @pl.when(step == 0)
    def _():
        st_scr[...] = jnp.zeros(st_scr.shape, F32)

    xbc = _silu(_causal_conv(xbc_ref, cw_ref, cb_ref, tail_scr, step, q))
    xs = xbc[:, :d_ssd]
    bm = xbc[:, d_ssd:d_ssd + SSD_GROUPS * n]
    cm = xbc[:, d_ssd + SSD_GROUPS * n:]

    dt = _softplus(dtp_ref[...] + dtb_ref[...])
    adt = dt * (-jnp.exp(alog_ref[...]))
    ri = lax.broadcasted_iota(jnp.int32, (q, q), 0)
    ci = lax.broadcasted_iota(jnp.int32, (q, q), 1)
    causal = ri >= ci
    acum = _dot_sel_lhs(causal.astype(F32), adt)
    er = lax.broadcasted_iota(jnp.int32, (LANE, d_ssd), 0)
    ec = lax.broadcasted_iota(jnp.int32, (LANE, d_ssd), 1)
    expand = (ec // SSD_HEAD_DIM == er).astype(F32)
    a_x = _dot_sel_rhs(acum, expand)
    dt_x = _dot_sel_rhs(dt, expand)
    acum_t = acum.T
    a_last = a_x[q - 1:q, :]
    exp_a = jnp.exp(a_x)
    xdt = xs * dt_x
    xsw = xdt * jnp.exp(a_last - a_x)
    cdec = jnp.exp(a_last)

    ys = []
    for g in range(SSD_GROUPS):
        bm_g = bm[:, g * n:(g + 1) * n]
        cm_g = cm[:, g * n:(g + 1) * n].astype(BF16)
        cb = lax.dot_general(cm_g, bm_g.astype(BF16), (((1,), (1,)), ((), ())), preferred_element_type=F32)
        yd = []
        for r in range(rpg):
            h = g * rpg + r
            seg = acum[:, h:h + 1] - acum_t[h:h + 1, :]
            decay = jnp.exp(jnp.where(causal, seg, -jnp.inf))
            m = (cb * decay).astype(BF16)
            xh = xdt[:, h * SSD_HEAD_DIM:(h + 1) * SSD_HEAD_DIM].astype(BF16)
            yd.append(jnp.dot(m, xh, preferred_element_type=F32))
        yd = jnp.concatenate(yd, axis=1)
        gs = slice(g * gw, (g + 1) * gw)
        prev = st_scr[g]
        yoff = jnp.dot(cm_g, prev.astype(BF16), preferred_element_type=F32) * exp_a[:, gs]
        st = jnp.dot(bm_g.T.astype(BF16), xsw[:, gs].astype(BF16), preferred_element_type=F32)
        st_scr[g] = prev * cdec[:, gs] + st
        ys.append(yd + yoff + dsk_ref[:, gs] * xs[:, gs])

    z = z_ref[...]
    for g in range(SSD_GROUPS):
        gs = slice(g * gw, (g + 1) * gw)
        y_ref[:, gs] = _rms(ys[g] * _silu(z[:, gs]), ng_ref[:, gs])


def _ssd(z, xbc, dtp, conv_w, conv_b, dt_bias, a_log, d_skip, norm_g, *, bsz):
    t, d_ssd = z.shape
    cch = xbc.shape[1]
    q = SSD_CHUNK
    nc = t // bsz // q
    heads = d_ssd // SSD_HEAD_DIM
    pad = lambda v, fill: jnp.concatenate([v, jnp.full((LANE - heads,), fill, F32)]).reshape(1, LANE)
    row = lambda c: pl.BlockSpec((q, c), lambda b, i: (b * nc + i, 0))
    return pl.pallas_call(
        functools.partial(_ssd_body, q=q, d_ssd=d_ssd),
        grid=(bsz, nc),
        in_specs=[row(d_ssd), row(cch), row(LANE), _const_spec((CONV_W, cch)), _const_spec((1, cch)),
                  _const_spec((1, LANE)), _const_spec((1, LANE)), _const_spec((1, d_ssd)),
                  _const_spec((1, d_ssd))],
        out_specs=row(d_ssd),
        out_shape=jax.ShapeDtypeStruct((t, d_ssd), F32),
        scratch_shapes=[pltpu.VMEM((SUBLANE, cch), F32),
                        pltpu.VMEM((SSD_GROUPS, SSD_STATE, d_ssd // SSD_GROUPS), F32)],
        compiler_params=_cparams("parallel", "arbitrary"),
        name="ssd",
    )(z, xbc, dtp, conv_w, conv_b.reshape(1, cch), pad(dt_bias, 0.0), pad(a_log, 0.0),
      jnp.repeat(d_skip, SSD_HEAD_DIM).reshape(1, d_ssd), norm_g.reshape(1, d_ssd))


def _lru_body(xl_ref, gl_ref, cw_ref, cb_ref, wa_ref, ba_ref, wi_ref, bi_ref, lam_ref,
              y_ref, tail_scr, h_scr, *, tl):
    step = pl.program_id(1)

    @pl.when(step == 0)
    def _():
        h_scr[...] = jnp.zeros(h_scr.shape, F32)

    xr = _causal_conv(xl_ref, cw_ref, cb_ref, tail_scr, step, tl)
    xrb = xr.astype(BF16)
    r = jax.nn.sigmoid(jnp.dot(xrb, wa_ref[...], preferred_element_type=F32) + ba_ref[...])
    i = jax.nn.sigmoid(jnp.dot(xrb, wi_ref[...], preferred_element_type=F32) + bi_ref[...])
    log_a = -LRU_C * r * _softplus(-lam_ref[...])
    a = jnp.exp(log_a)
    b = jnp.sqrt(-jnp.tanh(log_a) * (a * a + 1.0)) * (i * xr)
    rows = lax.broadcasted_iota(jnp.int32, a.shape, 0) % SUBLANE
    s = 1
    while s < SUBLANE:
        keep = rows >= s
        a_sh = jnp.where(keep, pltpu.roll(a, s, 0), 1.0)
        b_sh = jnp.where(keep, pltpu.roll(b, s, 0), 0.0)
        b = a * b_sh + b
        a = a * a_sh
        s *= 2
    h = h_scr[0:1, :]
    gate = jax.nn.gelu(gl_ref[...])
    for g in range(tl // SUBLANE):
        grp = slice(g * SUBLANE, (g + 1) * SUBLANE)
        hs = a[grp, :] * h + b[grp, :]
        y_ref[grp, :] = hs * gate[grp, :]
        h = hs[SUBLANE - 1:SUBLANE, :]
    h_scr[...] = jnp.broadcast_to(h, h_scr.shape)


def _block_diag(w):
    nb, bw, _ = w.shape
    out = jnp.zeros((nb * bw, nb * bw), w.dtype)
    for k in range(nb):
        out = out.at[k * bw:(k + 1) * bw, k * bw:(k + 1) * bw].set(w[k])
    return out


def _lru(xl, gl, conv_w, conv_b, wa, ba, wi, bi, lam, *, bsz, tl=512):
    t, dl = xl.shape
    nt = t // bsz // tl
    row = pl.BlockSpec((tl, dl), lambda b, i: (b * nt + i, 0))
    vec = _const_spec((1, dl))
    return pl.pallas_call(
        functools.partial(_lru_body, tl=tl),
        grid=(bsz, nt),
        in_specs=[row, row, _const_spec((CONV_W, dl)), vec, _const_spec((dl, dl)), vec,
                  _const_spec((dl, dl)), vec, vec],
        out_specs=row,
        out_shape=jax.ShapeDtypeStruct((t, dl), F32),
        scratch_shapes=[pltpu.VMEM((SUBLANE, dl), F32), pltpu.VMEM((SUBLANE, dl), F32)],
        compiler_params=_cparams("parallel", "arbitrary"),
        name="rglru",
    )(xl, gl, conv_w, conv_b.reshape(1, dl), _block_diag(wa).astype(BF16), ba.reshape(1, dl),
      _block_diag(wi).astype(BF16), bi.reshape(1, dl), lam.reshape(1, dl))


def _rope(x, ct, st, axis):
    width = x.shape[axis]
    fm = lax.broadcasted_iota(jnp.int32, x.shape, axis) % ATT_HEAD_DIM
    half = ROPE_ROT // 2
    partner = jnp.where(fm < half, pltpu.roll(x, width - half, axis), pltpu.roll(x, half, axis))
    return x * ct + partner * st


def _dot_nt(a, b):
    return lax.dot_general(a, b, (((1,), (1,)), ((), ())), preferred_element_type=F32)


def _prep_body(cq_ref, ckv_ref, misc_ref, ct_ref, st_ref, ctt_ref, stt_ref, cqg_ref, ckvg_ref, wuqt_ref,
               wuk_ref, wuvt_ref, qg_ref, kg_ref, wqit_ref, kig_ref,
               k_ref, ki_ref, vt_ref, qt_ref, qit_ref, wc_ref, *, tm):
    ct = ct_ref[...]
    st = st_ref[...]
    d_att = wuk_ref.shape[1]
    gr = lax.broadcasted_iota(jnp.int32, (d_att, d_att), 0) // ATT_HEAD_DIM
    gc = lax.broadcasted_iota(jnp.int32, (d_att, d_att), 1) // ATT_HEAD_DIM
    head_mean = jnp.where(gr == gc, 1.0 / ATT_HEAD_DIM, 0.0).astype(F32)

    cqn = _rms(cq_ref[...], cqg_ref[...]).astype(BF16)
    ckvn = _rms(ckv_ref[...], ckvg_ref[...]).astype(BF16)

    qt = _dot_nt(wuqt_ref[...], cqn)
    ms = _dot_sel_lhs(head_mean, qt * qt)
    qt = qt * lax.rsqrt(ms + RMS_EPS) * qg_ref[...]
    qt_ref[...] = (_rope(qt, ctt_ref[...], stt_ref[...], 0) * (ATT_HEAD_DIM ** -0.5 * LOG2_E)).astype(BF16)
    vt = _dot_nt(wuvt_ref[...], ckvn).astype(BF16)
    for pr in range(ATT_HEADS // 2):
        base = pr * V_PAIR_ROWS
        vt_ref[base:base + 2 * ATT_HEAD_DIM, :] = vt[pr * 2 * ATT_HEAD_DIM:(pr + 1) * 2 * ATT_HEAD_DIM, :]
        vt_ref[base + 2 * ATT_HEAD_DIM:base + V_PAIR_ROWS, :] = jnp.ones((V_ONES_ROWS, tm), BF16)

    k = jnp.dot(ckvn, wuk_ref[...], preferred_element_type=F32)
    ms = _dot_sel_rhs(k * k, head_mean)
    k_ref[...] = _rope(k * lax.rsqrt(ms + RMS_EPS) * kg_ref[...], ct, st, 1).astype(BF16)

    qit = _rope(_dot_nt(wqit_ref[...], cqn), ctt_ref[...], stt_ref[...], 0)
    misc = misc_ref[...]
    misc_t = misc.T
    wscale = IDX_HEADS ** -0.5 * IDX_DIM ** -0.5
    for j in range(tm // Q_BLOCK):
        cols = slice(j * Q_BLOCK, (j + 1) * Q_BLOCK)
        for h in range(IDX_HEADS):
            qit_ref[j, 0:IDX_DIM, h * Q_BLOCK:(h + 1) * Q_BLOCK] = (
                qit[h * IDX_DIM:(h + 1) * IDX_DIM, cols].astype(BF16))
        qit_ref[j, IDX_DIM:, :] = jnp.zeros((LANE - IDX_DIM, IDX_HEADS * Q_BLOCK), BF16)
        wrow = jnp.concatenate([misc_t[IDX_DIM + h:IDX_DIM + h + 1, cols] for h in range(IDX_HEADS)], axis=1)
        wc_ref[j] = jnp.broadcast_to(wrow * wscale, (SUBLANE, IDX_HEADS * Q_BLOCK))

    lane = lax.broadcasted_iota(jnp.int32, misc.shape, 1)
    km = jnp.where(lane < IDX_DIM, misc, 0.0)
    ms = jnp.sum(km * km, axis=-1, keepdims=True) * (1.0 / IDX_DIM)
    kin = km * lax.rsqrt(ms + RMS_EPS) * kig_ref[...]
    ki_ref[...] = _rope(kin, ct[:, :LANE], st[:, :LANE], 1).astype(BF16)


def _rope_lane_tables(length, width):
    half = ROPE_ROT // 2
    inv = ROPE_THETA ** (-jnp.arange(half, dtype=F32) * 2.0 / ROPE_ROT)
    ang = jnp.arange(length, dtype=F32)[:, None] * inv[None, :]
    cos, sin = jnp.cos(ang), jnp.sin(ang)
    ones = jnp.ones((length, ATT_HEAD_DIM - ROPE_ROT), F32)
    ct = jnp.concatenate([cos, cos, ones], axis=1)
    st = jnp.concatenate([-sin, sin, 0.0 * ones], axis=1)
    reps = width // ATT_HEAD_DIM
    return jnp.tile(ct, (1, reps)), jnp.tile(st, (1, reps))


def _dsa_prep(cq, ckv, misc, p, *, bsz):
    t, qr = cq.shape
    kvr = ckv.shape[1]
    tm = KEY_CHUNK
    length = t // bsz
    nt = length // tm
    d_att = ATT_HEADS * ATT_HEAD_DIM
    ct, st = _rope_lane_tables(length, d_att)
    wukv = p["w_ukv"].reshape(kvr, ATT_HEADS, 2, ATT_HEAD_DIM)
    wuk = wukv[:, :, 0, :].reshape(kvr, d_att).astype(BF16)
    wuvt = wukv[:, :, 1, :].reshape(kvr, d_att).T.astype(BF16)
    kig = jnp.concatenate([p["kidx_norm"], jnp.zeros((LANE - IDX_DIM,), F32)]).reshape(1, LANE)
    row = lambda c: pl.BlockSpec((tm, c), lambda i, b: (b * nt + i, 0))
    tab = pl.BlockSpec((tm, d_att), lambda i, b: (i, 0))
    tab_t = pl.BlockSpec((d_att, tm), lambda i, b: (0, i))
    nqb = tm // Q_BLOCK
    hq = IDX_HEADS * Q_BLOCK
    return pl.pallas_call(
        functools.partial(_prep_body, tm=tm),
        grid=(nt, bsz),
        in_specs=[row(qr), row(kvr), row(LANE), tab, tab, tab_t, tab_t, _const_spec((1, qr)),
                  _const_spec((1, kvr)), _const_spec((d_att, qr)), _const_spec((kvr, d_att)),
                  _const_spec((d_att, kvr)), _const_spec((d_att, 1)), _const_spec((1, d_att)),
                  _const_spec((IDX_HEADS * IDX_DIM, qr)), _const_spec((1, LANE))],
        out_specs=[row(d_att), row(LANE),
                   pl.BlockSpec((None, V_ROWS, tm), lambda i, b: (b * nt + i, 0, 0)),
                   pl.BlockSpec((None, d_att, tm), lambda i, b: (b * nt + i, 0, 0)),
                   pl.BlockSpec((nqb, LANE, hq), lambda i, b: (b * nt + i, 0, 0)),
                   pl.BlockSpec((nqb, SUBLANE, hq), lambda i, b: (b * nt + i, 0, 0))],
        out_shape=[jax.ShapeDtypeStruct((t, d_att), BF16),
                   jax.ShapeDtypeStruct((t, LANE), BF16),
                   jax.ShapeDtypeStruct((bsz * nt, V_ROWS, tm), BF16),
                   jax.ShapeDtypeStruct((bsz * nt, d_att, tm), BF16),
                   jax.ShapeDtypeStruct((t // Q_BLOCK, LANE, hq), BF16),
                   jax.ShapeDtypeStruct((t // Q_BLOCK, SUBLANE, hq), F32)],
        compiler_params=_cparams("parallel", "parallel"),
        name="dsa_prep",
    )(cq, ckv, misc, ct, st, ct.T, st.T, p["cq_norm"].reshape(1, qr), p["ckv_norm"].reshape(1, kvr),
      p["w_uq"].T.astype(BF16), wuk, wuvt, jnp.tile(p["q_norm"], ATT_HEADS).reshape(d_att, 1),
      jnp.tile(p["k_norm"], ATT_HEADS).reshape(1, d_att), p["w_qidx"].T.astype(BF16), kig)


INT_MIN = -2 ** 31
F32_TINY = 2.0 ** -126
F32_LOWEST = -3.4028234663852886e38
SEARCH_VALUE_STEPS = 18
SEARCH_CAP = 80


def _order_key_to_float(u):
    sk = u ^ jnp.int32(INT_MIN)
    fb = sk ^ ((sk >> 31) & jnp.int32(0x7FFFFFFF))
    return lax.bitcast_convert_type(fb, F32)


def _float_to_order_key(f):
    b = lax.bitcast_convert_type(f, jnp.int32)
    return (b ^ ((b >> 31) & jnp.int32(0x7FFFFFFF))) ^ jnp.int32(INT_MIN)


def _fold_rows(x, op):
    parts = [x[i * SUBLANE:(i + 1) * SUBLANE, :] for i in range(x.shape[0] // SUBLANE)]
    while len(parts) > 1:
        parts = [op(parts[i], parts[i + 1]) for i in range(0, len(parts), 2)]
    return parts[0]


def _dsa_body(qit_ref, wc_ref, qt_ref, ki_ref, k_ref, vt_ref, o_ref,
              sc_scr, lg_scr, qbd_scr, oacc_scr, *, n_sel, nk):
    kc = KEY_CHUNK
    qb = (nk - 1) * (kc // Q_BLOCK) + pl.program_id(1)
    qpos = qb * Q_BLOCK + lax.broadcasted_iota(jnp.int32, (1, Q_BLOCK), 1)
    hq = IDX_HEADS * Q_BLOCK
    k_f = jnp.float32(n_sel)

    def chunk(c):
        return pl.ds(c * kc, kc)

    def over_chunks(body, carry):
        for c in range(nk):
            carry = body(c, carry)
        return carry

    qit = qit_ref[...]
    wc = wc_ref[0:1, :]

    def score_chunk(c, carry):
        amax, c_pos, c_nn = carry
        s4 = jnp.dot(ki_ref[chunk(c), :], qit, preferred_element_type=F32)
        s4 = jnp.maximum(s4, 0.0) * wc
        s = (s4[:, 0:Q_BLOCK] + s4[:, Q_BLOCK:2 * Q_BLOCK]) + (s4[:, 2 * Q_BLOCK:3 * Q_BLOCK] + s4[:, 3 * Q_BLOCK:])
        kpos = c * kc + lax.broadcasted_iota(jnp.int32, (kc, Q_BLOCK), 0)
        sm = jnp.where(kpos <= qpos, s, -jnp.inf)
        sc_scr[chunk(c), :] = sm
        return (jnp.maximum(amax, _fold_rows(jnp.abs(s), jnp.maximum)),
                c_pos + _fold_rows(jnp.where(sm >= F32_TINY, 1.0, 0.0), jnp.add),
                c_nn + _fold_rows(jnp.where(sm >= 0.0, 1.0, 0.0), jnp.add))

    zeros8 = jnp.zeros((SUBLANE, Q_BLOCK), F32)
    amax, c_pos, c_nn = over_chunks(score_chunk, (zeros8, zeros8, zeros8))
    amax = jnp.max(amax, axis=0, keepdims=True)
    c_pos = jnp.sum(c_pos, axis=0, keepdims=True)
    c_nn = jnp.sum(c_nn, axis=0, keepdims=True)

    def count_ge(*thrs):
        def body(c, accs):
            blk = sc_scr[chunk(c), :]
            return tuple(a + _fold_rows(jnp.where(blk >= t, 1.0, 0.0), jnp.add) for a, t in zip(accs, thrs))
        accs = over_chunks(body, tuple(jnp.zeros((SUBLANE, Q_BLOCK), F32) for _ in thrs))
        return tuple(jnp.sum(a, axis=0, keepdims=True) for a in accs)

    n_valid = (qpos + 1).astype(F32)
    take_all = n_valid <= k_f
    zero_tie = (c_pos < k_f) & (c_nn >= k_f)
    pos = c_pos >= k_f
    hi_top = _order_key_to_float(_float_to_order_key(amax) + 1)
    lo = jnp.where(take_all, F32_LOWEST, jnp.where(zero_tie, 0.0, jnp.where(pos, F32_TINY, -amax)))
    hi = jnp.where(take_all, jnp.inf, jnp.where(zero_tie, F32_TINY, jnp.where(pos, hi_top, 0.0)))
    cnt_lo = jnp.where(take_all | (~zero_tie & ~pos), n_valid, jnp.where(zero_tie, c_nn, c_pos))
    cnt_hi = jnp.where(take_all | (~zero_tie & pos), 0.0, jnp.where(zero_tie, c_pos, c_nn))
    settled = take_all | zero_tie

    def search_step(st, on_values):
        lk, hk, lo, hi, cnt_lo, cnt_hi = st
        midk = lk + ((hk - lk) >> 1)
        mid = _order_key_to_float(midk)
        if on_values:
            vmid = lo + (hi - lo) * 0.5
            inside = (vmid > lo) & (vmid < hi)
            mid = jnp.where(inside, vmid, mid)
            midk = jnp.where(inside, _float_to_order_key(vmid), midk)
        cnt, = count_ge(mid)
        ge = cnt >= k_f
        return (jnp.where(ge, midk, lk), jnp.where(ge, hk, midk), jnp.where(ge, mid, lo), jnp.where(ge, hi, mid),
                jnp.where(ge, cnt, cnt_lo), jnp.where(ge, cnt_hi, cnt))

    def search_cond(carry):
        it, (lk, hk, _, _, cnt_lo, _) = carry
        open_ = jnp.logical_not(settled | (cnt_lo == k_f) | (hk - lk <= 1))
        return (it < SEARCH_CAP) & (jnp.sum(jnp.where(open_, 1.0, 0.0)) > 0.0)

    def search_pair(carry):
        it, st = carry
        return it + 1, search_step(search_step(st, False), False)

    st = (_float_to_order_key(lo), _float_to_order_key(hi), lo, hi, cnt_lo, cnt_hi)
    st = lax.fori_loop(0, SEARCH_VALUE_STEPS, lambda _, s: search_step(s, True), st)
    _, (_, _, lo, hi, cnt_lo, cnt_hi) = lax.while_loop(search_cond, search_pair, (jnp.int32(0), st))
    need = k_f - cnt_hi

    sub = Q_BLOCK
    ri = lax.broadcasted_iota(jnp.int32, (sub, sub), 0)
    ci = lax.broadcasted_iota(jnp.int32, (sub, sub), 1)
    before = (ri > ci).astype(BF16)

    qbd_scr[...] = jnp.zeros(qbd_scr.shape, BF16)
    for h in range(ATT_HEADS):
        rows = slice(h * ATT_HEAD_DIM, (h + 1) * ATT_HEAD_DIM)
        qbd_scr[rows, h * Q_BLOCK:(h + 1) * Q_BLOCK] = qt_ref[rows, :]
    qbd = qbd_scr[...]

    def logit_chunk(c, carry):
        m, seen = carry
        for j in range(kc // sub):
            rows = pl.ds(c * kc + j * sub, sub)
            blk = sc_scr[rows, :]
            in_hi = blk >= hi
            tie = (blk >= lo) & jnp.logical_not(in_hi)
            tie_f = jnp.where(tie, 1.0, 0.0)
            rank = jnp.dot(before, tie_f.astype(BF16), preferred_element_type=F32) + seen
            sel = in_hi | (tie & (rank < need))
            seen = seen + jnp.sum(tie_f, axis=0, keepdims=True)
            b = jnp.where(sel, 0.0, -jnp.inf)
            lg = jnp.dot(k_ref[rows, :], qbd, preferred_element_type=F32) + jnp.concatenate([b] * ATT_HEADS, axis=1)
            lg_scr[rows, :] = lg
            m = jnp.maximum(m, jnp.max(lg, axis=0, keepdims=True))
        return m, seen

    m, _ = over_chunks(logit_chunk, (jnp.full((1, hq), -jnp.inf, F32), jnp.zeros((1, Q_BLOCK), F32)))
    oacc_scr[...] = jnp.zeros(oacc_scr.shape, F32)

    pair_cols = 2 * Q_BLOCK

    def value_chunk(c, carry):
        depth = 2 * Q_BLOCK
        for pr in range(ATT_HEADS // 2):
            rws = slice(pr * V_PAIR_ROWS, (pr + 1) * V_PAIR_ROWS)
            cls = slice(pr * pair_cols, (pr + 1) * pair_cols)
            acc = oacc_scr[rws, :]
            for j in range(kc // depth):
                rows = pl.ds(c * kc + j * depth, depth)
                p = jnp.exp2(lg_scr[rows, cls] - m[:, cls]).astype(BF16)
                acc = acc + jnp.dot(vt_ref[c, rws, j * depth:(j + 1) * depth], p, preferred_element_type=F32)
            oacc_scr[rws, :] = acc
        return carry

    over_chunks(value_chunk, 0)
    outs = []
    for h in range(ATT_HEADS):
        base = (h // 2) * V_PAIR_ROWS
        cols = slice((h % 2) * Q_BLOCK, (h % 2 + 1) * Q_BLOCK)
        ones_row = base + 2 * ATT_HEAD_DIM
        ssum = oacc_scr[ones_row:ones_row + 1, cols]
        outs.append(oacc_scr[base + (h % 2) * ATT_HEAD_DIM:base + (h % 2 + 1) * ATT_HEAD_DIM, cols] / ssum)
    o_ref[...] = jnp.concatenate(outs, axis=0).T


def _dsa(k, ki, vt, qt, qit, wc, *, bsz):
    t, d_att = k.shape
    length = t // bsz
    nb = length // Q_BLOCK
    kc = KEY_CHUNK
    nkc = length // kc
    hq = IDX_HEADS * Q_BLOCK
    per_q = kc // Q_BLOCK
    vt = vt.reshape(bsz, nkc, V_ROWS, kc)
    ki = ki.reshape(bsz, length, LANE)
    k = k.reshape(bsz, length, d_att)
    outs = []
    for g in range(nkc):
        nk = g + 1
        qblk = lambda b, i, g=g: b * nb + g * per_q + i
        outs.append(pl.pallas_call(
            functools.partial(_dsa_body, n_sel=min(TOP_K, length // 4), nk=nk),
            grid=(bsz, per_q),
            in_specs=[pl.BlockSpec((None, LANE, hq), lambda b, i, f=qblk: (f(b, i), 0, 0)),
                      pl.BlockSpec((None, SUBLANE, hq), lambda b, i, f=qblk: (f(b, i), 0, 0)),
                      pl.BlockSpec((None, d_att, Q_BLOCK), lambda b, i, g=g: (b * nkc + g, 0, i)),
                      pl.BlockSpec((None, nk * kc, LANE), lambda b, i: (b, 0, 0)),
                      pl.BlockSpec((None, nk * kc, d_att), lambda b, i: (b, 0, 0)),
                      pl.BlockSpec((None, nk, V_ROWS, kc), lambda b, i: (b, 0, 0, 0))],
            out_specs=pl.BlockSpec((None, Q_BLOCK, d_att), lambda b, i: (b, i, 0)),
            out_shape=jax.ShapeDtypeStruct((bsz, kc, d_att), F32),
            scratch_shapes=[pltpu.VMEM((nk * kc, Q_BLOCK), F32), pltpu.VMEM((nk * kc, hq), F32),
                            pltpu.VMEM((d_att, hq), BF16), pltpu.VMEM((V_ROWS, 2 * Q_BLOCK), F32)],
            compiler_params=_cparams("parallel", "arbitrary"),
            name=f"dsa{nk}",
        )(qit, wc, qt, ki, k, vt))
    return jnp.stack(outs, axis=1).reshape(t, d_att)


def _pad_cols(w, width):
    return jnp.concatenate([w, jnp.zeros((w.shape[0], width - w.shape[1]), w.dtype)], axis=1)


def kernel(x, norm_ffn1, ffn1_w13, ffn1_w2, norm_mix, w_in, ssd_conv_w, ssd_conv_b, ssd_dt_bias, ssd_a_log, ssd_d, ssd_norm, cq_norm, ckv_norm, w_uq, w_ukv, q_norm, k_norm, w_qidx, kidx_norm, lru_conv_w, lru_conv_b, lru_wa, lru_ba, lru_wi, lru_bi, lru_lambda, w_out, norm_ffn2, ffn2_w13, ffn2_w2):
    bsz, length, d = x.shape
    depth = w_in.shape[0]
    d_ssd = ssd_norm.shape[1]
    conv_ch = ssd_conv_b.shape[1]
    heads = ssd_d.shape[1]
    q_rank, kv_rank = cq_norm.shape[1], ckv_norm.shape[1]
    d_lru = lru_lambda.shape[1]
    assert length % KEY_CHUNK == 0 and (bsz * length) % 512 == 0, (bsz, length)
    assert d_ssd == heads * SSD_HEAD_DIM and conv_ch == d_ssd + 2 * SSD_GROUPS * SSD_STATE, (d_ssd, conv_ch)
    assert w_uq.shape[2] == ATT_HEADS * ATT_HEAD_DIM and w_qidx.shape[2] == IDX_HEADS * IDX_DIM
    assert heads <= LANE and IDX_DIM + IDX_HEADS <= LANE and ssd_conv_w.shape[1] == CONV_W == lru_conv_w.shape[1]
    sizes = [d_ssd, conv_ch, heads, q_rank, kv_rank, IDX_DIM, IDX_HEADS, d_lru, d_lru]
    offs = [0]
    for s in sizes:
        offs.append(offs[-1] + s)
    widths = (d_ssd, conv_ch, LANE, q_rank, kv_rank, LANE, d_lru, d_lru)
    zero_rows = lambda n: jnp.zeros((n, d), BF16)

    xt = x.reshape(bsz * length, d)
    w13_1, w2_1, w13_2, w2_2 = (w.astype(BF16) for w in (ffn1_w13, ffn1_w2, ffn2_w13, ffn2_w2))
    for l in range(depth):
        wt = w_in[l].T.astype(BF16)
        w_pad = jnp.concatenate(
            [wt[offs[0]:offs[3]], zero_rows(LANE - heads), wt[offs[3]:offs[7]],
             zero_rows(LANE - IDX_DIM - IDX_HEADS), wt[offs[7]:offs[9]]], axis=0)
        xt, z, xbc, dtp, cq, ckv, misc, xl, gl = _ffn(xt, norm_ffn1[l], w13_1, w2_1, l,
                                                      proj=(norm_mix[l], w_pad, widths))
        y_ssd = _ssd(z, xbc, dtp, ssd_conv_w[l], ssd_conv_b[l], ssd_dt_bias[l], ssd_a_log[l], ssd_d[l],
                     ssd_norm[l], bsz=bsz)
        y_lru = _lru(xl, gl, lru_conv_w[l], lru_conv_b[l], lru_wa[l], lru_ba[l], lru_wi[l], lru_bi[l],
                     lru_lambda[l], bsz=bsz)
        prm = dict(cq_norm=cq_norm[l], ckv_norm=ckv_norm[l], w_uq=w_uq[l], w_ukv=w_ukv[l], q_norm=q_norm[l],
                   k_norm=k_norm[l], w_qidx=w_qidx[l], kidx_norm=kidx_norm[l])
        k, ki, vt, qt, qit, wc = _dsa_prep(cq, ckv, misc, prm, bsz=bsz)
        y_att = _dsa(k, ki, vt, qt, qit, wc, bsz=bsz)
        xt = _ffn(xt, norm_ffn2[l], w13_2, w2_2, l, mixed=([y_ssd, y_att, y_lru], w_out[l].astype(BF16)))
    return xt.reshape(bsz, length, d)
```

```python
import functools

import jax
import jax.numpy as jnp
from jax import lax
from jax.experimental import pallas as pl
from jax.experimental.pallas import tpu as pltpu

F32 = jnp.float32
BF16 = jnp.bfloat16

RMS_EPS = 1e-6
LOG2_E = 1.4426950408889634
SSD_HEAD_DIM = 64
SSD_GROUPS = 2
SSD_STATE = 128
SSD_CHUNK = 128
ATT_HEADS = 4
ATT_HEAD_DIM = 64
IDX_HEADS = 4
IDX_DIM = 64
TOP_K = 256
Q_BLOCK = 128
ROPE_THETA = 500000.0
ROPE_ROT = 16
LRU_C = 8.0
CONV_W = 4
LANE = 128
SUBLANE = 8
KEY_CHUNK = 512
V_ONES_ROWS = 16
V_PAIR_ROWS = 2 * ATT_HEAD_DIM + V_ONES_ROWS
V_ROWS = (ATT_HEADS // 2) * V_PAIR_ROWS
VMEM_LIMIT = 56 * 1024 * 1024


def _cparams(*sem):
    return pltpu.CompilerParams(dimension_semantics=sem, vmem_limit_bytes=VMEM_LIMIT)


def _const_spec(shape):
    nd = len(shape)
    return pl.BlockSpec(shape, lambda *_: (0,) * nd, pipeline_mode=pl.Buffered(1))


def _rms(x, g):
    ms = jnp.mean(x * x, axis=-1, keepdims=True)
    return x * lax.rsqrt(ms + RMS_EPS) * g


def _softplus(x):
    return jnp.maximum(x, 0.0) + jnp.log1p(jnp.exp(-jnp.abs(x)))


def _silu(x):
    return x * jax.nn.sigmoid(x)


def _split3(x):
    hi = x.astype(BF16)
    r = x - hi.astype(F32)
    mid = r.astype(BF16)
    return hi, mid, (r - mid.astype(F32)).astype(BF16)


def _dot_sel_rhs(x, sel):
    sel = sel.astype(BF16)
    return sum(jnp.dot(p, sel, preferred_element_type=F32) for p in _split3(x))


def _dot_sel_lhs(sel, x):
    sel = sel.astype(BF16)
    return sum(jnp.dot(sel, p, preferred_element_type=F32) for p in _split3(x))


def _ffn_body(*refs, d_ff, tf, n_mix, proj_widths):
    refs = list(refs)
    x = refs.pop(0)[...]
    if n_mix:
        y_refs = [refs.pop(0) for _ in range(n_mix)]
        wo_ref = refs.pop(0)
        off = 0
        for y_ref in y_refs:
            wd = y_ref.shape[1]
            x = x + jnp.dot(y_ref[...].astype(BF16), wo_ref[off:off + wd, :], preferred_element_type=F32)
            off += wd
    g_ref, w13_ref, w2_ref = refs.pop(0), refs.pop(0), refs.pop(0)
    if proj_widths:
        gp_ref, wi_ref = refs.pop(0), refs.pop(0)
    o_ref = refs.pop(0)

    h = _rms(x, g_ref[...]).astype(BF16)
    acc = jnp.zeros(x.shape, F32)
    for j in range(d_ff // tf):
        g = jnp.dot(h, w13_ref[:, j * tf:(j + 1) * tf], preferred_element_type=F32)
        u = jnp.dot(h, w13_ref[:, d_ff + j * tf:d_ff + (j + 1) * tf], preferred_element_type=F32)
        a = (_silu(g) * u).astype(BF16)
        acc = acc + jnp.dot(a, w2_ref[j * tf:(j + 1) * tf, :], preferred_element_type=F32)
    x = x + 0.5 * acc
    o_ref[...] = x

    if proj_widths:
        hp = _rms(x, gp_ref[...]).astype(BF16)
        off = 0
        for p_ref, wd in zip(refs, proj_widths):
            p_ref[...] = _dot_nt(hp, wi_ref[off:off + wd, :])
            off += wd


def _ffn(x, g, w13, w2, layer, *, mixed=None, proj=None, tm=512, tf=256):
    t, d = x.shape
    d_ff = w2.shape[1]
    row = lambda c: pl.BlockSpec((tm, c), lambda i: (i, 0))
    layer_spec = lambda r, c: pl.BlockSpec((None, r, c), lambda i: (layer, 0, 0), pipeline_mode=pl.Buffered(1))
    args, in_specs = [x], [row(d)]
    if mixed is not None:
        ys, w_out = mixed
        args += [*ys, w_out]
        in_specs += [row(y.shape[1]) for y in ys] + [_const_spec(w_out.shape)]
    args += [g.reshape(1, d), w13, w2]
    in_specs += [_const_spec((1, d)), layer_spec(d, 2 * d_ff), layer_spec(d_ff, d)]
    out_specs, out_shape, widths = [row(d)], [jax.ShapeDtypeStruct((t, d), F32)], ()
    if proj is not None:
        gp, w_in, widths = proj
        args += [gp.reshape(1, d), w_in]
        in_specs += [_const_spec((1, d)), _const_spec(w_in.shape)]
        out_specs += [row(wd) for wd in widths]
        out_shape += [jax.ShapeDtypeStruct((t, wd), F32) for wd in widths]
    outs = pl.pallas_call(
        functools.partial(_ffn_body, d_ff=d_ff, tf=tf, n_mix=len(mixed[0]) if mixed is not None else 0,
                          proj_widths=tuple(widths)),
        grid=(t // tm,),
        in_specs=in_specs,
        out_specs=out_specs,
        out_shape=out_shape,
        compiler_params=_cparams("parallel"),
        name="ffn",
    )(*args)
    return outs if proj is not None else outs[0]


def _causal_conv(x_ref, cw_ref, cb_ref, tail_scr, step, rows):
    @pl.when(step == 0)
    def _():
        tail_scr[...] = jnp.zeros(tail_scr.shape, F32)

    x = x_ref[...]
    tail = tail_scr[...]
    row = lax.broadcasted_iota(jnp.int32, tail.shape, 0)
    y = cb_ref[...] + cw_ref[CONV_W - 1:CONV_W, :] * x
    for w in range(CONV_W - 1):
        back = CONV_W - 1 - w
        xr = pltpu.roll(x, back, 0)
        top = jnp.where(row < back, pltpu.roll(tail, back, 0), xr[0:SUBLANE, :])
        y = y + cw_ref[w:w + 1, :] * jnp.concatenate([top, xr[SUBLANE:, :]], axis=0)
    tail_scr[...] = x[rows - SUBLANE:rows, :]
    return y


def _ssd_body(z_ref, xbc_ref, dtp_ref, cw_ref, cb_ref, dtb_ref, alog_ref, dsk_ref, ng_ref,
              y_ref, tail_scr, st_scr, *, q, d_ssd):
    step = pl.program_id(1)
    n = SSD_STATE
    gw = d_ssd // SSD_GROUPS
    rpg = gw // SSD_HEAD_DIM

    @pl.when(step == 0)
    def _():
        st_scr[...] = jnp.zeros(st_scr.shape, F32)

    xbc = _silu(_causal_conv(xbc_ref, cw_ref, cb_ref, tail_scr, step, q))
    xs = xbc[:, :d_ssd]
    bm = xbc[:, d_ssd:d_ssd + SSD_GROUPS * n]
    cm = xbc[:, d_ssd + SSD_GROUPS * n:]

    dt = _softplus(dtp_ref[...] + dtb_ref[...])
    adt = dt * (-jnp.exp(alog_ref[...]))
    ri = lax.broadcasted_iota(jnp.int32, (q, q), 0)
    ci = lax.broadcasted_iota(jnp.int32, (q, q), 1)
    causal = ri >= ci
    acum = _dot_sel_lhs(causal.astype(F32), adt)
    er = lax.broadcasted_iota(jnp.int32, (LANE, d_ssd), 0)
    ec = lax.broadcasted_iota(jnp.int32, (LANE, d_ssd), 1)
    expand = (ec // SSD_HEAD_DIM == er).astype(F32)
    a_x = _dot_sel_rhs(acum, expand)
    dt_x = _dot_sel_rhs(dt, expand)
    acum_t = acum.T
    a_last = a_x[q - 1:q, :]
    exp_a = jnp.exp(a_x)
    xdt = xs * dt_x
    xsw = xdt * jnp.exp(a_last - a_x)
    cdec = jnp.exp(a_last)

    ys = []
    for g in range(SSD_GROUPS):
        bm_g = bm[:, g * n:(g + 1) * n]
        cm_g = cm[:, g * n:(g + 1) * n].astype(BF16)
        cb = lax.dot_general(cm_g, bm_g.astype(BF16), (((1,), (1,)), ((), ())), preferred_element_type=F32)
        yd = []
        for r in range(rpg):
            h = g * rpg + r
            seg = acum[:, h:h + 1] - acum_t[h:h + 1, :]
            decay = jnp.exp(jnp.where(causal, seg, -jnp.inf))
            m = (cb * decay).astype(BF16)
            xh = xdt[:, h * SSD_HEAD_DIM:(h + 1) * SSD_HEAD_DIM].astype(BF16)
            yd.append(jnp.dot(m, xh, preferred_element_type=F32))
        yd = jnp.concatenate(yd, axis=1)
        gs = slice(g * gw, (g + 1) * gw)
        prev = st_scr[g]
        yoff = jnp.dot(cm_g, prev.astype(BF16), preferred_element_type=F32) * exp_a[:, gs]
        st = jnp.dot(bm_g.T.astype(BF16), xsw[:, gs].astype(BF16), preferred_element_type=F32)
        st_scr[g] = prev * cdec[:, gs] + st
        ys.append(yd + yoff + dsk_ref[:, gs] * xs[:, gs])

    z = z_ref[...]
    for g in range(SSD_GROUPS):
        gs = slice(g * gw, (g + 1) * gw)
        y_ref[:, gs] = _rms(ys[g] * _silu(z[:, gs]), ng_ref[:, gs])


def _ssd(z, xbc, dtp, conv_w, conv_b, dt_bias, a_log, d_skip, norm_g, *, bsz):
    t, d_ssd = z.shape
    cch = xbc.shape[1]
    q = SSD_CHUNK
    nc = t // bsz // q
    heads = d_ssd // SSD_HEAD_DIM
    pad = lambda v, fill: jnp.concatenate([v, jnp.full((LANE - heads,), fill, F32)]).reshape(1, LANE)
    row = lambda c: pl.BlockSpec((q, c), lambda b, i: (b * nc + i, 0))
    return pl.pallas_call(
        functools.partial(_ssd_body, q=q, d_ssd=d_ssd),
        grid=(bsz, nc),
        in_specs=[row(d_ssd), row(cch), row(LANE), _const_spec((CONV_W, cch)), _const_spec((1, cch)),
                  _const_spec((1, LANE)), _const_spec((1, LANE)), _const_spec((1, d_ssd)),
                  _const_spec((1, d_ssd))],
        out_specs=row(d_ssd),
        out_shape=jax.ShapeDtypeStruct((t, d_ssd), F32),
        scratch_shapes=[pltpu.VMEM((SUBLANE, cch), F32),
                        pltpu.VMEM((SSD_GROUPS, SSD_STATE, d_ssd // SSD_GROUPS), F32)],
        compiler_params=_cparams("parallel", "arbitrary"),
        name="ssd",
    )(z, xbc, dtp, conv_w, conv_b.reshape(1, cch), pad(dt_bias, 0.0), pad(a_log, 0.0),
      jnp.repeat(d_skip, SSD_HEAD_DIM).reshape(1, d_ssd), norm_g.reshape(1, d_ssd))


def _lru_body(xl_ref, gl_ref, cw_ref, cb_ref, wa_ref, ba_ref, wi_ref, bi_ref, lam_ref,
              y_ref, tail_scr, h_scr, *, tl):
    step = pl.program_id(1)

    @pl.when(step == 0)
    def _():
        h_scr[...] = jnp.zeros(h_scr.shape, F32)

    xr = _causal_conv(xl_ref, cw_ref, cb_ref, tail_scr, step, tl)
    xrb = xr.astype(BF16)
    r = jax.nn.sigmoid(jnp.dot(xrb, wa_ref[...], preferred_element_type=F32) + ba_ref[...])
    i = jax.nn.sigmoid(jnp.dot(xrb, wi_ref[...], preferred_element_type=F32) + bi_ref[...])
    log_a = -LRU_C * r * _softplus(-lam_ref[...])
    a = jnp.exp(log_a)
    b = jnp.sqrt(-jnp.tanh(log_a) * (a * a + 1.0)) * (i * xr)
    rows = lax.broadcasted_iota(jnp.int32, a.shape, 0) % SUBLANE
    s = 1
    while s < SUBLANE:
        keep = rows >= s
        a_sh = jnp.where(keep, pltpu.roll(a, s, 0), 1.0)
        b_sh = jnp.where(keep, pltpu.roll(b, s, 0), 0.0)
        b = a * b_sh + b
        a = a * a_sh
        s *= 2
    h = h_scr[0:1, :]
    gate = jax.nn.gelu(gl_ref[...])
    for g in range(tl // SUBLANE):
        grp = slice(g * SUBLANE, (g + 1) * SUBLANE)
        hs = a[grp, :] * h + b[grp, :]
        y_ref[grp, :] = hs * gate[grp, :]
        h = hs[SUBLANE - 1:SUBLANE, :]
    h_scr[...] = jnp.broadcast_to(h, h_scr.shape)


def _block_diag(w):
    nb, bw, _ = w.shape
    out = jnp.zeros((nb * bw, nb * bw), w.dtype)
    for k in range(nb):
        out = out.at[k * bw:(k + 1) * bw, k * bw:(k + 1) * bw].set(w[k])
    return out


def _lru(xl, gl, conv_w, conv_b, wa, ba, wi, bi, lam, *, bsz, tl=512):
    t, dl = xl.shape
    nt = t // bsz // tl
    row = pl.BlockSpec((tl, dl), lambda b, i: (b * nt + i, 0))
    vec = _const_spec((1, dl))
    return pl.pallas_call(
        functools.partial(_lru_body, tl=tl),
        grid=(bsz, nt),
        in_specs=[row, row, _const_spec((CONV_W, dl)), vec, _const_spec((dl, dl)), vec,
                  _const_spec((dl, dl)), vec, vec],
        out_specs=row,
        out_shape=jax.ShapeDtypeStruct((t, dl), F32),
        scratch_shapes=[pltpu.VMEM((SUBLANE, dl), F32), pltpu.VMEM((SUBLANE, dl), F32)],
        compiler_params=_cparams("parallel", "arbitrary"),
        name="rglru",
    )(xl, gl, conv_w, conv_b.reshape(1, dl), _block_diag(wa).astype(BF16), ba.reshape(1, dl),
      _block_diag(wi).astype(BF16), bi.reshape(1, dl), lam.reshape(1, dl))


def _rope(x, ct, st, axis):
    width = x.shape[axis]
    fm = lax.broadcasted_iota(jnp.int32, x.shape, axis) % ATT_HEAD_DIM
    half = ROPE_ROT // 2
    partner = jnp.where(fm < half, pltpu.roll(x, width - half, axis), pltpu.roll(x, half, axis))
    return x * ct + partner * st


def _dot_nt(a, b):
    return lax.dot_general(a, b, (((1,), (1,)), ((), ())), preferred_element_type=F32)


def _prep_body(cq_ref, ckv_ref, misc_ref, ct_ref, st_ref, ctt_ref, stt_ref, cqg_ref, ckvg_ref, wuqt_ref,
               wuk_ref, wuvt_ref, qg_ref, kg_ref, wqit_ref, kig_ref,
               k_ref, ki_ref, vt_ref, qt_ref, qit_ref, wc_ref, *, tm):
    ct = ct_ref[...]
    st = st_ref[...]
    d_att = wuk_ref.shape[1]
    gr = lax.broadcasted_iota(jnp.int32, (d_att, d_att), 0) // ATT_HEAD_DIM
    gc = lax.broadcasted_iota(jnp.int32, (d_att, d_att), 1) // ATT_HEAD_DIM
    head_mean = jnp.where(gr == gc, 1.0 / ATT_HEAD_DIM, 0.0).astype(F32)

    cqn = _rms(cq_ref[...], cqg_ref[...]).astype(BF16)
    ckvn = _rms(ckv_ref[...], ckvg_ref[...]).astype(BF16)

    qt = _dot_nt(wuqt_ref[...], cqn)
    ms = _dot_sel_lhs(head_mean, qt * qt)
    qt = qt * lax.rsqrt(ms + RMS_EPS) * qg_ref[...]
    qt_ref[...] = (_rope(qt, ctt_ref[...], stt_ref[...], 0) * (ATT_HEAD_DIM ** -0.5 * LOG2_E)).astype(BF16)
    vt = _dot_nt(wuvt_ref[...], ckvn).astype(BF16)
    for pr in range(ATT_HEADS // 2):
        base = pr * V_PAIR_ROWS
        vt_ref[base:base + 2 * ATT_HEAD_DIM, :] = vt[pr * 2 * ATT_HEAD_DIM:(pr + 1) * 2 * ATT_HEAD_DIM, :]
        vt_ref[base + 2 * ATT_HEAD_DIM:base + V_PAIR_ROWS, :] = jnp.ones((V_ONES_ROWS, tm), BF16)

    k = jnp.dot(ckvn, wuk_ref[...], preferred_element_type=F32)
    ms = _dot_sel_rhs(k * k, head_mean)
    k_ref[...] = _rope(k * lax.rsqrt(ms + RMS_EPS) * kg_ref[...], ct, st, 1).astype(BF16)

    qit = _rope(_dot_nt(wqit_ref[...], cqn), ctt_ref[...], stt_ref[...], 0)
    misc = misc_ref[...]
    misc_t = misc.T
    wscale = IDX_HEADS ** -0.5 * IDX_DIM ** -0.5
    for j in range(tm // Q_BLOCK):
        cols = slice(j * Q_BLOCK, (j + 1) * Q_BLOCK)
        for h in range(IDX_HEADS):
            qit_ref[j, 0:IDX_DIM, h * Q_BLOCK:(h + 1) * Q_BLOCK] = (
                qit[h * IDX_DIM:(h + 1) * IDX_DIM, cols].astype(BF16))
        qit_ref[j, IDX_DIM:, :] = jnp.zeros((LANE - IDX_DIM, IDX_HEADS * Q_BLOCK), BF16)
        wrow = jnp.concatenate([misc_t[IDX_DIM + h:IDX_DIM + h + 1, cols] for h in range(IDX_HEADS)], axis=1)
        wc_ref[j] = jnp.broadcast_to(wrow * wscale, (SUBLANE, IDX_HEADS * Q_BLOCK))

    lane = lax.broadcasted_iota(jnp.int32, misc.shape, 1)
    km = jnp.where(lane < IDX_DIM, misc, 0.0)
    ms = jnp.sum(km * km, axis=-1, keepdims=True) * (1.0 / IDX_DIM)
    kin = km * lax.rsqrt(ms + RMS_EPS) * kig_ref[...]
    ki_ref[...] = _rope(kin, ct[:, :LANE], st[:, :LANE], 1).astype(BF16)


def _rope_lane_tables(length, width):
    half = ROPE_ROT // 2
    inv = ROPE_THETA ** (-jnp.arange(half, dtype=F32) * 2.0 / ROPE_ROT)
    ang = jnp.arange(length, dtype=F32)[:, None] * inv[None, :]
    cos, sin = jnp.cos(ang), jnp.sin(ang)
    ones = jnp.ones((length, ATT_HEAD_DIM - ROPE_ROT), F32)
    ct = jnp.concatenate([cos, cos, ones], axis=1)
    st = jnp.concatenate([-sin, sin, 0.0 * ones], axis=1)
    reps = width // ATT_HEAD_DIM
    return jnp.tile(ct, (1, reps)), jnp.tile(st, (1, reps))


def _dsa_prep(cq, ckv, misc, p, *, bsz):
    t, qr = cq.shape
    kvr = ckv.shape[1]
    tm = KEY_CHUNK
    length = t // bsz
    nt = length // tm
    d_att = ATT_HEADS * ATT_HEAD_DIM
    ct, st = _rope_lane_tables(length, d_att)
    wukv = p["w_ukv"].reshape(kvr, ATT_HEADS, 2, ATT_HEAD_DIM)
    wuk = wukv[:, :, 0, :].reshape(kvr, d_att).astype(BF16)
    wuvt = wukv[:, :, 1, :].reshape(kvr, d_att).T.astype(BF16)
    kig = jnp.concatenate([p["kidx_norm"], jnp.zeros((LANE - IDX_DIM,), F32)]).reshape(1, LANE)
    row = lambda c: pl.BlockSpec((tm, c), lambda i, b: (b * nt + i, 0))
    tab = pl.BlockSpec((tm, d_att), lambda i, b: (i, 0))
    tab_t = pl.BlockSpec((d_att, tm), lambda i, b: (0, i))
    nqb = tm // Q_BLOCK
    hq = IDX_HEADS * Q_BLOCK
    return pl.pallas_call(
        functools.partial(_prep_body, tm=tm),
        grid=(nt, bsz),
        in_specs=[row(qr), row(kvr), row(LANE), tab, tab, tab_t, tab_t, _const_spec((1, qr)),
                  _const_spec((1, kvr)), _const_spec((d_att, qr)), _const_spec((kvr, d_att)),
                  _const_spec((d_att, kvr)), _const_spec((d_att, 1)), _const_spec((1, d_att)),
                  _const_spec((IDX_HEADS * IDX_DIM, qr)), _const_spec((1, LANE))],
        out_specs=[row(d_att), row(LANE),
                   pl.BlockSpec((None, V_ROWS, tm), lambda i, b: (b * nt + i, 0, 0)),
                   pl.BlockSpec((None, d_att, tm), lambda i, b: (b * nt + i, 0, 0)),
                   pl.BlockSpec((nqb, LANE, hq), lambda i, b: (b * nt + i, 0, 0)),
                   pl.BlockSpec((nqb, SUBLANE, hq), lambda i, b: (b * nt + i, 0, 0))],
        out_shape=[jax.ShapeDtypeStruct((t, d_att), BF16),
                   jax.ShapeDtypeStruct((t, LANE), BF16),
                   jax.ShapeDtypeStruct((bsz * nt, V_ROWS, tm), BF16),
                   jax.ShapeDtypeStruct((bsz * nt, d_att, tm), BF16),
                   jax.ShapeDtypeStruct((t // Q_BLOCK, LANE, hq), BF16),
                   jax.ShapeDtypeStruct((t // Q_BLOCK, SUBLANE, hq), F32)],
        compiler_params=_cparams("parallel", "parallel"),
        name="dsa_prep",
    )(cq, ckv, misc, ct, st, ct.T, st.T, p["cq_norm"].reshape(1, qr), p["ckv_norm"].reshape(1, kvr),
      p["w_uq"].T.astype(BF16), wuk, wuvt, jnp.tile(p["q_norm"], ATT_HEADS).reshape(d_att, 1),
      jnp.tile(p["k_norm"], ATT_HEADS).reshape(1, d_att), p["w_qidx"].T.astype(BF16), kig)


INT_MIN = -2 ** 31
F32_TINY = 2.0 ** -126
F32_LOWEST = -3.4028234663852886e38
SEARCH_VALUE_STEPS = 18
SEARCH_CAP = 80


def _order_key_to_float(u):
    sk = u ^ jnp.int32(INT_MIN)
    fb = sk ^ ((sk >> 31) & jnp.int32(0x7FFFFFFF))
    return lax.bitcast_convert_type(fb, F32)


def _float_to_order_key(f):
    b = lax.bitcast_convert_type(f, jnp.int32)
    return (b ^ ((b >> 31) & jnp.int32(0x7FFFFFFF))) ^ jnp.int32(INT_MIN)


def _fold_rows(x, op):
    parts = [x[i * SUBLANE:(i + 1) * SUBLANE, :] for i in range(x.shape[0] // SUBLANE)]
    while len(parts) > 1:
        parts = [op(parts[i], parts[i + 1]) for i in range(0, len(parts), 2)]
    return parts[0]


def _dsa_body(qit_ref, wc_ref, qt_ref, ki_ref, k_ref, vt_ref, *rest, n_sel, nk, chained):
    o_ref, sc_scr, lg_scr, qbd_scr, oacc_scr = rest[1:] if chained else rest
    kc = KEY_CHUNK
    qb = (nk - 1) * (kc // Q_BLOCK) + pl.program_id(1)
    qpos = qb * Q_BLOCK + lax.broadcasted_iota(jnp.int32, (1, Q_BLOCK), 1)
    hq = IDX_HEADS * Q_BLOCK
    k_f = jnp.float32(n_sel)

    def chunk(c):
        return pl.ds(c * kc, kc)

    def over_chunks(body, carry):
        for c in range(nk):
            carry = body(c, carry)
        return carry

    qit = qit_ref[...]
    wc = wc_ref[0:1, :]

    def score_chunk(c, carry):
        amax, c_pos, c_nn = carry
        s4 = jnp.dot(ki_ref[chunk(c), :], qit, preferred_element_type=F32)
        s4 = jnp.maximum(s4, 0.0) * wc
        s = (s4[:, 0:Q_BLOCK] + s4[:, Q_BLOCK:2 * Q_BLOCK]) + (s4[:, 2 * Q_BLOCK:3 * Q_BLOCK] + s4[:, 3 * Q_BLOCK:])
        kpos = c * kc + lax.broadcasted_iota(jnp.int32, (kc, Q_BLOCK), 0)
        sm = jnp.where(kpos <= qpos, s, -jnp.inf)
        sc_scr[chunk(c), :] = sm
        return (jnp.maximum(amax, _fold_rows(jnp.abs(s), jnp.maximum)),
                c_pos + _fold_rows(jnp.where(sm >= F32_TINY, 1.0, 0.0), jnp.add),
                c_nn + _fold_rows(jnp.where(sm >= 0.0, 1.0, 0.0), jnp.add))

    zeros8 = jnp.zeros((SUBLANE, Q_BLOCK), F32)
    amax, c_pos, c_nn = over_chunks(score_chunk, (zeros8, zeros8, zeros8))
    amax = jnp.max(amax, axis=0, keepdims=True)
    c_pos = jnp.sum(c_pos, axis=0, keepdims=True)
    c_nn = jnp.sum(c_nn, axis=0, keepdims=True)

    def count_ge(*thrs):
        def body(c, accs):
            blk = sc_scr[chunk(c), :]
            return tuple(a + _fold_rows(jnp.where(blk >= t, 1.0, 0.0), jnp.add) for a, t in zip(accs, thrs))
        accs = over_chunks(body, tuple(jnp.zeros((SUBLANE, Q_BLOCK), F32) for _ in thrs))
        return tuple(jnp.sum(a, axis=0, keepdims=True) for a in accs)

    n_valid = (qpos + 1).astype(F32)
    take_all = n_valid <= k_f
    zero_tie = (c_pos < k_f) & (c_nn >= k_f)
    pos = c_pos >= k_f
    hi_top = _order_key_to_float(_float_to_order_key(amax) + 1)
    lo = jnp.where(take_all, F32_LOWEST, jnp.where(zero_tie, 0.0, jnp.where(pos, F32_TINY, -amax)))
    hi = jnp.where(take_all, jnp.inf, jnp.where(zero_tie, F32_TINY, jnp.where(pos, hi_top, 0.0)))
    cnt_lo = jnp.where(take_all | (~zero_tie & ~pos), n_valid, jnp.where(zero_tie, c_nn, c_pos))
    cnt_hi = jnp.where(take_all | (~zero_tie & pos), 0.0, jnp.where(zero_tie, c_pos, c_nn))
    settled = take_all | zero_tie

    def search_step(st, on_values):
        lk, hk, lo, hi, cnt_lo, cnt_hi = st
        midk = lk + ((hk - lk) >> 1)
        mid = _order_key_to_float(midk)
        if on_values:
            vmid = lo + (hi - lo) * 0.5
            inside = (vmid > lo) & (vmid < hi)
            mid = jnp.where(inside, vmid, mid)
            midk = jnp.where(inside, _float_to_order_key(vmid), midk)
        cnt, = count_ge(mid)
        ge = cnt >= k_f
        return (jnp.where(ge, midk, lk), jnp.where(ge, hk, midk), jnp.where(ge, mid, lo), jnp.where(ge, hi, mid),
                jnp.where(ge, cnt, cnt_lo), jnp.where(ge, cnt_hi, cnt))

    def search_cond(carry):
        it, (lk, hk, _, _, cnt_lo, _) = carry
        open_ = jnp.logical_not(settled | (cnt_lo == k_f) | (hk - lk <= 1))
        return (it < SEARCH_CAP) & (jnp.sum(jnp.where(open_, 1.0, 0.0)) > 0.0)

    def search_pair(carry):
        it, st = carry
        return it + 1, search_step(search_step(st, False), False)

    st = (_float_to_order_key(lo), _float_to_order_key(hi), lo, hi, cnt_lo, cnt_hi)
    st = lax.fori_loop(0, SEARCH_VALUE_STEPS, lambda _, s: search_step(s, True), st)
    _, (_, _, lo, hi, cnt_lo, cnt_hi) = lax.while_loop(search_cond, search_pair, (jnp.int32(0), st))
    need = k_f - cnt_hi

    sub = Q_BLOCK
    ri = lax.broadcasted_iota(jnp.int32, (sub, sub), 0)
    ci = lax.broadcasted_iota(jnp.int32, (sub, sub), 1)
    before = (ri > ci).astype(BF16)

    qbd_scr[...] = jnp.zeros(qbd_scr.shape, BF16)
    for h in range(ATT_HEADS):
        rows = slice(h * ATT_HEAD_DIM, (h + 1) * ATT_HEAD_DIM)
        qbd_scr[rows, h * Q_BLOCK:(h + 1) * Q_BLOCK] = qt_ref[rows, :]
    qbd = qbd_scr[...]

    def logit_chunk(c, carry):
        m, seen = carry
        for j in range(kc // sub):
            rows = pl.ds(c * kc + j * sub, sub)
            blk = sc_scr[rows, :]
            in_hi = blk >= hi
            tie = (blk >= lo) & jnp.logical_not(in_hi)
            tie_f = jnp.where(tie, 1.0, 0.0)
            rank = jnp.dot(before, tie_f.astype(BF16), preferred_element_type=F32) + seen
            sel = in_hi | (tie & (rank < need))
            seen = seen + jnp.sum(tie_f, axis=0, keepdims=True)
            b = jnp.where(sel, 0.0, -jnp.inf)
            lg = jnp.dot(k_ref[rows, :], qbd, preferred_element_type=F32) + jnp.concatenate([b] * ATT_HEADS, axis=1)
            lg_scr[rows, :] = lg
            m = jnp.maximum(m, jnp.max(lg, axis=0, keepdims=True))
        return m, seen

    m, _ = over_chunks(logit_chunk, (jnp.full((1, hq), -jnp.inf, F32), jnp.zeros((1, Q_BLOCK), F32)))
    oacc_scr[...] = jnp.zeros(oacc_scr.shape, F32)

    pair_cols = 2 * Q_BLOCK

    def value_chunk(c, carry):
        depth = 2 * Q_BLOCK
        for pr in range(ATT_HEADS // 2):
            rws = slice(pr * V_PAIR_ROWS, (pr + 1) * V_PAIR_ROWS)
            cls = slice(pr * pair_cols, (pr + 1) * pair_cols)
            acc = oacc_scr[rws, :]
            for j in range(kc // depth):
                rows = pl.ds(c * kc + j * depth, depth)
                p = jnp.exp2(lg_scr[rows, cls] - m[:, cls]).astype(BF16)
                acc = acc + jnp.dot(vt_ref[c, rws, j * depth:(j + 1) * depth], p, preferred_element_type=F32)
            oacc_scr[rws, :] = acc
        return carry

    over_chunks(value_chunk, 0)
    outs = []
    for h in range(ATT_HEADS):
        base = (h // 2) * V_PAIR_ROWS
        cols = slice((h % 2) * Q_BLOCK, (h % 2 + 1) * Q_BLOCK)
        ones_row = base + 2 * ATT_HEAD_DIM
        ssum = oacc_scr[ones_row:ones_row + 1, cols]
        outs.append(oacc_scr[base + (h % 2) * ATT_HEAD_DIM:base + (h % 2 + 1) * ATT_HEAD_DIM, cols] / ssum)
    o_ref[...] = jnp.concatenate(outs, axis=0).T


def _dsa(k, ki, vt, qt, qit, wc, *, bsz):
    t, d_att = k.shape
    length = t // bsz
    nb = length // Q_BLOCK
    kc = KEY_CHUNK
    nkc = length // kc
    hq = IDX_HEADS * Q_BLOCK
    per_q = kc // Q_BLOCK
    vt = vt.reshape(bsz, nkc, V_ROWS, kc)
    ki = ki.reshape(bsz, length, LANE)
    k = k.reshape(bsz, length, d_att)
    out = None
    for g in range(nkc):
        nk = g + 1
        qblk = lambda b, i, g=g: b * nb + g * per_q + i
        in_specs = [pl.BlockSpec((None, LANE, hq), lambda b, i, f=qblk: (f(b, i), 0, 0)),
                    pl.BlockSpec((None, SUBLANE, hq), lambda b, i, f=qblk: (f(b, i), 0, 0)),
                    pl.BlockSpec((None, d_att, Q_BLOCK), lambda b, i, g=g: (b * nkc + g, 0, i)),
                    pl.BlockSpec((None, nk * kc, LANE), lambda b, i: (b, 0, 0)),
                    pl.BlockSpec((None, nk * kc, d_att), lambda b, i: (b, 0, 0)),
                    pl.BlockSpec((None, nk, V_ROWS, kc), lambda b, i: (b, 0, 0, 0))]
        args = [qit, wc, qt, ki, k, vt]
        chained = out is not None
        if chained:
            in_specs.append(pl.BlockSpec(memory_space=pl.ANY))
            args.append(out)
        out = pl.pallas_call(
            functools.partial(_dsa_body, n_sel=min(TOP_K, length // 4), nk=nk, chained=chained),
            grid=(bsz, per_q),
            in_specs=in_specs,
            out_specs=pl.BlockSpec((None, Q_BLOCK, d_att), lambda b, i, g=g: (b, g * per_q + i, 0)),
            out_shape=jax.ShapeDtypeStruct((bsz, length, d_att), F32),
            input_output_aliases={len(args) - 1: 0} if chained else {},
            scratch_shapes=[pltpu.VMEM((nk * kc, Q_BLOCK), F32), pltpu.VMEM((nk * kc, hq), F32),
                            pltpu.VMEM((d_att, hq), BF16), pltpu.VMEM((V_ROWS, 2 * Q_BLOCK), F32)],
            compiler_params=_cparams("parallel", "arbitrary"),
            name=f"dsa{nk}",
        )(*args)
    return out.reshape(t, d_att)


def _pad_cols(w, width):
    return jnp.concatenate([w, jnp.zeros((w.shape[0], width - w.shape[1]), w.dtype)], axis=1)


def kernel(x, norm_ffn1, ffn1_w13, ffn1_w2, norm_mix, w_in, ssd_conv_w, ssd_conv_b, ssd_dt_bias, ssd_a_log, ssd_d, ssd_norm, cq_norm, ckv_norm, w_uq, w_ukv, q_norm, k_norm, w_qidx, kidx_norm, lru_conv_w, lru_conv_b, lru_wa, lru_ba, lru_wi, lru_bi, lru_lambda, w_out, norm_ffn2, ffn2_w13, ffn2_w2):
    bsz, length, d = x.shape
    depth = w_in.shape[0]
    d_ssd = ssd_norm.shape[1]
    conv_ch = ssd_conv_b.shape[1]
    heads = ssd_d.shape[1]
    q_rank, kv_rank = cq_norm.shape[1], ckv_norm.shape[1]
    d_lru = lru_lambda.shape[1]
    assert length % KEY_CHUNK == 0 and (bsz * length) % 512 == 0, (bsz, length)
    assert d_ssd == heads * SSD_HEAD_DIM and conv_ch == d_ssd + 2 * SSD_GROUPS * SSD_STATE, (d_ssd, conv_ch)
    assert w_uq.shape[2] == ATT_HEADS * ATT_HEAD_DIM and w_qidx.shape[2] == IDX_HEADS * IDX_DIM
    assert heads <= LANE and IDX_DIM + IDX_HEADS <= LANE and ssd_conv_w.shape[1] == CONV_W == lru_conv_w.shape[1]
    sizes = [d_ssd, conv_ch, heads, q_rank, kv_rank, IDX_DIM, IDX_HEADS, d_lru, d_lru]
    offs = [0]
    for s in sizes:
        offs.append(offs[-1] + s)
    widths = (d_ssd, conv_ch, LANE, q_rank, kv_rank, LANE, d_lru, d_lru)
    zero_rows = lambda n: jnp.zeros((n, d), BF16)

    xt = x.reshape(bsz * length, d)
    w13_1, w2_1, w13_2, w2_2 = (w.astype(BF16) for w in (ffn1_w13, ffn1_w2, ffn2_w13, ffn2_w2))
    for l in range(depth):
        wt = w_in[l].T.astype(BF16)
        w_pad = jnp.concatenate(
            [wt[offs[0]:offs[3]], zero_rows(LANE - heads), wt[offs[3]:offs[7]],
             zero_rows(LANE - IDX_DIM - IDX_HEADS), wt[offs[7]:offs[9]]], axis=0)
        xt, z, xbc, dtp, cq, ckv, misc, xl, gl = _ffn(xt, norm_ffn1[l], w13_1, w2_1, l,
                                                      proj=(norm_mix[l], w_pad, widths))
        y_ssd = _ssd(z, xbc, dtp, ssd_conv_w[l], ssd_conv_b[l], ssd_dt_bias[l], ssd_a_log[l], ssd_d[l],
                     ssd_norm[l], bsz=bsz)
        y_lru = _lru(xl, gl, lru_conv_w[l], lru_conv_b[l], lru_wa[l], lru_ba[l], lru_wi[l], lru_bi[l],
                     lru_lambda[l], bsz=bsz)
        prm = dict(cq_norm=cq_norm[l], ckv_norm=ckv_norm[l], w_uq=w_uq[l], w_ukv=w_ukv[l], q_norm=q_norm[l],
                   k_norm=k_norm[l], w_qidx=w_qidx[l], kidx_norm=kidx_norm[l])
        k, ki, vt, qt, qit, wc = _dsa_prep(cq, ckv, misc, prm, bsz=bsz)
        y_att = _dsa(k, ki, vt, qt, qit, wc, bsz=bsz)
        xt = _ffn(xt, norm_ffn2[l], w13_2, w2_2, l, mixed=([y_ssd, y_att, y_lru], w_out[l].astype(BF16)))
    return xt.reshape(bsz, length, d)
```

```python
import functools

import jax
import jax.numpy as jnp
from jax import lax
from jax.experimental import pallas as pl
from jax.experimental.pallas import tpu as pltpu

F32 = jnp.float32
BF16 = jnp.bfloat16

RMS_EPS = 1e-6
LOG2_E = 1.4426950408889634
SSD_HEAD_DIM = 64
SSD_GROUPS = 2
SSD_STATE = 128
SSD_CHUNK = 128
ATT_HEADS = 4
ATT_HEAD_DIM = 64
IDX_HEADS = 4
IDX_DIM = 64
TOP_K = 256
Q_BLOCK = 128
ROPE_THETA = 500000.0
ROPE_ROT = 16
LRU_C = 8.0
CONV_W = 4
LANE = 128
SUBLANE = 8
KEY_CHUNK = 512
V_ONES_ROWS = 16
V_PAIR_ROWS = 2 * ATT_HEAD_DIM + V_ONES_ROWS
V_ROWS = (ATT_HEADS // 2) * V_PAIR_ROWS
VMEM_LIMIT = 56 * 1024 * 1024


def _cparams(*sem):
    return pltpu.CompilerParams(dimension_semantics=sem, vmem_limit_bytes=VMEM_LIMIT)


def _const_spec(shape):
    nd = len(shape)
    return pl.BlockSpec(shape, lambda *_: (0,) * nd, pipeline_mode=pl.Buffered(1))


def _rms(x, g):
    ms = jnp.mean(x * x, axis=-1, keepdims=True)
    return x * lax.rsqrt(ms + RMS_EPS) * g


def _softplus(x):
    return jnp.maximum(x, 0.0) + jnp.log1p(jnp.exp(-jnp.abs(x)))


def _silu(x):
    return x * jax.nn.sigmoid(x)


def _split3(x):
    hi = x.astype(BF16)
    r = x - hi.astype(F32)
    mid = r.astype(BF16)
    return hi, mid, (r - mid.astype(F32)).astype(BF16)


def _dot_sel_rhs(x, sel):
    sel = sel.astype(BF16)
    return sum(jnp.dot(p, sel, preferred_element_type=F32) for p in _split3(x))


def _dot_sel_lhs(sel, x):
    sel = sel.astype(BF16)
    return sum(jnp.dot(sel, p, preferred_element_type=F32) for p in _split3(x))


def _ffn_body(*refs, d_ff, tf, n_mix, proj_widths):
    refs = list(refs)
    x = refs.pop(0)[...]
    if n_mix:
        y_refs = [refs.pop(0) for _ in range(n_mix)]
        wo_ref = refs.pop(0)
        off = 0
        for y_ref in y_refs:
            wd = y_ref.shape[1]
            x = x + jnp.dot(y_ref[...].astype(BF16), wo_ref[off:off + wd, :], preferred_element_type=F32)
            off += wd
    g_ref, w13_ref, w2_ref = refs.pop(0), refs.pop(0), refs.pop(0)
    if proj_widths:
        gp_ref, wi_ref = refs.pop(0), refs.pop(0)
    o_ref = refs.pop(0)

    h = _rms(x, g_ref[...]).astype(BF16)
    acc = jnp.zeros(x.shape, F32)
    for j in range(d_ff // tf):
        g = jnp.dot(h, w13_ref[:, j * tf:(j + 1) * tf], preferred_element_type=F32)
        u = jnp.dot(h, w13_ref[:, d_ff + j * tf:d_ff + (j + 1) * tf], preferred_element_type=F32)
        a = (_silu(g) * u).astype(BF16)
        acc = acc + jnp.dot(a, w2_ref[j * tf:(j + 1) * tf, :], preferred_element_type=F32)
    x = x + 0.5 * acc
    o_ref[...] = x

    if proj_widths:
        hp = _rms(x, gp_ref[...]).astype(BF16)
        off = 0
        for p_ref, wd in zip(refs, proj_widths):
            p_ref[...] = _dot_nt(hp, wi_ref[off:off + wd, :])
            off += wd


def _ffn(x, g, w13, w2, layer, *, mixed=None, proj=None, tm=512, tf=256):
    t, d = x.shape
    d_ff = w2.shape[1]
    row = lambda c: pl.BlockSpec((tm, c), lambda i: (i, 0))
    layer_spec = lambda r, c: pl.BlockSpec((None, r, c), lambda i: (layer, 0, 0), pipeline_mode=pl.Buffered(1))
    args, in_specs = [x], [row(d)]
    if mixed is not None:
        ys, w_out = mixed
        args += [*ys, w_out]
        in_specs += [row(y.shape[1]) for y in ys] + [_const_spec(w_out.shape)]
    args += [g.reshape(1, d), w13, w2]
    in_specs += [_const_spec((1, d)), layer_spec(d, 2 * d_ff), layer_spec(d_ff, d)]
    out_specs, out_shape, widths = [row(d)], [jax.ShapeDtypeStruct((t, d), F32)], ()
    if proj is not None:
        gp, w_in, widths = proj
        args += [gp.reshape(1, d), w_in]
        in_specs += [_const_spec((1, d)), _const_spec(w_in.shape)]
        out_specs += [row(wd) for wd in widths]
        out_shape += [jax.ShapeDtypeStruct((t, wd), F32) for wd in widths]
    outs = pl.pallas_call(
        functools.partial(_ffn_body, d_ff=d_ff, tf=tf, n_mix=len(mixed[0]) if mixed is not None else 0,
                          proj_widths=tuple(widths)),
        grid=(t // tm,),
        in_specs=in_specs,
        out_specs=out_specs,
        out_shape=out_shape,
        compiler_params=_cparams("parallel"),
        name="ffn",
    )(*args)
    return outs if proj is not None else outs[0]


def _causal_conv(x_ref, cw_ref, cb_ref, tail_scr, step, rows):
    @pl.when(step == 0)
    def _():
        tail_scr[...] = jnp.zeros(tail_scr.shape, F32)

    x = x_ref[...]
    tail = tail_scr[...]
    row = lax.broadcasted_iota(jnp.int32, tail.shape, 0)
    y = cb_ref[...] + cw_ref[CONV_W - 1:CONV_W, :] * x
    for w in range(CONV_W - 1):
        back = CONV_W - 1 - w
        xr = pltpu.roll(x, back, 0)
        top = jnp.where(row < back, pltpu.roll(tail, back, 0), xr[0:SUBLANE, :])
        y = y + cw_ref[w:w + 1, :] * jnp.concatenate([top, xr[SUBLANE:, :]], axis=0)
    tail_scr[...] = x[rows - SUBLANE:rows, :]
    return y


def _ssd_body(z_ref, xbc_ref, dtp_ref, cw_ref, cb_ref, dtb_ref, alog_ref, dsk_ref, ng_ref, expand_ref,
              y_ref, tail_scr, st_scr, *, q, d_ssd):
    step = pl.program_id(1)
    n = SSD_STATE
    gw = d_ssd // SSD_GROUPS
    rpg = gw // SSD_HEAD_DIM

    @pl.when(step == 0)
    def _():
        st_scr[...] = jnp.zeros(st_scr.shape, F32)

    xbc = _silu(_causal_conv(xbc_ref, cw_ref, cb_ref, tail_scr, step, q))
    xs = xbc[:, :d_ssd]
    bm = xbc[:, d_ssd:d_ssd + SSD_GROUPS * n]
    cm = xbc[:, d_ssd + SSD_GROUPS * n:]

    dt = _softplus(dtp_ref[...] + dtb_ref[...])
    adt = dt * (-jnp.exp(alog_ref[...]))
    ri = lax.broadcasted_iota(jnp.int32, (q, q), 0)
    ci = lax.broadcasted_iota(jnp.int32, (q, q), 1)
    causal = ri >= ci
    acum = _dot_sel_lhs(causal.astype(F32), adt)
    expand = expand_ref[...]
    a_x = _dot_sel_rhs(acum, expand)
    dt_x = _dot_sel_rhs(dt, expand)
    acum_t = acum.T
    a_last = a_x[q - 1:q, :]
    exp_a = jnp.exp(a_x)
    xdt = xs * dt_x
    xsw = xdt * jnp.exp(a_last - a_x)
    cdec = jnp.exp(a_last)

    ys = []
    for g in range(SSD_GROUPS):
        bm_g = bm[:, g * n:(g + 1) * n]
        cm_g = cm[:, g * n:(g + 1) * n].astype(BF16)
        cb = lax.dot_general(cm_g, bm_g.astype(BF16), (((1,), (1,)), ((), ())), preferred_element_type=F32)
        yd = []
        for r in range(rpg):
            h = g * rpg + r
            seg = acum[:, h:h + 1] - acum_t[h:h + 1, :]
            decay = jnp.exp(jnp.where(causal, seg, -jnp.inf))
            m = (cb * decay).astype(BF16)
            xh = xdt[:, h * SSD_HEAD_DIM:(h + 1) * SSD_HEAD_DIM].astype(BF16)
            yd.append(jnp.dot(m, xh, preferred_element_type=F32))
        yd = jnp.concatenate(yd, axis=1)
        gs = slice(g * gw, (g + 1) * gw)
        prev = st_scr[g]
        yoff = jnp.dot(cm_g, prev.astype(BF16), preferred_element_type=F32) * exp_a[:, gs]
        st = jnp.dot(bm_g.T.astype(BF16), xsw[:, gs].astype(BF16), preferred_element_type=F32)
        st_scr[g] = prev * cdec[:, gs] + st
        ys.append(yd + yoff + dsk_ref[:, gs] * xs[:, gs])

    z = z_ref[...]
    for g in range(SSD_GROUPS):
        gs = slice(g * gw, (g + 1) * gw)
        y_ref[:, gs] = _rms(ys[g] * _silu(z[:, gs]), ng_ref[:, gs])


def _ssd(z, xbc, dtp, conv_w, conv_b, dt_bias, a_log, d_skip, norm_g, *, bsz):
    t, d_ssd = z.shape
    cch = xbc.shape[1]
    q = SSD_CHUNK
    nc = t // bsz // q
    heads = d_ssd // SSD_HEAD_DIM
    pad = lambda v, fill: jnp.concatenate([v, jnp.full((LANE - heads,), fill, F32)]).reshape(1, LANE)
    row = lambda c: pl.BlockSpec((q, c), lambda b, i: (b * nc + i, 0))
    expand = (jnp.arange(d_ssd)[None, :] // SSD_HEAD_DIM == jnp.arange(LANE)[:, None]).astype(BF16)
    return pl.pallas_call(
        functools.partial(_ssd_body, q=q, d_ssd=d_ssd),
        grid=(bsz, nc),
        in_specs=[row(d_ssd), row(cch), row(LANE), _const_spec((CONV_W, cch)), _const_spec((1, cch)),
                  _const_spec((1, LANE)), _const_spec((1, LANE)), _const_spec((1, d_ssd)),
                  _const_spec((1, d_ssd)), _const_spec((LANE, d_ssd))],
        out_specs=row(d_ssd),
        out_shape=jax.ShapeDtypeStruct((t, d_ssd), F32),
        scratch_shapes=[pltpu.VMEM((SUBLANE, cch), F32),
                        pltpu.VMEM((SSD_GROUPS, SSD_STATE, d_ssd // SSD_GROUPS), F32)],
        compiler_params=_cparams("parallel", "arbitrary"),
        name="ssd",
    )(z, xbc, dtp, conv_w, conv_b.reshape(1, cch), pad(dt_bias, 0.0), pad(a_log, 0.0),
      jnp.repeat(d_skip, SSD_HEAD_DIM).reshape(1, d_ssd), norm_g.reshape(1, d_ssd), expand)


def _lru_body(xl_ref, gl_ref, cw_ref, cb_ref, wa_ref, ba_ref, wi_ref, bi_ref, lam_ref,
              y_ref, tail_scr, h_scr, *, tl):
    step = pl.program_id(1)

    @pl.when(step == 0)
    def _():
        h_scr[...] = jnp.zeros(h_scr.shape, F32)

    xr = _causal_conv(xl_ref, cw_ref, cb_ref, tail_scr, step, tl)
    xrb = xr.astype(BF16)
    r = jax.nn.sigmoid(jnp.dot(xrb, wa_ref[...], preferred_element_type=F32) + ba_ref[...])
    i = jax.nn.sigmoid(jnp.dot(xrb, wi_ref[...], preferred_element_type=F32) + bi_ref[...])
    log_a = -LRU_C * r * _softplus(-lam_ref[...])
    a = jnp.exp(log_a)
    b = jnp.sqrt(-jnp.tanh(log_a) * (a * a + 1.0)) * (i * xr)
    rows = lax.broadcasted_iota(jnp.int32, a.shape, 0) % SUBLANE
    s = 1
    while s < SUBLANE:
        keep = rows >= s
        a_sh = jnp.where(keep, pltpu.roll(a, s, 0), 1.0)
        b_sh = jnp.where(keep, pltpu.roll(b, s, 0), 0.0)
        b = a * b_sh + b
        a = a * a_sh
        s *= 2
    h = h_scr[0:1, :]
    gate = jax.nn.gelu(gl_ref[...])
    for g in range(tl // SUBLANE):
        grp = slice(g * SUBLANE, (g + 1) * SUBLANE)
        hs = a[grp, :] * h + b[grp, :]
        y_ref[grp, :] = hs * gate[grp, :]
        h = hs[SUBLANE - 1:SUBLANE, :]
    h_scr[...] = jnp.broadcast_to(h, h_scr.shape)


def _block_diag(w):
    nb, bw, _ = w.shape
    out = jnp.zeros((nb * bw, nb * bw), w.dtype)
    for k in range(nb):
        out = out.at[k * bw:(k + 1) * bw, k * bw:(k + 1) * bw].set(w[k])
    return out


def _lru(xl, gl, conv_w, conv_b, wa, ba, wi, bi, lam, *, bsz, tl=512):
    t, dl = xl.shape
    nt = t // bsz // tl
    row = pl.BlockSpec((tl, dl), lambda b, i: (b * nt + i, 0))
    vec = _const_spec((1, dl))
    return pl.pallas_call(
        functools.partial(_lru_body, tl=tl),
        grid=(bsz, nt),
        in_specs=[row, row, _const_spec((CONV_W, dl)), vec, _const_spec((dl, dl)), vec,
                  _const_spec((dl, dl)), vec, vec],
        out_specs=row,
        out_shape=jax.ShapeDtypeStruct((t, dl), F32),
        scratch_shapes=[pltpu.VMEM((SUBLANE, dl), F32), pltpu.VMEM((SUBLANE, dl), F32)],
        compiler_params=_cparams("parallel", "arbitrary"),
        name="rglru",
    )(xl, gl, conv_w, conv_b.reshape(1, dl), _block_diag(wa).astype(BF16), ba.reshape(1, dl),
      _block_diag(wi).astype(BF16), bi.reshape(1, dl), lam.reshape(1, dl))


def _rope(x, ct, st, axis):
    width = x.shape[axis]
    fm = lax.broadcasted_iota(jnp.int32, x.shape, axis) % ATT_HEAD_DIM
    half = ROPE_ROT // 2
    partner = jnp.where(fm < half, pltpu.roll(x, width - half, axis), pltpu.roll(x, half, axis))
    return x * ct + partner * st


def _dot_nt(a, b):
    return lax.dot_general(a, b, (((1,), (1,)), ((), ())), preferred_element_type=F32)


def _prep_body(cq_ref, ckv_ref, misc_ref, ct_ref, st_ref, ctt_ref, stt_ref, cqg_ref, ckvg_ref, wuqt_ref,
               wuk_ref, wuvt_ref, qg_ref, kg_ref, wqit_ref, kig_ref, hm_ref,
               k_ref, ki_ref, vt_ref, qt_ref, qit_ref, wc_ref, *, tm):
    ct = ct_ref[...]
    st = st_ref[...]
    d_att = wuk_ref.shape[1]
    head_mean = hm_ref[...]

    cqn = _rms(cq_ref[...], cqg_ref[...]).astype(BF16)
    ckvn = _rms(ckv_ref[...], ckvg_ref[...]).astype(BF16)

    qt = _dot_nt(wuqt_ref[...], cqn)
    ms = _dot_sel_lhs(head_mean, qt * qt)
    qt = qt * lax.rsqrt(ms + RMS_EPS) * qg_ref[...]
    qt_ref[...] = (_rope(qt, ctt_ref[...], stt_ref[...], 0) * (ATT_HEAD_DIM ** -0.5 * LOG2_E)).astype(BF16)
    vt = _dot_nt(wuvt_ref[...], ckvn).astype(BF16)
    for pr in range(ATT_HEADS // 2):
        base = pr * V_PAIR_ROWS
        vt_ref[base:base + 2 * ATT_HEAD_DIM, :] = vt[pr * 2 * ATT_HEAD_DIM:(pr + 1) * 2 * ATT_HEAD_DIM, :]
        vt_ref[base + 2 * ATT_HEAD_DIM:base + V_PAIR_ROWS, :] = jnp.ones((V_ONES_ROWS, tm), BF16)

    k = jnp.dot(ckvn, wuk_ref[...], preferred_element_type=F32)
    ms = _dot_sel_rhs(k * k, head_mean)
    k_ref[...] = _rope(k * lax.rsqrt(ms + RMS_EPS) * kg_ref[...], ct, st, 1).astype(BF16)

    qit = _rope(_dot_nt(wqit_ref[...], cqn), ctt_ref[...], stt_ref[...], 0)
    misc = misc_ref[...]
    misc_t = misc.T
    wscale = IDX_HEADS ** -0.5 * IDX_DIM ** -0.5
    for j in range(tm // Q_BLOCK):
        cols = slice(j * Q_BLOCK, (j + 1) * Q_BLOCK)
        for h in range(IDX_HEADS):
            qit_ref[j, 0:IDX_DIM, h * Q_BLOCK:(h + 1) * Q_BLOCK] = (
                qit[h * IDX_DIM:(h + 1) * IDX_DIM, cols].astype(BF16))
        qit_ref[j, IDX_DIM:, :] = jnp.zeros((LANE - IDX_DIM, IDX_HEADS * Q_BLOCK), BF16)
        wrow = jnp.concatenate([misc_t[IDX_DIM + h:IDX_DIM + h + 1, cols] for h in range(IDX_HEADS)], axis=1)
        wc_ref[j] = jnp.broadcast_to(wrow * wscale, (SUBLANE, IDX_HEADS * Q_BLOCK))

    lane = lax.broadcasted_iota(jnp.int32, misc.shape, 1)
    km = jnp.where(lane < IDX_DIM, misc, 0.0)
    ms = jnp.sum(km * km, axis=-1, keepdims=True) * (1.0 / IDX_DIM)
    kin = km * lax.rsqrt(ms + RMS_EPS) * kig_ref[...]
    ki_ref[...] = _rope(kin, ct[:, :LANE], st[:, :LANE], 1).astype(BF16)


def _rope_lane_tables(length, width):
    half = ROPE_ROT // 2
    inv = ROPE_THETA ** (-jnp.arange(half, dtype=F32) * 2.0 / ROPE_ROT)
    ang = jnp.arange(length, dtype=F32)[:, None] * inv[None, :]
    cos, sin = jnp.cos(ang), jnp.sin(ang)
    ones = jnp.ones((length, ATT_HEAD_DIM - ROPE_ROT), F32)
    ct = jnp.concatenate([cos, cos, ones], axis=1)
    st = jnp.concatenate([-sin, sin, 0.0 * ones], axis=1)
    reps = width // ATT_HEAD_DIM
    return jnp.tile(ct, (1, reps)), jnp.tile(st, (1, reps))


def _dsa_prep(cq, ckv, misc, p, *, bsz):
    t, qr = cq.shape
    kvr = ckv.shape[1]
    tm = KEY_CHUNK
    length = t // bsz
    nt = length // tm
    d_att = ATT_HEADS * ATT_HEAD_DIM
    ct, st = _rope_lane_tables(length, d_att)
    wukv = p["w_ukv"].reshape(kvr, ATT_HEADS, 2, ATT_HEAD_DIM)
    wuk = wukv[:, :, 0, :].reshape(kvr, d_att).astype(BF16)
    wuvt = wukv[:, :, 1, :].reshape(kvr, d_att).T.astype(BF16)
    kig = jnp.concatenate([p["kidx_norm"], jnp.zeros((LANE - IDX_DIM,), F32)]).reshape(1, LANE)
    head_of = jnp.arange(d_att) // ATT_HEAD_DIM
    head_mean = jnp.where(head_of[:, None] == head_of[None, :], 1.0 / ATT_HEAD_DIM, 0.0).astype(BF16)
    row = lambda c: pl.BlockSpec((tm, c), lambda i, b: (b * nt + i, 0))
    tab = pl.BlockSpec((tm, d_att), lambda i, b: (i, 0))
    tab_t = pl.BlockSpec((d_att, tm), lambda i, b: (0, i))
    nqb = tm // Q_BLOCK
    hq = IDX_HEADS * Q_BLOCK
    return pl.pallas_call(
        functools.partial(_prep_body, tm=tm),
        grid=(nt, bsz),
        in_specs=[row(qr), row(kvr), row(LANE), tab, tab, tab_t, tab_t, _const_spec((1, qr)),
                  _const_spec((1, kvr)), _const_spec((d_att, qr)), _const_spec((kvr, d_att)),
                  _const_spec((d_att, kvr)), _const_spec((d_att, 1)), _const_spec((1, d_att)),
                  _const_spec((IDX_HEADS * IDX_DIM, qr)), _const_spec((1, LANE)), _const_spec((d_att, d_att))],
        out_specs=[row(d_att), row(LANE),
                   pl.BlockSpec((None, V_ROWS, tm), lambda i, b: (b * nt + i, 0, 0)),
                   pl.BlockSpec((None, d_att, tm), lambda i, b: (b * nt + i, 0, 0)),
                   pl.BlockSpec((nqb, LANE, hq), lambda i, b: (b * nt + i, 0, 0)),
                   pl.BlockSpec((nqb, SUBLANE, hq), lambda i, b: (b * nt + i, 0, 0))],
        out_shape=[jax.ShapeDtypeStruct((t, d_att), BF16),
                   jax.ShapeDtypeStruct((t, LANE), BF16),
                   jax.ShapeDtypeStruct((bsz * nt, V_ROWS, tm), BF16),
                   jax.ShapeDtypeStruct((bsz * nt, d_att, tm), BF16),
                   jax.ShapeDtypeStruct((t // Q_BLOCK, LANE, hq), BF16),
                   jax.ShapeDtypeStruct((t // Q_BLOCK, SUBLANE, hq), F32)],
        compiler_params=_cparams("parallel", "parallel"),
        name="dsa_prep",
    )(cq, ckv, misc, ct, st, ct.T, st.T, p["cq_norm"].reshape(1, qr), p["ckv_norm"].reshape(1, kvr),
      p["w_uq"].T.astype(BF16), wuk, wuvt, jnp.tile(p["q_norm"], ATT_HEADS).reshape(d_att, 1),
      jnp.tile(p["k_norm"], ATT_HEADS).reshape(1, d_att), p["w_qidx"].T.astype(BF16), kig, head_mean)


INT_MIN = -2 ** 31
F32_TINY = 2.0 ** -126
F32_LOWEST = -3.4028234663852886e38
SEARCH_VALUE_STEPS = 18
SEARCH_CAP = 80


def _order_key_to_float(u):
    sk = u ^ jnp.int32(INT_MIN)
    fb = sk ^ ((sk >> 31) & jnp.int32(0x7FFFFFFF))
    return lax.bitcast_convert_type(fb, F32)


def _float_to_order_key(f):
    b = lax.bitcast_convert_type(f, jnp.int32)
    return (b ^ ((b >> 31) & jnp.int32(0x7FFFFFFF))) ^ jnp.int32(INT_MIN)


def _fold_rows(x, op):
    parts = [x[i * SUBLANE:(i + 1) * SUBLANE, :] for i in range(x.shape[0] // SUBLANE)]
    while len(parts) > 1:
        parts = [op(parts[i], parts[i + 1]) for i in range(0, len(parts), 2)]
    return parts[0]


def _dsa_body(qit_ref, wc_ref, qt_ref, ki_ref, k_ref, vt_ref, *rest, n_sel, nk, chained):
    o_ref, sc_scr, lg_scr, qbd_scr, oacc_scr = rest[1:] if chained else rest
    kc = KEY_CHUNK
    qb = (nk - 1) * (kc // Q_BLOCK) + pl.program_id(1)
    qpos = qb * Q_BLOCK + lax.broadcasted_iota(jnp.int32, (1, Q_BLOCK), 1)
    hq = IDX_HEADS * Q_BLOCK
    k_f = jnp.float32(n_sel)

    def chunk(c):
        return pl.ds(c * kc, kc)

    def over_chunks(body, carry):
        for c in range(nk):
            carry = body(c, carry)
        return carry

    qit = qit_ref[...]
    wc = wc_ref[0:1, :]

    def score_chunk(c, carry):
        amax, c_pos, c_nn = carry
        s4 = jnp.dot(ki_ref[chunk(c), :], qit, preferred_element_type=F32)
        s4 = jnp.maximum(s4, 0.0) * wc
        s = (s4[:, 0:Q_BLOCK] + s4[:, Q_BLOCK:2 * Q_BLOCK]) + (s4[:, 2 * Q_BLOCK:3 * Q_BLOCK] + s4[:, 3 * Q_BLOCK:])
        kpos = c * kc + lax.broadcasted_iota(jnp.int32, (kc, Q_BLOCK), 0)
        sm = jnp.where(kpos <= qpos, s, -jnp.inf)
        sc_scr[chunk(c), :] = sm
        return (jnp.maximum(amax, _fold_rows(jnp.abs(s), jnp.maximum)),
                c_pos + _fold_rows(jnp.where(sm >= F32_TINY, 1.0, 0.0), jnp.add),
                c_nn + _fold_rows(jnp.where(sm >= 0.0, 1.0, 0.0), jnp.add))

    zeros8 = jnp.zeros((SUBLANE, Q_BLOCK), F32)
    amax, c_pos, c_nn = over_chunks(score_chunk, (zeros8, zeros8, zeros8))
    amax = jnp.max(amax, axis=0, keepdims=True)
    c_pos = jnp.sum(c_pos, axis=0, keepdims=True)
    c_nn = jnp.sum(c_nn, axis=0, keepdims=True)

    def count_ge(*thrs):
        def body(c, accs):
            blk = sc_scr[chunk(c), :]
            return tuple(a + _fold_rows(jnp.where(blk >= t, 1.0, 0.0), jnp.add) for a, t in zip(accs, thrs))
        accs = over_chunks(body, tuple(jnp.zeros((SUBLANE, Q_BLOCK), F32) for _ in thrs))
        return tuple(jnp.sum(a, axis=0, keepdims=True) for a in accs)

    n_valid = (qpos + 1).astype(F32)
    take_all = n_valid <= k_f
    zero_tie = (c_pos < k_f) & (c_nn >= k_f)
    pos = c_pos >= k_f
    hi_top = _order_key_to_float(_float_to_order_key(amax) + 1)
    lo = jnp.where(take_all, F32_LOWEST, jnp.where(zero_tie, 0.0, jnp.where(pos, F32_TINY, -amax)))
    hi = jnp.where(take_all, jnp.inf, jnp.where(zero_tie, F32_TINY, jnp.where(pos, hi_top, 0.0)))
    cnt_lo = jnp.where(take_all | (~zero_tie & ~pos), n_valid, jnp.where(zero_tie, c_nn, c_pos))
    cnt_hi = jnp.where(take_all | (~zero_tie & pos), 0.0, jnp.where(zero_tie, c_pos, c_nn))
    settled = take_all | zero_tie

    def search_step(st, on_values):
        lk, hk, lo, hi, cnt_lo, cnt_hi = st
        midk = lk + ((hk - lk) >> 1)
        mid = _order_key_to_float(midk)
        if on_values:
            vmid = lo + (hi - lo) * 0.5
            inside = (vmid > lo) & (vmid < hi)
            mid = jnp.where(inside, vmid, mid)
            midk = jnp.where(inside, _float_to_order_key(vmid), midk)
        cnt, = count_ge(mid)
        ge = cnt >= k_f
        return (jnp.where(ge, midk, lk), jnp.where(ge, hk, midk), jnp.where(ge, mid, lo), jnp.where(ge, hi, mid),
                jnp.where(ge, cnt, cnt_lo), jnp.where(ge, cnt_hi, cnt))

    def search_cond(carry):
        it, (lk, hk, _, _, cnt_lo, _) = carry
        open_ = jnp.logical_not(settled | (cnt_lo == k_f) | (hk - lk <= 1))
        return (it < SEARCH_CAP) & (jnp.sum(jnp.where(open_, 1.0, 0.0)) > 0.0)

    def search_pair(carry):
        it, st = carry
        return it + 1, search_step(search_step(st, False), False)

    st = (_float_to_order_key(lo), _float_to_order_key(hi), lo, hi, cnt_lo, cnt_hi)
    st = lax.fori_loop(0, SEARCH_VALUE_STEPS, lambda _, s: search_step(s, True), st)
    _, (_, _, lo, hi, cnt_lo, cnt_hi) = lax.while_loop(search_cond, search_pair, (jnp.int32(0), st))
    need = k_f - cnt_hi

    sub = LANE
    ri = lax.broadcasted_iota(jnp.int32, (sub, sub), 0)
    ci = lax.broadcasted_iota(jnp.int32, (sub, sub), 1)
    before = (ri > ci).astype(BF16)

    qbd_scr[...] = jnp.zeros(qbd_scr.shape, BF16)
    for h in range(ATT_HEADS):
        rows = slice(h * ATT_HEAD_DIM, (h + 1) * ATT_HEAD_DIM)
        qbd_scr[rows, h * Q_BLOCK:(h + 1) * Q_BLOCK] = qt_ref[rows, :]
    qbd = qbd_scr[...]

    def logit_chunk(c, carry):
        m, seen = carry
        for j in range(kc // sub):
            rows = pl.ds(c * kc + j * sub, sub)
            blk = sc_scr[rows, :]
            in_hi = blk >= hi
            tie = (blk >= lo) & jnp.logical_not(in_hi)
            tie_f = jnp.where(tie, 1.0, 0.0)
            rank = jnp.dot(before, tie_f.astype(BF16), preferred_element_type=F32) + seen
            sel = in_hi | (tie & (rank < need))
            seen = seen + jnp.sum(tie_f, axis=0, keepdims=True)
            b = jnp.where(sel, 0.0, -jnp.inf)
            lg = jnp.dot(k_ref[rows, :], qbd, preferred_element_type=F32) + jnp.concatenate([b] * ATT_HEADS, axis=1)
            lg_scr[rows, :] = lg
            m = jnp.maximum(m, jnp.max(lg, axis=0, keepdims=True))
        return m, seen

    m, _ = over_chunks(logit_chunk, (jnp.full((1, hq), -jnp.inf, F32), jnp.zeros((1, Q_BLOCK), F32)))
    oacc_scr[...] = jnp.zeros(oacc_scr.shape, F32)

    pair_cols = 2 * Q_BLOCK

    def value_chunk(c, carry):
        depth = 2 * LANE
        for pr in range(ATT_HEADS // 2):
            rws = slice(pr * V_PAIR_ROWS, (pr + 1) * V_PAIR_ROWS)
            cls = slice(pr * pair_cols, (pr + 1) * pair_cols)
            acc = oacc_scr[rws, :]
            for j in range(kc // depth):
                rows = pl.ds(c * kc + j * depth, depth)
                p = jnp.exp2(lg_scr[rows, cls] - m[:, cls]).astype(BF16)
                acc = acc + jnp.dot(vt_ref[c, rws, j * depth:(j + 1) * depth], p, preferred_element_type=F32)
            oacc_scr[rws, :] = acc
        return carry

    over_chunks(value_chunk, 0)
    outs = []
    for h in range(ATT_HEADS):
        base = (h // 2) * V_PAIR_ROWS
        cols = slice((h % 2) * Q_BLOCK, (h % 2 + 1) * Q_BLOCK)
        ones_row = base + 2 * ATT_HEAD_DIM
        ssum = oacc_scr[ones_row:ones_row + 1, cols]
        outs.append(oacc_scr[base + (h % 2) * ATT_HEAD_DIM:base + (h % 2 + 1) * ATT_HEAD_DIM, cols] / ssum)
    o_ref[...] = jnp.concatenate(outs, axis=0).T


def _dsa(k, ki, vt, qt, qit, wc, *, bsz):
    t, d_att = k.shape
    length = t // bsz
    nb = length // Q_BLOCK
    kc = KEY_CHUNK
    nkc = length // kc
    hq = IDX_HEADS * Q_BLOCK
    per_q = kc // Q_BLOCK
    vt = vt.reshape(bsz, nkc, V_ROWS, kc)
    ki = ki.reshape(bsz, length, LANE)
    k = k.reshape(bsz, length, d_att)
    out = None
    for g in range(nkc):
        nk = g + 1
        qblk = lambda b, i, g=g: b * nb + g * per_q + i
        in_specs = [pl.BlockSpec((None, LANE, hq), lambda b, i, f=qblk: (f(b, i), 0, 0)),
                    pl.BlockSpec((None, SUBLANE, hq), lambda b, i, f=qblk: (f(b, i), 0, 0)),
                    pl.BlockSpec((None, d_att, Q_BLOCK), lambda b, i, g=g: (b * nkc + g, 0, i)),
                    pl.BlockSpec((None, nk * kc, LANE), lambda b, i: (b, 0, 0)),
                    pl.BlockSpec((None, nk * kc, d_att), lambda b, i: (b, 0, 0)),
                    pl.BlockSpec((None, nk, V_ROWS, kc), lambda b, i: (b, 0, 0, 0))]
        args = [qit, wc, qt, ki, k, vt]
        chained = out is not None
        if chained:
            in_specs.append(pl.BlockSpec(memory_space=pl.ANY))
            args.append(out)
        out = pl.pallas_call(
            functools.partial(_dsa_body, n_sel=min(TOP_K, length // 4), nk=nk, chained=chained),
            grid=(bsz, per_q),
            in_specs=in_specs,
            out_specs=pl.BlockSpec((None, Q_BLOCK, d_att), lambda b, i, g=g: (b, g * per_q + i, 0)),
            out_shape=jax.ShapeDtypeStruct((bsz, length, d_att), F32),
            input_output_aliases={len(args) - 1: 0} if chained else {},
            scratch_shapes=[pltpu.VMEM((nk * kc, Q_BLOCK), F32), pltpu.VMEM((nk * kc, hq), F32),
                            pltpu.VMEM((d_att, hq), BF16), pltpu.VMEM((V_ROWS, 2 * Q_BLOCK), F32)],
            compiler_params=_cparams("parallel", "arbitrary"),
            name=f"dsa{nk}",
        )(*args)
    return out.reshape(t, d_att)


def _pad_cols(w, width):
    return jnp.concatenate([w, jnp.zeros((w.shape[0], width - w.shape[1]), w.dtype)], axis=1)


def kernel(x, norm_ffn1, ffn1_w13, ffn1_w2, norm_mix, w_in, ssd_conv_w, ssd_conv_b, ssd_dt_bias, ssd_a_log, ssd_d, ssd_norm, cq_norm, ckv_norm, w_uq, w_ukv, q_norm, k_norm, w_qidx, kidx_norm, lru_conv_w, lru_conv_b, lru_wa, lru_ba, lru_wi, lru_bi, lru_lambda, w_out, norm_ffn2, ffn2_w13, ffn2_w2):
    bsz, length, d = x.shape
    depth = w_in.shape[0]
    d_ssd = ssd_norm.shape[1]
    conv_ch = ssd_conv_b.shape[1]
    heads = ssd_d.shape[1]
    q_rank, kv_rank = cq_norm.shape[1], ckv_norm.shape[1]
    d_lru = lru_lambda.shape[1]
    assert length % KEY_CHUNK == 0 and (bsz * length) % 512 == 0, (bsz, length)
    assert d_ssd == heads * SSD_HEAD_DIM and conv_ch == d_ssd + 2 * SSD_GROUPS * SSD_STATE, (d_ssd, conv_ch)
    assert w_uq.shape[2] == ATT_HEADS * ATT_HEAD_DIM and w_qidx.shape[2] == IDX_HEADS * IDX_DIM
    assert heads <= LANE and IDX_DIM + IDX_HEADS <= LANE and ssd_conv_w.shape[1] == CONV_W == lru_conv_w.shape[1]
    sizes = [d_ssd, conv_ch, heads, q_rank, kv_rank, IDX_DIM, IDX_HEADS, d_lru, d_lru]
    offs = [0]
    for s in sizes:
        offs.append(offs[-1] + s)
    widths = (d_ssd, conv_ch, LANE, q_rank, kv_rank, LANE, d_lru, d_lru)
    zero_rows = lambda n: jnp.zeros((n, d), BF16)

    xt = x.reshape(bsz * length, d)
    w13_1, w2_1, w13_2, w2_2 = (w.astype(BF16) for w in (ffn1_w13, ffn1_w2, ffn2_w13, ffn2_w2))
    for l in range(depth):
        wt = w_in[l].T.astype(BF16)
        w_pad = jnp.concatenate(
            [wt[offs[0]:offs[3]], zero_rows(LANE - heads), wt[offs[3]:offs[7]],
             zero_rows(LANE - IDX_DIM - IDX_HEADS), wt[offs[7]:offs[9]]], axis=0)
        xt, z, xbc, dtp, cq, ckv, misc, xl, gl = _ffn(xt, norm_ffn1[l], w13_1, w2_1, l,
                                                      proj=(norm_mix[l], w_pad, widths))
        y_ssd = _ssd(z, xbc, dtp, ssd_conv_w[l], ssd_conv_b[l], ssd_dt_bias[l], ssd_a_log[l], ssd_d[l],
                     ssd_norm[l], bsz=bsz)
        y_lru = _lru(xl, gl, lru_conv_w[l], lru_conv_b[l], lru_wa[l], lru_ba[l], lru_wi[l], lru_bi[l],
                     lru_lambda[l], bsz=bsz)
        prm = dict(cq_norm=cq_norm[l], ckv_norm=ckv_norm[l], w_uq=w_uq[l], w_ukv=w_ukv[l], q_norm=q_norm[l],
                   k_norm=k_norm[l], w_qidx=w_qidx[l], kidx_norm=kidx_norm[l])
        k, ki, vt, qt, qit, wc = _dsa_prep(cq, ckv, misc, prm, bsz=bsz)
        y_att = _dsa(k, ki, vt, qt, qit, wc, bsz=bsz)
        xt = _ffn(xt, norm_ffn2[l], w13_2, w2_2, l, mixed=([y_ssd, y_att, y_lru], w_out[l].astype(BF16)))
    return xt.reshape(bsz, length, d)
```

```python
import functools

import jax
import jax.numpy as jnp
from jax import lax
from jax.experimental import pallas as pl
from jax.experimental.pallas import tpu as pltpu

F32 = jnp.float32
BF16 = jnp.bfloat16

RMS_EPS = 1e-6
LOG2_E = 1.4426950408889634
SSD_HEAD_DIM = 64
SSD_GROUPS = 2
SSD_STATE = 128
SSD_CHUNK = 128
ATT_HEADS = 4
ATT_HEAD_DIM = 64
IDX_HEADS = 4
IDX_DIM = 64
TOP_K = 256
Q_BLOCK = 128
ROPE_THETA = 500000.0
ROPE_ROT = 16
LRU_C = 8.0
CONV_W = 4
LANE = 128
SUBLANE = 8
KEY_CHUNK = 512
V_ONES_ROWS = 16
V_PAIR_ROWS = 2 * ATT_HEAD_DIM + V_ONES_ROWS
V_ROWS = (ATT_HEADS // 2) * V_PAIR_ROWS
VMEM_LIMIT = 56 * 1024 * 1024


def _cparams(*sem):
    return pltpu.CompilerParams(dimension_semantics=sem, vmem_limit_bytes=VMEM_LIMIT)


def _const_spec(shape):
    nd = len(shape)
    return pl.BlockSpec(shape, lambda *_: (0,) * nd, pipeline_mode=pl.Buffered(1))


def _rms(x, g):
    ms = jnp.mean(x * x, axis=-1, keepdims=True)
    return x * lax.rsqrt(ms + RMS_EPS) * g


def _softplus(x):
    return jnp.maximum(x, 0.0) + jnp.log1p(jnp.exp(-jnp.abs(x)))


def _silu(x):
    return x * jax.nn.sigmoid(x)


def _split3(x):
    hi = x.astype(BF16)
    r = x - hi.astype(F32)
    mid = r.astype(BF16)
    return hi, mid, (r - mid.astype(F32)).astype(BF16)


def _dot_sel_rhs(x, sel):
    sel = sel.astype(BF16)
    return sum(jnp.dot(p, sel, preferred_element_type=F32) for p in _split3(x))


def _dot_sel_lhs(sel, x):
    sel = sel.astype(BF16)
    return sum(jnp.dot(sel, p, preferred_element_type=F32) for p in _split3(x))


def _ffn_body(*refs, d_ff, tf, n_mix, proj_widths, d_lru):
    refs = list(refs)
    x = refs.pop(0)[...]
    if n_mix:
        y_refs = [refs.pop(0) for _ in range(n_mix)]
        wo_ref = refs.pop(0)
        off = 0
        for y_ref in y_refs:
            wd = y_ref.shape[1]
            x = x + jnp.dot(y_ref[...].astype(BF16), wo_ref[off:off + wd, :], preferred_element_type=F32)
            off += wd
    g_ref, w13_ref, w2_ref = refs.pop(0), refs.pop(0), refs.pop(0)
    if proj_widths:
        gp_ref, wi_ref = refs.pop(0), refs.pop(0)
    if d_lru:
        lru_refs = [refs.pop(0) for _ in range(7)]
    o_ref = refs.pop(0)

    h = _rms(x, g_ref[...]).astype(BF16)
    acc = jnp.zeros(x.shape, F32)
    for j in range(d_ff // tf):
        g = jnp.dot(h, w13_ref[:, j * tf:(j + 1) * tf], preferred_element_type=F32)
        u = jnp.dot(h, w13_ref[:, d_ff + j * tf:d_ff + (j + 1) * tf], preferred_element_type=F32)
        a = (_silu(g) * u).astype(BF16)
        acc = acc + jnp.dot(a, w2_ref[j * tf:(j + 1) * tf, :], preferred_element_type=F32)
    x = x + 0.5 * acc
    o_ref[...] = x

    if proj_widths:
        hp = _rms(x, gp_ref[...]).astype(BF16)
        off = 0
        if d_lru:
            *p_refs, y_lru_ref, tail_scr, h_scr = refs
            b = pl.program_id(1)
            xl = _dot_nt(hp, wi_ref[0:d_lru, :])
            gl = _dot_nt(hp, wi_ref[d_lru:2 * d_lru, :])
            off = 2 * d_lru
            _lru_tile(xl, gl, *lru_refs, y_lru_ref, tail_scr.at[b], h_scr.at[b], pl.program_id(0) == 0)
        else:
            p_refs = refs
        for p_ref, wd in zip(p_refs, proj_widths):
            p_ref[...] = _dot_nt(hp, wi_ref[off:off + wd, :])
            off += wd


def _ffn(x, g, w13, w2, layer, bsz, *, mixed=None, proj=None, lru=None, tm=512, tf=256):
    t, d = x.shape
    d_ff = w2.shape[1]
    nt = t // bsz // tm
    row = lambda c: pl.BlockSpec((tm, c), lambda i, b: (b * nt + i, 0))
    layer_spec = lambda r, c: pl.BlockSpec((None, r, c), lambda i, b: (layer, 0, 0), pipeline_mode=pl.Buffered(1))
    args, in_specs = [x], [row(d)]
    if mixed is not None:
        ys, w_out = mixed
        args += [*ys, w_out]
        in_specs += [row(y.shape[1]) for y in ys] + [_const_spec(w_out.shape)]
    args += [g.reshape(1, d), w13, w2]
    in_specs += [_const_spec((1, d)), layer_spec(d, 2 * d_ff), layer_spec(d_ff, d)]
    out_specs, out_shape, widths, scratch, d_lru = [row(d)], [jax.ShapeDtypeStruct((t, d), F32)], (), [], 0
    if proj is not None:
        gp, w_in, widths = proj
        args += [gp.reshape(1, d), w_in]
        in_specs += [_const_spec((1, d)), _const_spec(w_in.shape)]
        out_specs += [row(wd) for wd in widths]
        out_shape += [jax.ShapeDtypeStruct((t, wd), F32) for wd in widths]
    if lru is not None:
        lru_args, lru_specs = _lru_operands(*lru)
        d_lru = lru[-1].shape[0]
        args += lru_args
        in_specs += lru_specs
        out_specs.append(row(d_lru))
        out_shape.append(jax.ShapeDtypeStruct((t, d_lru), F32))
        scratch = [pltpu.VMEM((bsz, SUBLANE, d_lru), F32), pltpu.VMEM((bsz, SUBLANE, d_lru), F32)]
    outs = pl.pallas_call(
        functools.partial(_ffn_body, d_ff=d_ff, tf=tf, n_mix=len(mixed[0]) if mixed is not None else 0,
                          proj_widths=tuple(widths), d_lru=d_lru),
        grid=(nt, bsz),
        in_specs=in_specs,
        out_specs=out_specs,
        out_shape=out_shape,
        scratch_shapes=scratch,
        compiler_params=_cparams("arbitrary", "arbitrary"),
        name="ffn",
    )(*args)
    return outs if proj is not None else outs[0]


def _causal_conv(x, cw_ref, cb_ref, tail_scr, first, rows):
    tail = jnp.where(first, 0.0, tail_scr[...])
    row = lax.broadcasted_iota(jnp.int32, tail.shape, 0)
    y = cb_ref[...] + cw_ref[CONV_W - 1:CONV_W, :] * x
    for w in range(CONV_W - 1):
        back = CONV_W - 1 - w
        xr = pltpu.roll(x, back, 0)
        top = jnp.where(row < back, pltpu.roll(tail, back, 0), xr[0:SUBLANE, :])
        y = y + cw_ref[w:w + 1, :] * jnp.concatenate([top, xr[SUBLANE:, :]], axis=0)
    tail_scr[...] = x[rows - SUBLANE:rows, :]
    return y


def _ssd_body(z_ref, xbc_ref, dtp_ref, cw_ref, cb_ref, dtb_ref, alog_ref, dsk_ref, ng_ref, expand_ref,
              y_ref, tail_scr, st_scr, *, q, d_ssd):
    step = pl.program_id(1)
    n = SSD_STATE
    gw = d_ssd // SSD_GROUPS
    rpg = gw // SSD_HEAD_DIM

    @pl.when(step == 0)
    def _():
        st_scr[...] = jnp.zeros(st_scr.shape, F32)

    xbc = _silu(_causal_conv(xbc_ref[...], cw_ref, cb_ref, tail_scr, step == 0, q))
    xs = xbc[:, :d_ssd]
    bm = xbc[:, d_ssd:d_ssd + SSD_GROUPS * n]
    cm = xbc[:, d_ssd + SSD_GROUPS * n:]

    dt = _softplus(dtp_ref[...] + dtb_ref[...])
    adt = dt * (-jnp.exp(alog_ref[...]))
    ri = lax.broadcasted_iota(jnp.int32, (q, q), 0)
    ci = lax.broadcasted_iota(jnp.int32, (q, q), 1)
    causal = ri >= ci
    acum = _dot_sel_lhs(causal.astype(F32), adt)
    expand = expand_ref[...]
    a_x = _dot_sel_rhs(acum, expand)
    dt_x = _dot_sel_rhs(dt, expand)
    acum_t = acum.T
    a_last = a_x[q - 1:q, :]
    exp_a = jnp.exp(a_x)
    xdt = xs * dt_x
    xsw = xdt * jnp.exp(a_last - a_x)
    cdec = jnp.exp(a_last)

    ys = []
    for g in range(SSD_GROUPS):
        bm_g = bm[:, g * n:(g + 1) * n]
        cm_g = cm[:, g * n:(g + 1) * n].astype(BF16)
        cb = lax.dot_general(cm_g, bm_g.astype(BF16), (((1,), (1,)), ((), ())), preferred_element_type=F32)
        yd = []
        for r in range(rpg):
            h = g * rpg + r
            seg = acum[:, h:h + 1] - acum_t[h:h + 1, :]
            decay = jnp.exp(jnp.where(causal, seg, -jnp.inf))
            m = (cb * decay).astype(BF16)
            xh = xdt[:, h * SSD_HEAD_DIM:(h + 1) * SSD_HEAD_DIM].astype(BF16)
            yd.append(jnp.dot(m, xh, preferred_element_type=F32))
        yd = jnp.concatenate(yd, axis=1)
        gs = slice(g * gw, (g + 1) * gw)
        prev = st_scr[g]
        yoff = jnp.dot(cm_g, prev.astype(BF16), preferred_element_type=F32) * exp_a[:, gs]
        st = jnp.dot(bm_g.T.astype(BF16), xsw[:, gs].astype(BF16), preferred_element_type=F32)
        st_scr[g] = prev * cdec[:, gs] + st
        ys.append(yd + yoff + dsk_ref[:, gs] * xs[:, gs])

    z = z_ref[...]
    for g in range(SSD_GROUPS):
        gs = slice(g * gw, (g + 1) * gw)
        y_ref[:, gs] = _rms(ys[g] * _silu(z[:, gs]), ng_ref[:, gs])


def _ssd(z, xbc, dtp, conv_w, conv_b, dt_bias, a_log, d_skip, norm_g, *, bsz):
    t, d_ssd = z.shape
    cch = xbc.shape[1]
    q = SSD_CHUNK
    nc = t // bsz // q
    heads = d_ssd // SSD_HEAD_DIM
    pad = lambda v, fill: jnp.concatenate([v, jnp.full((LANE - heads,), fill, F32)]).reshape(1, LANE)
    row = lambda c: pl.BlockSpec((q, c), lambda b, i: (b * nc + i, 0))
    expand = (jnp.arange(d_ssd)[None, :] // SSD_HEAD_DIM == jnp.arange(LANE)[:, None]).astype(BF16)
    return pl.pallas_call(
        functools.partial(_ssd_body, q=q, d_ssd=d_ssd),
        grid=(bsz, nc),
        in_specs=[row(d_ssd), row(cch), row(LANE), _const_spec((CONV_W, cch)), _const_spec((1, cch)),
                  _const_spec((1, LANE)), _const_spec((1, LANE)), _const_spec((1, d_ssd)),
                  _const_spec((1, d_ssd)), _const_spec((LANE, d_ssd))],
        out_specs=row(d_ssd),
        out_shape=jax.ShapeDtypeStruct((t, d_ssd), F32),
        scratch_shapes=[pltpu.VMEM((SUBLANE, cch), F32),
                        pltpu.VMEM((SSD_GROUPS, SSD_STATE, d_ssd // SSD_GROUPS), F32)],
        compiler_params=_cparams("parallel", "arbitrary"),
        name="ssd",
    )(z, xbc, dtp, conv_w, conv_b.reshape(1, cch), pad(dt_bias, 0.0), pad(a_log, 0.0),
      jnp.repeat(d_skip, SSD_HEAD_DIM).reshape(1, d_ssd), norm_g.reshape(1, d_ssd), expand)


def _lru_tile(xl, gl, cw_ref, cb_ref, wa_ref, ba_ref, wi_ref, bi_ref, lam_ref, y_ref, tail_scr, h_scr, first):
    tl = xl.shape[0]
    xr = _causal_conv(xl, cw_ref, cb_ref, tail_scr, first, tl)
    xrb = xr.astype(BF16)
    r = jax.nn.sigmoid(jnp.dot(xrb, wa_ref[...], preferred_element_type=F32) + ba_ref[...])
    i = jax.nn.sigmoid(jnp.dot(xrb, wi_ref[...], preferred_element_type=F32) + bi_ref[...])
    log_a = -LRU_C * r * _softplus(-lam_ref[...])
    a = jnp.exp(log_a)
    b = jnp.sqrt(-jnp.tanh(log_a) * (a * a + 1.0)) * (i * xr)
    rows = lax.broadcasted_iota(jnp.int32, a.shape, 0) % SUBLANE
    s = 1
    while s < SUBLANE:
        keep = rows >= s
        a_sh = jnp.where(keep, pltpu.roll(a, s, 0), 1.0)
        b_sh = jnp.where(keep, pltpu.roll(b, s, 0), 0.0)
        b = a * b_sh + b
        a = a * a_sh
        s *= 2
    h = jnp.where(first, 0.0, h_scr[0:1, :])
    gate = jax.nn.gelu(gl)
    for g in range(tl // SUBLANE):
        grp = slice(g * SUBLANE, (g + 1) * SUBLANE)
        hs = a[grp, :] * h + b[grp, :]
        y_ref[grp, :] = hs * gate[grp, :]
        h = hs[SUBLANE - 1:SUBLANE, :]
    h_scr[...] = jnp.broadcast_to(h, h_scr.shape)


def _block_diag(w):
    nb, bw, _ = w.shape
    out = jnp.zeros((nb * bw, nb * bw), w.dtype)
    for k in range(nb):
        out = out.at[k * bw:(k + 1) * bw, k * bw:(k + 1) * bw].set(w[k])
    return out


def _lru_operands(conv_w, conv_b, wa, ba, wi, bi, lam):
    dl = lam.shape[0]
    vec = _const_spec((1, dl))
    args = [conv_w, conv_b.reshape(1, dl), _block_diag(wa).astype(BF16), ba.reshape(1, dl),
            _block_diag(wi).astype(BF16), bi.reshape(1, dl), lam.reshape(1, dl)]
    return args, [_const_spec((CONV_W, dl)), vec, _const_spec((dl, dl)), vec, _const_spec((dl, dl)), vec, vec]


def _rope(x, ct, st, axis):
    width = x.shape[axis]
    fm = lax.broadcasted_iota(jnp.int32, x.shape, axis) % ATT_HEAD_DIM
    half = ROPE_ROT // 2
    partner = jnp.where(fm < half, pltpu.roll(x, width - half, axis), pltpu.roll(x, half, axis))
    return x * ct + partner * st


def _dot_nt(a, b):
    return lax.dot_general(a, b, (((1,), (1,)), ((), ())), preferred_element_type=F32)


def _prep_body(cq_ref, ckv_ref, misc_ref, ct_ref, st_ref, ctt_ref, stt_ref, cqg_ref, ckvg_ref, wuqt_ref,
               wuk_ref, wuvt_ref, qg_ref, kg_ref, wqit_ref, kig_ref, hm_ref,
               k_ref, ki_ref, vt_ref, qt_ref, qit_ref, wc_ref, *, tm):
    ct = ct_ref[...]
    st = st_ref[...]
    d_att = wuk_ref.shape[1]
    head_mean = hm_ref[...]

    cqn = _rms(cq_ref[...], cqg_ref[...]).astype(BF16)
    ckvn = _rms(ckv_ref[...], ckvg_ref[...]).astype(BF16)

    qt = _dot_nt(wuqt_ref[...], cqn)
    ms = _dot_sel_lhs(head_mean, qt * qt)
    qt = qt * lax.rsqrt(ms + RMS_EPS) * qg_ref[...]
    qt_ref[...] = (_rope(qt, ctt_ref[...], stt_ref[...], 0) * (ATT_HEAD_DIM ** -0.5 * LOG2_E)).astype(BF16)
    vt = _dot_nt(wuvt_ref[...], ckvn).astype(BF16)
    for pr in range(ATT_HEADS // 2):
        base = pr * V_PAIR_ROWS
        vt_ref[base:base + 2 * ATT_HEAD_DIM, :] = vt[pr * 2 * ATT_HEAD_DIM:(pr + 1) * 2 * ATT_HEAD_DIM, :]
        vt_ref[base + 2 * ATT_HEAD_DIM:base + V_PAIR_ROWS, :] = jnp.ones((V_ONES_ROWS, tm), BF16)

    k = jnp.dot(ckvn, wuk_ref[...], preferred_element_type=F32)
    ms = _dot_sel_rhs(k * k, head_mean)
    k_ref[...] = _rope(k * lax.rsqrt(ms + RMS_EPS) * kg_ref[...], ct, st, 1).astype(BF16)

    qit = _rope(_dot_nt(wqit_ref[...], cqn), ctt_ref[...], stt_ref[...], 0)
    misc = misc_ref[...]
    misc_t = misc.T
    wscale = IDX_HEADS ** -0.5 * IDX_DIM ** -0.5
    for j in range(tm // Q_BLOCK):
        cols = slice(j * Q_BLOCK, (j + 1) * Q_BLOCK)
        for h in range(IDX_HEADS):
            qit_ref[j, 0:IDX_DIM, h * Q_BLOCK:(h + 1) * Q_BLOCK] = (
                qit[h * IDX_DIM:(h + 1) * IDX_DIM, cols].astype(BF16))
        qit_ref[j, IDX_DIM:, :] = jnp.zeros((LANE - IDX_DIM, IDX_HEADS * Q_BLOCK), BF16)
        wrow = jnp.concatenate([misc_t[IDX_DIM + h:IDX_DIM + h + 1, cols] for h in range(IDX_HEADS)], axis=1)
        wc_ref[j] = jnp.broadcast_to(wrow * wscale, (SUBLANE, IDX_HEADS * Q_BLOCK))

    lane = lax.broadcasted_iota(jnp.int32, misc.shape, 1)
    km = jnp.where(lane < IDX_DIM, misc, 0.0)
    ms = jnp.sum(km * km, axis=-1, keepdims=True) * (1.0 / IDX_DIM)
    kin = km * lax.rsqrt(ms + RMS_EPS) * kig_ref[...]
    ki_ref[...] = _rope(kin, ct[:, :LANE], st[:, :LANE], 1).astype(BF16)


def _rope_lane_tables(length, width):
    half = ROPE_ROT // 2
    inv = ROPE_THETA ** (-jnp.arange(half, dtype=F32) * 2.0 / ROPE_ROT)
    ang = jnp.arange(length, dtype=F32)[:, None] * inv[None, :]
    cos, sin = jnp.cos(ang), jnp.sin(ang)
    ones = jnp.ones((length, ATT_HEAD_DIM - ROPE_ROT), F32)
    ct = jnp.concatenate([cos, cos, ones], axis=1)
    st = jnp.concatenate([-sin, sin, 0.0 * ones], axis=1)
    reps = width // ATT_HEAD_DIM
    return jnp.tile(ct, (1, reps)), jnp.tile(st, (1, reps))


def _dsa_prep(cq, ckv, misc, p, *, bsz):
    t, qr = cq.shape
    kvr = ckv.shape[1]
    tm = KEY_CHUNK
    length = t // bsz
    nt = length // tm
    d_att = ATT_HEADS * ATT_HEAD_DIM
    ct, st = _rope_lane_tables(length, d_att)
    wukv = p["w_ukv"].reshape(kvr, ATT_HEADS, 2, ATT_HEAD_DIM)
    wuk = wukv[:, :, 0, :].reshape(kvr, d_att).astype(BF16)
    wuvt = wukv[:, :, 1, :].reshape(kvr, d_att).T.astype(BF16)
    kig = jnp.concatenate([p["kidx_norm"], jnp.zeros((LANE - IDX_DIM,), F32)]).reshape(1, LANE)
    head_of = jnp.arange(d_att) // ATT_HEAD_DIM
    head_mean = jnp.where(head_of[:, None] == head_of[None, :], 1.0 / ATT_HEAD_DIM, 0.0).astype(BF16)
    row = lambda c: pl.BlockSpec((tm, c), lambda i, b: (b * nt + i, 0))
    tab = pl.BlockSpec((tm, d_att), lambda i, b: (i, 0))
    tab_t = pl.BlockSpec((d_att, tm), lambda i, b: (0, i))
    nqb = tm // Q_BLOCK
    hq = IDX_HEADS * Q_BLOCK
    return pl.pallas_call(
        functools.partial(_prep_body, tm=tm),
        grid=(nt, bsz),
        in_specs=[row(qr), row(kvr), row(LANE), tab, tab, tab_t, tab_t, _const_spec((1, qr)),
                  _const_spec((1, kvr)), _const_spec((d_att, qr)), _const_spec((kvr, d_att)),
                  _const_spec((d_att, kvr)), _const_spec((d_att, 1)), _const_spec((1, d_att)),
                  _const_spec((IDX_HEADS * IDX_DIM, qr)), _const_spec((1, LANE)), _const_spec((d_att, d_att))],
        out_specs=[row(d_att), row(LANE),
                   pl.BlockSpec((None, V_ROWS, tm), lambda i, b: (b * nt + i, 0, 0)),
                   pl.BlockSpec((None, d_att, tm), lambda i, b: (b * nt + i, 0, 0)),
                   pl.BlockSpec((nqb, LANE, hq), lambda i, b: (b * nt + i, 0, 0)),
                   pl.BlockSpec((nqb, SUBLANE, hq), lambda i, b: (b * nt + i, 0, 0))],
        out_shape=[jax.ShapeDtypeStruct((t, d_att), BF16),
                   jax.ShapeDtypeStruct((t, LANE), BF16),
                   jax.ShapeDtypeStruct((bsz * nt, V_ROWS, tm), BF16),
                   jax.ShapeDtypeStruct((bsz * nt, d_att, tm), BF16),
                   jax.ShapeDtypeStruct((t // Q_BLOCK, LANE, hq), BF16),
                   jax.ShapeDtypeStruct((t // Q_BLOCK, SUBLANE, hq), F32)],
        compiler_params=_cparams("parallel", "parallel"),
        name="dsa_prep",
    )(cq, ckv, misc, ct, st, ct.T, st.T, p["cq_norm"].reshape(1, qr), p["ckv_norm"].reshape(1, kvr),
      p["w_uq"].T.astype(BF16), wuk, wuvt, jnp.tile(p["q_norm"], ATT_HEADS).reshape(d_att, 1),
      jnp.tile(p["k_norm"], ATT_HEADS).reshape(1, d_att), p["w_qidx"].T.astype(BF16), kig, head_mean)


INT_MIN = -2 ** 31
F32_TINY = 2.0 ** -126
F32_LOWEST = -3.4028234663852886e38
SEARCH_VALUE_STEPS = 18
SEARCH_CAP = 80


def _order_key_to_float(u):
    sk = u ^ jnp.int32(INT_MIN)
    fb = sk ^ ((sk >> 31) & jnp.int32(0x7FFFFFFF))
    return lax.bitcast_convert_type(fb, F32)


def _float_to_order_key(f):
    b = lax.bitcast_convert_type(f, jnp.int32)
    return (b ^ ((b >> 31) & jnp.int32(0x7FFFFFFF))) ^ jnp.int32(INT_MIN)


def _fold_rows(x, op):
    parts = [x[i * SUBLANE:(i + 1) * SUBLANE, :] for i in range(x.shape[0] // SUBLANE)]
    while len(parts) > 1:
        parts = [op(parts[i], parts[i + 1]) for i in range(0, len(parts), 2)]
    return parts[0]


def _dsa_body(qit_ref, wc_ref, qt_ref, ki_ref, k_ref, vt_ref, *rest, n_sel, nk, chained):
    o_ref, sc_scr, lg_scr, qbd_scr, oacc_scr = rest[1:] if chained else rest
    kc = KEY_CHUNK
    qb = (nk - 1) * (kc // Q_BLOCK) + pl.program_id(1)
    qpos = qb * Q_BLOCK + lax.broadcasted_iota(jnp.int32, (1, Q_BLOCK), 1)
    hq = IDX_HEADS * Q_BLOCK
    k_f = jnp.float32(n_sel)

    def chunk(c):
        return pl.ds(c * kc, kc)

    def over_chunks(body, carry):
        for c in range(nk):
            carry = body(c, carry)
        return carry

    qit = qit_ref[...]
    wc = wc_ref[0:1, :]

    def score_chunk(c, carry):
        amax, c_pos, c_nn = carry
        s4 = jnp.dot(ki_ref[chunk(c), :], qit, preferred_element_type=F32)
        s4 = jnp.maximum(s4, 0.0) * wc
        s = (s4[:, 0:Q_BLOCK] + s4[:, Q_BLOCK:2 * Q_BLOCK]) + (s4[:, 2 * Q_BLOCK:3 * Q_BLOCK] + s4[:, 3 * Q_BLOCK:])
        kpos = c * kc + lax.broadcasted_iota(jnp.int32, (kc, Q_BLOCK), 0)
        sm = jnp.where(kpos <= qpos, s, -jnp.inf)
        sc_scr[chunk(c), :] = sm
        return (jnp.maximum(amax, _fold_rows(jnp.abs(s), jnp.maximum)),
                c_pos + _fold_rows(jnp.where(sm >= F32_TINY, 1.0, 0.0), jnp.add),
                c_nn + _fold_rows(jnp.where(sm >= 0.0, 1.0, 0.0), jnp.add))

    zeros8 = jnp.zeros((SUBLANE, Q_BLOCK), F32)
    amax, c_pos, c_nn = over_chunks(score_chunk, (zeros8, zeros8, zeros8))
    amax = jnp.max(amax, axis=0, keepdims=True)
    c_pos = jnp.sum(c_pos, axis=0, keepdims=True)
    c_nn = jnp.sum(c_nn, axis=0, keepdims=True)

    def count_ge(*thrs):
        def body(c, accs):
            blk = sc_scr[chunk(c), :]
            return tuple(a + _fold_rows(jnp.where(blk >= t, 1.0, 0.0), jnp.add) for a, t in zip(accs, thrs))
        accs = over_chunks(body, tuple(jnp.zeros((SUBLANE, Q_BLOCK), F32) for _ in thrs))
        return tuple(jnp.sum(a, axis=0, keepdims=True) for a in accs)

    n_valid = (qpos + 1).astype(F32)
    take_all = n_valid <= k_f
    zero_tie = (c_pos < k_f) & (c_nn >= k_f)
    pos = c_pos >= k_f
    hi_top = _order_key_to_float(_float_to_order_key(amax) + 1)
    lo = jnp.where(take_all, F32_LOWEST, jnp.where(zero_tie, 0.0, jnp.where(pos, F32_TINY, -amax)))
    hi = jnp.where(take_all, jnp.inf, jnp.where(zero_tie, F32_TINY, jnp.where(pos, hi_top, 0.0)))
    cnt_lo = jnp.where(take_all | (~zero_tie & ~pos), n_valid, jnp.where(zero_tie, c_nn, c_pos))
    cnt_hi = jnp.where(take_all | (~zero_tie & pos), 0.0, jnp.where(zero_tie, c_pos, c_nn))
    settled = take_all | zero_tie

    def search_step(st, on_values):
        lk, hk, lo, hi, cnt_lo, cnt_hi = st
        midk = lk + ((hk - lk) >> 1)
        mid = _order_key_to_float(midk)
        if on_values:
            vmid = lo + (hi - lo) * 0.5
            inside = (vmid > lo) & (vmid < hi)
            mid = jnp.where(inside, vmid, mid)
            midk = jnp.where(inside, _float_to_order_key(vmid), midk)
        cnt, = count_ge(mid)
        ge = cnt >= k_f
        return (jnp.where(ge, midk, lk), jnp.where(ge, hk, midk), jnp.where(ge, mid, lo), jnp.where(ge, hi, mid),
                jnp.where(ge, cnt, cnt_lo), jnp.where(ge, cnt_hi, cnt))

    def search_cond(carry):
        it, (lk, hk, _, _, cnt_lo, _) = carry
        open_ = jnp.logical_not(settled | (cnt_lo == k_f) | (hk - lk <= 1))
        return (it < SEARCH_CAP) & (jnp.sum(jnp.where(open_, 1.0, 0.0)) > 0.0)

    def search_pair(carry):
        it, st = carry
        return it + 1, search_step(search_step(st, False), False)

    st = (_float_to_order_key(lo), _float_to_order_key(hi), lo, hi, cnt_lo, cnt_hi)
    st = lax.fori_loop(0, SEARCH_VALUE_STEPS, lambda _, s: search_step(s, True), st)
    _, (_, _, lo, hi, cnt_lo, cnt_hi) = lax.while_loop(search_cond, search_pair, (jnp.int32(0), st))
    need = k_f - cnt_hi

    sub = LANE
    ri = lax.broadcasted_iota(jnp.int32, (sub, sub), 0)
    ci = lax.broadcasted_iota(jnp.int32, (sub, sub), 1)
    before = (ri > ci).astype(BF16)

    qbd_scr[...] = jnp.zeros(qbd_scr.shape, BF16)
    for h in range(ATT_HEADS):
        rows = slice(h * ATT_HEAD_DIM, (h + 1) * ATT_HEAD_DIM)
        qbd_scr[rows, h * Q_BLOCK:(h + 1) * Q_BLOCK] = qt_ref[rows, :]
    qbd = qbd_scr[...]

    def logit_chunk(c, carry):
        m, seen = carry
        for j in range(kc // sub):
            rows = pl.ds(c * kc + j * sub, sub)
            blk = sc_scr[rows, :]
            in_hi = blk >= hi
            tie = (blk >= lo) & jnp.logical_not(in_hi)
            tie_f = jnp.where(tie, 1.0, 0.0)
            rank = jnp.dot(before, tie_f.astype(BF16), preferred_element_type=F32) + seen
            sel = in_hi | (tie & (rank < need))
            seen = seen + jnp.sum(tie_f, axis=0, keepdims=True)
            b = jnp.where(sel, 0.0, -jnp.inf)
            lg = jnp.dot(k_ref[rows, :], qbd, preferred_element_type=F32) + jnp.concatenate([b] * ATT_HEADS, axis=1)
            lg_scr[rows, :] = lg
            m = jnp.maximum(m, jnp.max(lg, axis=0, keepdims=True))
        return m, seen

    m, _ = over_chunks(logit_chunk, (jnp.full((1, hq), -jnp.inf, F32), jnp.zeros((1, Q_BLOCK), F32)))
    oacc_scr[...] = jnp.zeros(oacc_scr.shape, F32)

    pair_cols = 2 * Q_BLOCK

    def value_chunk(c, carry):
        depth = 2 * LANE
        for pr in range(ATT_HEADS // 2):
            rws = slice(pr * V_PAIR_ROWS, (pr + 1) * V_PAIR_ROWS)
            cls = slice(pr * pair_cols, (pr + 1) * pair_cols)
            acc = oacc_scr[rws, :]
            for j in range(kc // depth):
                rows = pl.ds(c * kc + j * depth, depth)
                p = jnp.exp2(lg_scr[rows, cls] - m[:, cls]).astype(BF16)
                acc = acc + jnp.dot(vt_ref[c, rws, j * depth:(j + 1) * depth], p, preferred_element_type=F32)
            oacc_scr[rws, :] = acc
        return carry

    over_chunks(value_chunk, 0)
    outs = []
    for h in range(ATT_HEADS):
        base = (h // 2) * V_PAIR_ROWS
        cols = slice((h % 2) * Q_BLOCK, (h % 2 + 1) * Q_BLOCK)
        ones_row = base + 2 * ATT_HEAD_DIM
        ssum = oacc_scr[ones_row:ones_row + 1, cols]
        outs.append(oacc_scr[base + (h % 2) * ATT_HEAD_DIM:base + (h % 2 + 1) * ATT_HEAD_DIM, cols] / ssum)
    o_ref[...] = jnp.concatenate(outs, axis=0).T


def _dsa(k, ki, vt, qt, qit, wc, *, bsz):
    t, d_att = k.shape
    length = t // bsz
    nb = length // Q_BLOCK
    kc = KEY_CHUNK
    nkc = length // kc
    hq = IDX_HEADS * Q_BLOCK
    per_q = kc // Q_BLOCK
    vt = vt.reshape(bsz, nkc, V_ROWS, kc)
    ki = ki.reshape(bsz, length, LANE)
    k = k.reshape(bsz, length, d_att)
    out = None
    for g in range(nkc):
        nk = g + 1
        qblk = lambda b, i, g=g: b * nb + g * per_q + i
        in_specs = [pl.BlockSpec((None, LANE, hq), lambda b, i, f=qblk: (f(b, i), 0, 0)),
                    pl.BlockSpec((None, SUBLANE, hq), lambda b, i, f=qblk: (f(b, i), 0, 0)),
                    pl.BlockSpec((None, d_att, Q_BLOCK), lambda b, i, g=g: (b * nkc + g, 0, i)),
                    pl.BlockSpec((None, nk * kc, LANE), lambda b, i: (b, 0, 0)),
                    pl.BlockSpec((None, nk * kc, d_att), lambda b, i: (b, 0, 0)),
                    pl.BlockSpec((None, nk, V_ROWS, kc), lambda b, i: (b, 0, 0, 0))]
        args = [qit, wc, qt, ki, k, vt]
        chained = out is not None
        if chained:
            in_specs.append(pl.BlockSpec(memory_space=pl.ANY))
            args.append(out)
        out = pl.pallas_call(
            functools.partial(_dsa_body, n_sel=min(TOP_K, length // 4), nk=nk, chained=chained),
            grid=(bsz, per_q),
            in_specs=in_specs,
            out_specs=pl.BlockSpec((None, Q_BLOCK, d_att), lambda b, i, g=g: (b, g * per_q + i, 0)),
            out_shape=jax.ShapeDtypeStruct((bsz, length, d_att), F32),
            input_output_aliases={len(args) - 1: 0} if chained else {},
            scratch_shapes=[pltpu.VMEM((nk * kc, Q_BLOCK), F32), pltpu.VMEM((nk * kc, hq), F32),
                            pltpu.VMEM((d_att, hq), BF16), pltpu.VMEM((V_ROWS, 2 * Q_BLOCK), F32)],
            compiler_params=_cparams("parallel", "arbitrary"),
            name=f"dsa{nk}",
        )(*args)
    return out.reshape(t, d_att)


def _pad_cols(w, width):
    return jnp.concatenate([w, jnp.zeros((w.shape[0], width - w.shape[1]), w.dtype)], axis=1)


def kernel(x, norm_ffn1, ffn1_w13, ffn1_w2, norm_mix, w_in, ssd_conv_w, ssd_conv_b, ssd_dt_bias, ssd_a_log, ssd_d, ssd_norm, cq_norm, ckv_norm, w_uq, w_ukv, q_norm, k_norm, w_qidx, kidx_norm, lru_conv_w, lru_conv_b, lru_wa, lru_ba, lru_wi, lru_bi, lru_lambda, w_out, norm_ffn2, ffn2_w13, ffn2_w2):
    bsz, length, d = x.shape
    depth = w_in.shape[0]
    d_ssd = ssd_norm.shape[1]
    conv_ch = ssd_conv_b.shape[1]
    heads = ssd_d.shape[1]
    q_rank, kv_rank = cq_norm.shape[1], ckv_norm.shape[1]
    d_lru = lru_lambda.shape[1]
    assert length % KEY_CHUNK == 0 and (bsz * length) % 512 == 0, (bsz, length)
    assert d_ssd == heads * SSD_HEAD_DIM and conv_ch == d_ssd + 2 * SSD_GROUPS * SSD_STATE, (d_ssd, conv_ch)
    assert w_uq.shape[2] == ATT_HEADS * ATT_HEAD_DIM and w_qidx.shape[2] == IDX_HEADS * IDX_DIM
    assert heads <= LANE and IDX_DIM + IDX_HEADS <= LANE and ssd_conv_w.shape[1] == CONV_W == lru_conv_w.shape[1]
    sizes = [d_ssd, conv_ch, heads, q_rank, kv_rank, IDX_DIM, IDX_HEADS, d_lru, d_lru]
    offs = [0]
    for s in sizes:
        offs.append(offs[-1] + s)
    widths = (d_ssd, conv_ch, LANE, q_rank, kv_rank, LANE)
    zero_rows = lambda n: jnp.zeros((n, d), BF16)

    xt = x.reshape(bsz * length, d)
    w13_1, w2_1, w13_2, w2_2 = (w.astype(BF16) for w in (ffn1_w13, ffn1_w2, ffn2_w13, ffn2_w2))
    for l in range(depth):
        wt = w_in[l].T.astype(BF16)
        w_pad = jnp.concatenate(
            [wt[offs[7]:offs[9]], wt[offs[0]:offs[3]], zero_rows(LANE - heads), wt[offs[3]:offs[7]],
             zero_rows(LANE - IDX_DIM - IDX_HEADS)], axis=0)
        xt, z, xbc, dtp, cq, ckv, misc, y_lru = _ffn(
            xt, norm_ffn1[l], w13_1, w2_1, l, bsz, proj=(norm_mix[l], w_pad, widths),
            lru=(lru_conv_w[l], lru_conv_b[l], lru_wa[l], lru_ba[l], lru_wi[l], lru_bi[l], lru_lambda[l]))
        y_ssd = _ssd(z, xbc, dtp, ssd_conv_w[l], ssd_conv_b[l], ssd_dt_bias[l], ssd_a_log[l], ssd_d[l],
                     ssd_norm[l], bsz=bsz)
        prm = dict(cq_norm=cq_norm[l], ckv_norm=ckv_norm[l], w_uq=w_uq[l], w_ukv=w_ukv[l], q_norm=q_norm[l],
                   k_norm=k_norm[l], w_qidx=w_qidx[l], kidx_norm=kidx_norm[l])
        k, ki, vt, qt, qit, wc = _dsa_prep(cq, ckv, misc, prm, bsz=bsz)
        y_att = _dsa(k, ki, vt, qt, qit, wc, bsz=bsz)
        xt = _ffn(xt, norm_ffn2[l], w13_2, w2_2, l, bsz, mixed=([y_ssd, y_att, y_lru], w_out[l].astype(BF16)))
    return xt.reshape(bsz, length, d)
```

```python
import functools

import jax
import jax.numpy as jnp
from jax import lax
from jax.experimental import pallas as pl
from jax.experimental.pallas import tpu as pltpu

F32 = jnp.float32
BF16 = jnp.bfloat16

RMS_EPS = 1e-6
LOG2_E = 1.4426950408889634
SSD_HEAD_DIM = 64
SSD_GROUPS = 2
SSD_STATE = 128
SSD_CHUNK = 128
ATT_HEADS = 4
ATT_HEAD_DIM = 64
IDX_HEADS = 4
IDX_DIM = 64
TOP_K = 256
Q_BLOCK = 128
ROPE_THETA = 500000.0
ROPE_ROT = 16
LRU_C = 8.0
CONV_W = 4
LANE = 128
SUBLANE = 8
KEY_CHUNK = 512
V_ONES_ROWS = 16
V_PAIR_ROWS = 2 * ATT_HEAD_DIM + V_ONES_ROWS
V_ROWS = (ATT_HEADS // 2) * V_PAIR_ROWS
VMEM_LIMIT = 56 * 1024 * 1024


def _cparams(*sem):
    return pltpu.CompilerParams(dimension_semantics=sem, vmem_limit_bytes=VMEM_LIMIT)


def _const_spec(shape):
    nd = len(shape)
    return pl.BlockSpec(shape, lambda *_: (0,) * nd, pipeline_mode=pl.Buffered(1))


def _rms(x, g):
    ms = jnp.mean(x * x, axis=-1, keepdims=True)
    return x * lax.rsqrt(ms + RMS_EPS) * g


def _softplus(x):
    return jnp.maximum(x, 0.0) + jnp.log1p(jnp.exp(-jnp.abs(x)))


def _silu(x):
    return x * jax.nn.sigmoid(x)


def _split3(x):
    hi = x.astype(BF16)
    r = x - hi.astype(F32)
    mid = r.astype(BF16)
    return hi, mid, (r - mid.astype(F32)).astype(BF16)


def _dot_sel_rhs(x, sel):
    sel = sel.astype(BF16)
    return sum(jnp.dot(p, sel, preferred_element_type=F32) for p in _split3(x))


def _dot_sel_lhs(sel, x):
    sel = sel.astype(BF16)
    return sum(jnp.dot(sel, p, preferred_element_type=F32) for p in _split3(x))


def _ffn_body(*refs, d_ff, tf, n_mix, proj_widths):
    refs = list(refs)
    x = refs.pop(0)[...]
    if n_mix:
        y_refs = [refs.pop(0) for _ in range(n_mix)]
        wo_ref = refs.pop(0)
        off = 0
        for y_ref in y_refs:
            wd = y_ref.shape[1]
            x = x + jnp.dot(y_ref[...].astype(BF16), wo_ref[off:off + wd, :], preferred_element_type=F32)
            off += wd
    g_ref, w13_ref, w2_ref = refs.pop(0), refs.pop(0), refs.pop(0)
    if proj_widths:
        gp_ref, wi_ref = refs.pop(0), refs.pop(0)
    o_ref = refs.pop(0)

    h = _rms(x, g_ref[...]).astype(BF16)
    acc = jnp.zeros(x.shape, F32)
    for j in range(d_ff // tf):
        g = jnp.dot(h, w13_ref[:, j * tf:(j + 1) * tf], preferred_element_type=F32)
        u = jnp.dot(h, w13_ref[:, d_ff + j * tf:d_ff + (j + 1) * tf], preferred_element_type=F32)
        a = (_silu(g) * u).astype(BF16)
        acc = acc + jnp.dot(a, w2_ref[j * tf:(j + 1) * tf, :], preferred_element_type=F32)
    x = x + 0.5 * acc
    o_ref[...] = x

    if proj_widths:
        hp = _rms(x, gp_ref[...]).astype(BF16)
        off = 0
        for p_ref, wd in zip(refs, proj_widths):
            p_ref[...] = _dot_nt(hp, wi_ref[off:off + wd, :])
            off += wd


def _ffn(x, g, w13, w2, layer, *, mixed=None, proj=None, tm=512, tf=256):
    t, d = x.shape
    d_ff = w2.shape[1]
    row = lambda c: pl.BlockSpec((tm, c), lambda i: (i, 0))
    layer_spec = lambda r, c: pl.BlockSpec((None, r, c), lambda i: (layer, 0, 0), pipeline_mode=pl.Buffered(1))
    args, in_specs = [x], [row(d)]
    if mixed is not None:
        ys, w_out = mixed
        args += [*ys, w_out]
        in_specs += [row(y.shape[1]) for y in ys] + [_const_spec(w_out.shape)]
    args += [g.reshape(1, d), w13, w2]
    in_specs += [_const_spec((1, d)), layer_spec(d, 2 * d_ff), layer_spec(d_ff, d)]
    out_specs, out_shape, widths = [row(d)], [jax.ShapeDtypeStruct((t, d), F32)], ()
    if proj is not None:
        gp, w_in, widths = proj
        args += [gp.reshape(1, d), w_in]
        in_specs += [_const_spec((1, d)), _const_spec(w_in.shape)]
        out_specs += [row(wd) for wd in widths]
        out_shape += [jax.ShapeDtypeStruct((t, wd), F32) for wd in widths]
    outs = pl.pallas_call(
        functools.partial(_ffn_body, d_ff=d_ff, tf=tf, n_mix=len(mixed[0]) if mixed is not None else 0,
                          proj_widths=tuple(widths)),
        grid=(t // tm,),
        in_specs=in_specs,
        out_specs=out_specs,
        out_shape=out_shape,
        compiler_params=_cparams("parallel"),
        name="ffn",
    )(*args)
    return outs if proj is not None else outs[0]


def _causal_conv(x_ref, cw_ref, cb_ref, tail_scr, step, rows):
    @pl.when(step == 0)
    def _():
        tail_scr[...] = jnp.zeros(tail_scr.shape, F32)

    x = x_ref[...]
    tail = tail_scr[...]
    row = lax.broadcasted_iota(jnp.int32, tail.shape, 0)
    y = cb_ref[...] + cw_ref[CONV_W - 1:CONV_W, :] * x
    for w in range(CONV_W - 1):
        back = CONV_W - 1 - w
        xr = pltpu.roll(x, back, 0)
        top = jnp.where(row < back, pltpu.roll(tail, back, 0), xr[0:SUBLANE, :])
        y = y + cw_ref[w:w + 1, :] * jnp.concatenate([top, xr[SUBLANE:, :]], axis=0)
    tail_scr[...] = x[rows - SUBLANE:rows, :]
    return y


def _ssd_body(z_ref, xbc_ref, dtp_ref, cw_ref, cb_ref, dtb_ref, alog_ref, dsk_ref, ng_ref, expand_ref,
              y_ref, tail_scr, st_scr, *, q, d_ssd):
    step = pl.program_id(1)
    n = SSD_STATE
    gw = d_ssd // SSD_GROUPS
    rpg = gw // SSD_HEAD_DIM

    @pl.when(step == 0)
    def _():
        st_scr[...] = jnp.zeros(st_scr.shape, F32)

    xbc = _silu(_causal_conv(xbc_ref, cw_ref, cb_ref, tail_scr, step, q))
    xs = xbc[:, :d_ssd]
    bm = xbc[:, d_ssd:d_ssd + SSD_GROUPS * n]
    cm = xbc[:, d_ssd + SSD_GROUPS * n:]

    dt = _softplus(dtp_ref[...] + dtb_ref[...])
    adt = dt * (-jnp.exp(alog_ref[...]))
    ri = lax.broadcasted_iota(jnp.int32, (q, q), 0)
    ci = lax.broadcasted_iota(jnp.int32, (q, q), 1)
    causal = ri >= ci
    acum = _dot_sel_lhs(causal.astype(F32), adt)
    expand = expand_ref[...]
    a_x = _dot_sel_rhs(acum, expand)
    dt_x = _dot_sel_rhs(dt, expand)
    acum_t = acum.T
    a_last = a_x[q - 1:q, :]
    exp_a = jnp.exp(a_x)
    xdt = xs * dt_x
    xsw = xdt * jnp.exp(a_last - a_x)
    cdec = jnp.exp(a_last)

    ys = []
    for g in range(SSD_GROUPS):
        bm_g = bm[:, g * n:(g + 1) * n]
        cm_g = cm[:, g * n:(g + 1) * n].astype(BF16)
        cb = lax.dot_general(cm_g, bm_g.astype(BF16), (((1,), (1,)), ((), ())), preferred_element_type=F32)
        yd = []
        for r in range(rpg):
            h = g * rpg + r
            seg = acum[:, h:h + 1] - acum_t[h:h + 1, :]
            decay = jnp.exp(jnp.where(causal, seg, -jnp.inf))
            m = (cb * decay).astype(BF16)
            xh = xdt[:, h * SSD_HEAD_DIM:(h + 1) * SSD_HEAD_DIM].astype(BF16)
            yd.append(jnp.dot(m, xh, preferred_element_type=F32))
        yd = jnp.concatenate(yd, axis=1)
        gs = slice(g * gw, (g + 1) * gw)
        prev = st_scr[g]
        yoff = jnp.dot(cm_g, prev.astype(BF16), preferred_element_type=F32) * exp_a[:, gs]
        st = jnp.dot(bm_g.T.astype(BF16), xsw[:, gs].astype(BF16), preferred_element_type=F32)
        st_scr[g] = prev * cdec[:, gs] + st
        ys.append(yd + yoff + dsk_ref[:, gs] * xs[:, gs])

    z = z_ref[...]
    for g in range(SSD_GROUPS):
        gs = slice(g * gw, (g + 1) * gw)
        y_ref[:, gs] = _rms(ys[g] * _silu(z[:, gs]), ng_ref[:, gs])


def _ssd(z, xbc, dtp, conv_w, conv_b, dt_bias, a_log, d_skip, norm_g, *, bsz):
    t, d_ssd = z.shape
    cch = xbc.shape[1]
    q = SSD_CHUNK
    nc = t // bsz // q
    heads = d_ssd // SSD_HEAD_DIM
    pad = lambda v, fill: jnp.concatenate([v, jnp.full((LANE - heads,), fill, F32)]).reshape(1, LANE)
    row = lambda c: pl.BlockSpec((q, c), lambda b, i: (b * nc + i, 0))
    expand = (jnp.arange(d_ssd)[None, :] // SSD_HEAD_DIM == jnp.arange(LANE)[:, None]).astype(BF16)
    return pl.pallas_call(
        functools.partial(_ssd_body, q=q, d_ssd=d_ssd),
        grid=(bsz, nc),
        in_specs=[row(d_ssd), row(cch), row(LANE), _const_spec((CONV_W, cch)), _const_spec((1, cch)),
                  _const_spec((1, LANE)), _const_spec((1, LANE)), _const_spec((1, d_ssd)),
                  _const_spec((1, d_ssd)), _const_spec((LANE, d_ssd))],
        out_specs=row(d_ssd),
        out_shape=jax.ShapeDtypeStruct((t, d_ssd), F32),
        scratch_shapes=[pltpu.VMEM((SUBLANE, cch), F32),
                        pltpu.VMEM((SSD_GROUPS, SSD_STATE, d_ssd // SSD_GROUPS), F32)],
        compiler_params=_cparams("parallel", "arbitrary"),
        name="ssd",
    )(z, xbc, dtp, conv_w, conv_b.reshape(1, cch), pad(dt_bias, 0.0), pad(a_log, 0.0),
      jnp.repeat(d_skip, SSD_HEAD_DIM).reshape(1, d_ssd), norm_g.reshape(1, d_ssd), expand)


def _lru_body(xl_ref, gl_ref, cw_ref, cb_ref, wa_ref, ba_ref, wi_ref, bi_ref, lam_ref,
              y_ref, tail_scr, h_scr, *, tl):
    step = pl.program_id(1)

    @pl.when(step == 0)
    def _():
        h_scr[...] = jnp.zeros(h_scr.shape, F32)

    xr = _causal_conv(xl_ref, cw_ref, cb_ref, tail_scr, step, tl)
    xrb = xr.astype(BF16)
    r = jax.nn.sigmoid(jnp.dot(xrb, wa_ref[...], preferred_element_type=F32) + ba_ref[...])
    i = jax.nn.sigmoid(jnp.dot(xrb, wi_ref[...], preferred_element_type=F32) + bi_ref[...])
    log_a = -LRU_C * r * _softplus(-lam_ref[...])
    a = jnp.exp(log_a)
    b = jnp.sqrt(-jnp.tanh(log_a) * (a * a + 1.0)) * (i * xr)
    rows = lax.broadcasted_iota(jnp.int32, a.shape, 0) % SUBLANE
    s = 1
    while s < SUBLANE:
        keep = rows >= s
        a_sh = jnp.where(keep, pltpu.roll(a, s, 0), 1.0)
        b_sh = jnp.where(keep, pltpu.roll(b, s, 0), 0.0)
        b = a * b_sh + b
        a = a * a_sh
        s *= 2
    h = h_scr[0:1, :]
    gate = jax.nn.gelu(gl_ref[...])
    for g in range(tl // SUBLANE):
        grp = slice(g * SUBLANE, (g + 1) * SUBLANE)
        hs = a[grp, :] * h + b[grp, :]
        y_ref[grp, :] = hs * gate[grp, :]
        h = hs[SUBLANE - 1:SUBLANE, :]
    h_scr[...] = jnp.broadcast_to(h, h_scr.shape)


def _block_diag(w):
    nb, bw, _ = w.shape
    out = jnp.zeros((nb * bw, nb * bw), w.dtype)
    for k in range(nb):
        out = out.at[k * bw:(k + 1) * bw, k * bw:(k + 1) * bw].set(w[k])
    return out


def _lru(xl, gl, conv_w, conv_b, wa, ba, wi, bi, lam, *, bsz, tl=512):
    t, dl = xl.shape
    nt = t // bsz // tl
    row = pl.BlockSpec((tl, dl), lambda b, i: (b * nt + i, 0))
    vec = _const_spec((1, dl))
    return pl.pallas_call(
        functools.partial(_lru_body, tl=tl),
        grid=(bsz, nt),
        in_specs=[row, row, _const_spec((CONV_W, dl)), vec, _const_spec((dl, dl)), vec,
                  _const_spec((dl, dl)), vec, vec],
        out_specs=row,
        out_shape=jax.ShapeDtypeStruct((t, dl), F32),
        scratch_shapes=[pltpu.VMEM((SUBLANE, dl), F32), pltpu.VMEM((SUBLANE, dl), F32)],
        compiler_params=_cparams("parallel", "arbitrary"),
        name="rglru",
    )(xl, gl, conv_w, conv_b.reshape(1, dl), _block_diag(wa).astype(BF16), ba.reshape(1, dl),
      _block_diag(wi).astype(BF16), bi.reshape(1, dl), lam.reshape(1, dl))


def _rope(x, ct, st, axis):
    width = x.shape[axis]
    fm = lax.broadcasted_iota(jnp.int32, x.shape, axis) % ATT_HEAD_DIM
    half = ROPE_ROT // 2
    partner = jnp.where(fm < half, pltpu.roll(x, width - half, axis), pltpu.roll(x, half, axis))
    return x * ct + partner * st


def _dot_nt(a, b):
    return lax.dot_general(a, b, (((1,), (1,)), ((), ())), preferred_element_type=F32)


def _prep_body(cq_ref, ckv_ref, misc_ref, ct_ref, st_ref, ctt_ref, stt_ref, cqg_ref, ckvg_ref, wuqt_ref,
               wuk_ref, wuvt_ref, qg_ref, kg_ref, wqit_ref, kig_ref, hm_ref,
               k_ref, ki_ref, vt_ref, qt_ref, qit_ref, wc_ref, *, tm):
    ct = ct_ref[...]
    st = st_ref[...]
    d_att = wuk_ref.shape[1]
    head_mean = hm_ref[...]

    cqn = _rms(cq_ref[...], cqg_ref[...]).astype(BF16)
    ckvn = _rms(ckv_ref[...], ckvg_ref[...]).astype(BF16)

    qt = _dot_nt(wuqt_ref[...], cqn)
    ms = _dot_sel_lhs(head_mean, qt * qt)
    qt = qt * lax.rsqrt(ms + RMS_EPS) * qg_ref[...]
    qt_ref[...] = (_rope(qt, ctt_ref[...], stt_ref[...], 0) * (ATT_HEAD_DIM ** -0.5 * LOG2_E)).astype(BF16)
    vt = _dot_nt(wuvt_ref[...], ckvn).astype(BF16)
    for pr in range(ATT_HEADS // 2):
        base = pr * V_PAIR_ROWS
        vt_ref[base:base + 2 * ATT_HEAD_DIM, :] = vt[pr * 2 * ATT_HEAD_DIM:(pr + 1) * 2 * ATT_HEAD_DIM, :]
        vt_ref[base + 2 * ATT_HEAD_DIM:base + V_PAIR_ROWS, :] = jnp.ones((V_ONES_ROWS, tm), BF16)

    k = jnp.dot(ckvn, wuk_ref[...], preferred_element_type=F32)
    ms = _dot_sel_rhs(k * k, head_mean)
    k_ref[...] = _rope(k * lax.rsqrt(ms + RMS_EPS) * kg_ref[...], ct, st, 1).astype(BF16)

    qit = _rope(_dot_nt(wqit_ref[...], cqn), ctt_ref[...], stt_ref[...], 0)
    misc = misc_ref[...]
    misc_t = misc.T
    wscale = IDX_HEADS ** -0.5 * IDX_DIM ** -0.5
    for j in range(tm // Q_BLOCK):
        cols = slice(j * Q_BLOCK, (j + 1) * Q_BLOCK)
        for h in range(IDX_HEADS):
            qit_ref[j, 0:IDX_DIM, h * Q_BLOCK:(h + 1) * Q_BLOCK] = (
                qit[h * IDX_DIM:(h + 1) * IDX_DIM, cols].astype(BF16))
        qit_ref[j, IDX_DIM:, :] = jnp.zeros((LANE - IDX_DIM, IDX_HEADS * Q_BLOCK), BF16)
        wrow = jnp.concatenate([misc_t[IDX_DIM + h:IDX_DIM + h + 1, cols] for h in range(IDX_HEADS)], axis=1)
        wc_ref[j] = jnp.broadcast_to(wrow * wscale, (SUBLANE, IDX_HEADS * Q_BLOCK))

    lane = lax.broadcasted_iota(jnp.int32, misc.shape, 1)
    km = jnp.where(lane < IDX_DIM, misc, 0.0)
    ms = jnp.sum(km * km, axis=-1, keepdims=True) * (1.0 / IDX_DIM)
    kin = km * lax.rsqrt(ms + RMS_EPS) * kig_ref[...]
    ki_ref[...] = _rope(kin, ct[:, :LANE], st[:, :LANE], 1).astype(BF16)


def _rope_lane_tables(length, width):
    half = ROPE_ROT // 2
    inv = ROPE_THETA ** (-jnp.arange(half, dtype=F32) * 2.0 / ROPE_ROT)
    ang = jnp.arange(length, dtype=F32)[:, None] * inv[None, :]
    cos, sin = jnp.cos(ang), jnp.sin(ang)
    ones = jnp.ones((length, ATT_HEAD_DIM - ROPE_ROT), F32)
    ct = jnp.concatenate([cos, cos, ones], axis=1)
    st = jnp.concatenate([-sin, sin, 0.0 * ones], axis=1)
    reps = width // ATT_HEAD_DIM
    return jnp.tile(ct, (1, reps)), jnp.tile(st, (1, reps))


def _dsa_prep(cq, ckv, misc, p, *, bsz):
    t, qr = cq.shape
    kvr = ckv.shape[1]
    tm = KEY_CHUNK
    length = t // bsz
    nt = length // tm
    d_att = ATT_HEADS * ATT_HEAD_DIM
    ct, st = _rope_lane_tables(length, d_att)
    wukv = p["w_ukv"].reshape(kvr, ATT_HEADS, 2, ATT_HEAD_DIM)
    wuk = wukv[:, :, 0, :].reshape(kvr, d_att).astype(BF16)
    wuvt = wukv[:, :, 1, :].reshape(kvr, d_att).T.astype(BF16)
    kig = jnp.concatenate([p["kidx_norm"], jnp.zeros((LANE - IDX_DIM,), F32)]).reshape(1, LANE)
    head_of = jnp.arange(d_att) // ATT_HEAD_DIM
    head_mean = jnp.where(head_of[:, None] == head_of[None, :], 1.0 / ATT_HEAD_DIM, 0.0).astype(BF16)
    row = lambda c: pl.BlockSpec((tm, c), lambda i, b: (b * nt + i, 0))
    tab = pl.BlockSpec((tm, d_att), lambda i, b: (i, 0))
    tab_t = pl.BlockSpec((d_att, tm), lambda i, b: (0, i))
    nqb = tm // Q_BLOCK
    hq = IDX_HEADS * Q_BLOCK
    return pl.pallas_call(
        functools.partial(_prep_body, tm=tm),
        grid=(nt, bsz),
        in_specs=[row(qr), row(kvr), row(LANE), tab, tab, tab_t, tab_t, _const_spec((1, qr)),
                  _const_spec((1, kvr)), _const_spec((d_att, qr)), _const_spec((kvr, d_att)),
                  _const_spec((d_att, kvr)), _const_spec((d_att, 1)), _const_spec((1, d_att)),
                  _const_spec((IDX_HEADS * IDX_DIM, qr)), _const_spec((1, LANE)), _const_spec((d_att, d_att))],
        out_specs=[row(d_att), row(LANE),
                   pl.BlockSpec((None, V_ROWS, tm), lambda i, b: (b * nt + i, 0, 0)),
                   pl.BlockSpec((None, d_att, tm), lambda i, b: (b * nt + i, 0, 0)),
                   pl.BlockSpec((nqb, LANE, hq), lambda i, b: (b * nt + i, 0, 0)),
                   pl.BlockSpec((nqb, SUBLANE, hq), lambda i, b: (b * nt + i, 0, 0))],
        out_shape=[jax.ShapeDtypeStruct((t, d_att), BF16),
                   jax.ShapeDtypeStruct((t, LANE), BF16),
                   jax.ShapeDtypeStruct((bsz * nt, V_ROWS, tm), BF16),
                   jax.ShapeDtypeStruct((bsz * nt, d_att, tm), BF16),
                   jax.ShapeDtypeStruct((t // Q_BLOCK, LANE, hq), BF16),
                   jax.ShapeDtypeStruct((t // Q_BLOCK, SUBLANE, hq), F32)],
        compiler_params=_cparams("parallel", "parallel"),
        name="dsa_prep",
    )(cq, ckv, misc, ct, st, ct.T, st.T, p["cq_norm"].reshape(1, qr), p["ckv_norm"].reshape(1, kvr),
      p["w_uq"].T.astype(BF16), wuk, wuvt, jnp.tile(p["q_norm"], ATT_HEADS).reshape(d_att, 1),
      jnp.tile(p["k_norm"], ATT_HEADS).reshape(1, d_att), p["w_qidx"].T.astype(BF16), kig, head_mean)


INT_MIN = -2 ** 31
F32_TINY = 2.0 ** -126
F32_LOWEST = -3.4028234663852886e38
SEARCH_VALUE_STEPS = 18
SEARCH_CAP = 80


def _order_key_to_float(u):
    sk = u ^ jnp.int32(INT_MIN)
    fb = sk ^ ((sk >> 31) & jnp.int32(0x7FFFFFFF))
    return lax.bitcast_convert_type(fb, F32)


def _float_to_order_key(f):
    b = lax.bitcast_convert_type(f, jnp.int32)
    return (b ^ ((b >> 31) & jnp.int32(0x7FFFFFFF))) ^ jnp.int32(INT_MIN)


def _fold_rows(x, op):
    parts = [x[i * SUBLANE:(i + 1) * SUBLANE, :] for i in range(x.shape[0] // SUBLANE)]
    while len(parts) > 1:
        parts = [op(parts[i], parts[i + 1]) for i in range(0, len(parts), 2)]
    return parts[0]


def _dsa_body(qit_ref, wc_ref, qt_ref, ki_ref, k_ref, vt_ref, _prev_ref, o_ref,
              sc_scr, lg_scr, qbd_scr, oacc_scr, *, n_sel, nk):
    kc = KEY_CHUNK
    qb = (nk - 1) * (kc // Q_BLOCK) + pl.program_id(1)
    qpos = qb * Q_BLOCK + lax.broadcasted_iota(jnp.int32, (1, Q_BLOCK), 1)
    hq = IDX_HEADS * Q_BLOCK
    k_f = jnp.float32(n_sel)

    def chunk(c):
        return pl.ds(c * kc, kc)

    def over_chunks(body, carry):
        for c in range(nk):
            carry = body(c, carry)
        return carry

    qit = qit_ref[...]
    wc = wc_ref[0:1, :]

    def score_chunk(c, carry):
        amax, c_pos, c_nn = carry
        s4 = jnp.dot(ki_ref[chunk(c), :], qit, preferred_element_type=F32)
        s4 = jnp.maximum(s4, 0.0) * wc
        s = (s4[:, 0:Q_BLOCK] + s4[:, Q_BLOCK:2 * Q_BLOCK]) + (s4[:, 2 * Q_BLOCK:3 * Q_BLOCK] + s4[:, 3 * Q_BLOCK:])
        kpos = c * kc + lax.broadcasted_iota(jnp.int32, (kc, Q_BLOCK), 0)
        sm = jnp.where(kpos <= qpos, s, -jnp.inf)
        sc_scr[chunk(c), :] = sm
        return (jnp.maximum(amax, _fold_rows(jnp.abs(s), jnp.maximum)),
                c_pos + _fold_rows(jnp.where(sm >= F32_TINY, 1.0, 0.0), jnp.add),
                c_nn + _fold_rows(jnp.where(sm >= 0.0, 1.0, 0.0), jnp.add))

    zeros8 = jnp.zeros((SUBLANE, Q_BLOCK), F32)
    amax, c_pos, c_nn = over_chunks(score_chunk, (zeros8, zeros8, zeros8))
    amax = jnp.max(amax, axis=0, keepdims=True)
    c_pos = jnp.sum(c_pos, axis=0, keepdims=True)
    c_nn = jnp.sum(c_nn, axis=0, keepdims=True)

    def count_ge(*thrs):
        def body(c, accs):
            blk = sc_scr[chunk(c), :]
            return tuple(a + _fold_rows(jnp.where(blk >= t, 1.0, 0.0), jnp.add) for a, t in zip(accs, thrs))
        accs = over_chunks(body, tuple(jnp.zeros((SUBLANE, Q_BLOCK), F32) for _ in thrs))
        return tuple(jnp.sum(a, axis=0, keepdims=True) for a in accs)

    n_valid = (qpos + 1).astype(F32)
    take_all = n_valid <= k_f
    zero_tie = (c_pos < k_f) & (c_nn >= k_f)
    pos = c_pos >= k_f
    hi_top = _order_key_to_float(_float_to_order_key(amax) + 1)
    lo = jnp.where(take_all, F32_LOWEST, jnp.where(zero_tie, 0.0, jnp.where(pos, F32_TINY, -amax)))
    hi = jnp.where(take_all, jnp.inf, jnp.where(zero_tie, F32_TINY, jnp.where(pos, hi_top, 0.0)))
    cnt_lo = jnp.where(take_all | (~zero_tie & ~pos), n_valid, jnp.where(zero_tie, c_nn, c_pos))
    cnt_hi = jnp.where(take_all | (~zero_tie & pos), 0.0, jnp.where(zero_tie, c_pos, c_nn))
    settled = take_all | zero_tie

    def search_step(st, on_values):
        lk, hk, lo, hi, cnt_lo, cnt_hi = st
        midk = lk + ((hk - lk) >> 1)
        mid = _order_key_to_float(midk)
        if on_values:
            vmid = lo + (hi - lo) * 0.5
            inside = (vmid > lo) & (vmid < hi)
            mid = jnp.where(inside, vmid, mid)
            midk = jnp.where(inside, _float_to_order_key(vmid), midk)
        cnt, = count_ge(mid)
        ge = cnt >= k_f
        return (jnp.where(ge, midk, lk), jnp.where(ge, hk, midk), jnp.where(ge, mid, lo), jnp.where(ge, hi, mid),
                jnp.where(ge, cnt, cnt_lo), jnp.where(ge, cnt_hi, cnt))

    def search_cond(carry):
        it, (lk, hk, _, _, cnt_lo, _) = carry
        open_ = jnp.logical_not(settled | (cnt_lo == k_f) | (hk - lk <= 1))
        return (it < SEARCH_CAP) & (jnp.sum(jnp.where(open_, 1.0, 0.0)) > 0.0)

    def search_pair(carry):
        it, st = carry
        return it + 1, search_step(search_step(st, False), False)

    st = (_float_to_order_key(lo), _float_to_order_key(hi), lo, hi, cnt_lo, cnt_hi)
    st = lax.fori_loop(0, SEARCH_VALUE_STEPS, lambda _, s: search_step(s, True), st)
    _, (_, _, lo, hi, cnt_lo, cnt_hi) = lax.while_loop(search_cond, search_pair, (jnp.int32(0), st))
    need = k_f - cnt_hi

    sub = LANE
    ri = lax.broadcasted_iota(jnp.int32, (sub, sub), 0)
    ci = lax.broadcasted_iota(jnp.int32, (sub, sub), 1)
    before = (ri > ci).astype(BF16)

    qbd_scr[...] = jnp.zeros(qbd_scr.shape, BF16)
    for h in range(ATT_HEADS):
        rows = slice(h * ATT_HEAD_DIM, (h + 1) * ATT_HEAD_DIM)
        qbd_scr[rows, h * Q_BLOCK:(h + 1) * Q_BLOCK] = qt_ref[rows, :]
    qbd = qbd_scr[...]

    def logit_chunk(c, carry):
        m, seen = carry
        for j in range(kc // sub):
            rows = pl.ds(c * kc + j * sub, sub)
            blk = sc_scr[rows, :]
            in_hi = blk >= hi
            tie = (blk >= lo) & jnp.logical_not(in_hi)
            tie_f = jnp.where(tie, 1.0, 0.0)
            rank = jnp.dot(before, tie_f.astype(BF16), preferred_element_type=F32) + seen
            sel = in_hi | (tie & (rank < need))
            seen = seen + jnp.sum(tie_f, axis=0, keepdims=True)
            b = jnp.where(sel, 0.0, -jnp.inf)
            lg = jnp.dot(k_ref[rows, :], qbd, preferred_element_type=F32) + jnp.concatenate([b] * ATT_HEADS, axis=1)
            lg_scr[rows, :] = lg
            m = jnp.maximum(m, jnp.max(lg, axis=0, keepdims=True))
        return m, seen

    m, _ = over_chunks(logit_chunk, (jnp.full((1, hq), -jnp.inf, F32), jnp.zeros((1, Q_BLOCK), F32)))
    oacc_scr[...] = jnp.zeros(oacc_scr.shape, F32)

    pair_cols = 2 * Q_BLOCK

    def value_chunk(c, carry):
        depth = 2 * LANE
        for pr in range(ATT_HEADS // 2):
            rws = slice(pr * V_PAIR_ROWS, (pr + 1) * V_PAIR_ROWS)
            cls = slice(pr * pair_cols, (pr + 1) * pair_cols)
            acc = oacc_scr[rws, :]
            for j in range(kc // depth):
                rows = pl.ds(c * kc + j * depth, depth)
                p = jnp.exp2(lg_scr[rows, cls] - m[:, cls]).astype(BF16)
                acc = acc + jnp.dot(vt_ref[c, rws, j * depth:(j + 1) * depth], p, preferred_element_type=F32)
            oacc_scr[rws, :] = acc
        return carry

    over_chunks(value_chunk, 0)
    outs = []
    for h in range(ATT_HEADS):
        base = (h // 2) * V_PAIR_ROWS
        cols = slice((h % 2) * Q_BLOCK, (h % 2 + 1) * Q_BLOCK)
        ones_row = base + 2 * ATT_HEAD_DIM
        ssum = oacc_scr[ones_row:ones_row + 1, cols]
        outs.append(oacc_scr[base + (h % 2) * ATT_HEAD_DIM:base + (h % 2 + 1) * ATT_HEAD_DIM, cols] / ssum)
    o_ref[...] = jnp.concatenate(outs, axis=0).T


def _dsa(k, ki, vt, qt, qit, wc, *, bsz):
    t, d_att = k.shape
    length = t // bsz
    nb = length // Q_BLOCK
    kc = KEY_CHUNK
    nkc = length // kc
    hq = IDX_HEADS * Q_BLOCK
    per_q = kc // Q_BLOCK
    vt = vt.reshape(bsz, nkc, V_ROWS, kc)
    ki = ki.reshape(bsz, length, LANE)
    k = k.reshape(bsz, length, d_att)
    out = jnp.zeros((bsz, length, d_att), F32)
    for g in range(nkc):
        nk = g + 1
        qblk = lambda b, i, g=g: b * nb + g * per_q + i
        out = pl.pallas_call(
            functools.partial(_dsa_body, n_sel=min(TOP_K, length // 4), nk=nk),
            grid=(bsz, per_q),
            in_specs=[pl.BlockSpec((None, LANE, hq), lambda b, i, f=qblk: (f(b, i), 0, 0)),
                      pl.BlockSpec((None, SUBLANE, hq), lambda b, i, f=qblk: (f(b, i), 0, 0)),
                      pl.BlockSpec((None, d_att, Q_BLOCK), lambda b, i, g=g: (b * nkc + g, 0, i)),
                      pl.BlockSpec((None, nk * kc, LANE), lambda b, i: (b, 0, 0)),
                      pl.BlockSpec((None, nk * kc, d_att), lambda b, i: (b, 0, 0)),
                      pl.BlockSpec((None, nk, V_ROWS, kc), lambda b, i: (b, 0, 0, 0)),
                      pl.BlockSpec(memory_space=pl.ANY)],
            out_specs=pl.BlockSpec((None, Q_BLOCK, d_att), lambda b, i, g=g: (b, g * per_q + i, 0)),
            out_shape=jax.ShapeDtypeStruct((bsz, length, d_att), F32),
            input_output_aliases={6: 0},
            scratch_shapes=[pltpu.VMEM((nk * kc, Q_BLOCK), F32), pltpu.VMEM((nk * kc, hq), F32),
                            pltpu.VMEM((d_att, hq), BF16), pltpu.VMEM((V_ROWS, 2 * Q_BLOCK), F32)],
            compiler_params=_cparams("parallel", "arbitrary"),
            name=f"dsa{nk}",
        )(qit, wc, qt, ki, k, vt, out)
    return out.reshape(t, d_att)


def _pad_cols(w, width):
    return jnp.concatenate([w, jnp.zeros((w.shape[0], width - w.shape[1]), w.dtype)], axis=1)


def kernel(x, norm_ffn1, ffn1_w13, ffn1_w2, norm_mix, w_in, ssd_conv_w, ssd_conv_b, ssd_dt_bias, ssd_a_log, ssd_d, ssd_norm, cq_norm, ckv_norm, w_uq, w_ukv, q_norm, k_norm, w_qidx, kidx_norm, lru_conv_w, lru_conv_b, lru_wa, lru_ba, lru_wi, lru_bi, lru_lambda, w_out, norm_ffn2, ffn2_w13, ffn2_w2):
    bsz, length, d = x.shape
    depth = w_in.shape[0]
    d_ssd = ssd_norm.shape[1]
    conv_ch = ssd_conv_b.shape[1]
    heads = ssd_d.shape[1]
    q_rank, kv_rank = cq_norm.shape[1], ckv_norm.shape[1]
    d_lru = lru_lambda.shape[1]
    assert length % KEY_CHUNK == 0 and (bsz * length) % 512 == 0, (bsz, length)
    assert d_ssd == heads * SSD_HEAD_DIM and conv_ch == d_ssd + 2 * SSD_GROUPS * SSD_STATE, (d_ssd, conv_ch)
    assert w_uq.shape[2] == ATT_HEADS * ATT_HEAD_DIM and w_qidx.shape[2] == IDX_HEADS * IDX_DIM
    assert heads <= LANE and IDX_DIM + IDX_HEADS <= LANE and ssd_conv_w.shape[1] == CONV_W == lru_conv_w.shape[1]
    sizes = [d_ssd, conv_ch, heads, q_rank, kv_rank, IDX_DIM, IDX_HEADS, d_lru, d_lru]
    offs = [0]
    for s in sizes:
        offs.append(offs[-1] + s)
    widths = (d_ssd, conv_ch, LANE, q_rank, kv_rank, LANE, d_lru, d_lru)
    zero_rows = lambda n: jnp.zeros((n, d), BF16)

    xt = x.reshape(bsz * length, d)
    w13_1, w2_1, w13_2, w2_2 = (w.astype(BF16) for w in (ffn1_w13, ffn1_w2, ffn2_w13, ffn2_w2))
    for l in range(depth):
        wt = w_in[l].T.astype(BF16)
        w_pad = jnp.concatenate(
            [wt[offs[0]:offs[3]], zero_rows(LANE - heads), wt[offs[3]:offs[7]],
             zero_rows(LANE - IDX_DIM - IDX_HEADS), wt[offs[7]:offs[9]]], axis=0)
        xt, z, xbc, dtp, cq, ckv, misc, xl, gl = _ffn(xt, norm_ffn1[l], w13_1, w2_1, l,
                                                      proj=(norm_mix[l], w_pad, widths))
        y_ssd = _ssd(z, xbc, dtp, ssd_conv_w[l], ssd_conv_b[l], ssd_dt_bias[l], ssd_a_log[l], ssd_d[l],
                     ssd_norm[l], bsz=bsz)
        y_lru = _lru(xl, gl, lru_conv_w[l], lru_conv_b[l], lru_wa[l], lru_ba[l], lru_wi[l], lru_bi[l],
                     lru_lambda[l], bsz=bsz)
        prm = dict(cq_norm=cq_norm[l], ckv_norm=ckv_norm[l], w_uq=w_uq[l], w_ukv=w_ukv[l], q_norm=q_norm[l],
                   k_norm=k_norm[l], w_qidx=w_qidx[l], kidx_norm=kidx_norm[l])
        k, ki, vt, qt, qit, wc = _dsa_prep(cq, ckv, misc, prm, bsz=bsz)
        y_att = _dsa(k, ki, vt, qt, qit, wc, bsz=bsz)
        xt = _ffn(xt, norm_ffn2[l], w13_2, w2_2, l, mixed=([y_ssd, y_att, y_lru], w_out[l].astype(BF16)))
    return xt.reshape(bsz, length, d)
```
